```python
import jax, jax.numpy as jnp
from jax import lax
import numpy as np

D_MODEL = 1024
BATCH = 16
SEQ = 256
DEPTH = 4
DEC_BATCH = 4
DEC_SEQ = 2048
PAST_LEN = 512

GRID_W = 64
N_MIXERS = 3
NORM_EPS = 1e-6
NEG_INF = -1e30
ROPE_THETA = 10000.0
Q_BLOCK = 128

NA_HEADS = 16
NA_HEAD_DIM = 64
NA_WIN_ROWS = 8
NA_WIN_COLS = 16
MLA_HEADS = 16
MLA_Q_LORA = 512
MLA_KV_LORA = 256
MLA_NOPE = 64
MLA_ROPE = 32
MLA_V = 64
GQA_HEADS = 16
GQA_KV_HEADS = 4
GQA_GROUP = GQA_HEADS // GQA_KV_HEADS
GQA_HEAD_DIM = 64
GQA_WINDOW = 128
GQA_BLOCK = GQA_WINDOW
FFN_DIM = 2816
N_EXPERTS = 8
TOP_K = 2
EXPERT_DIM = 3584
N_NA = (DEPTH + 2) // 3
N_MLA = (DEPTH + 1) // 3
N_GQA = DEPTH // 3
N_DENSE = (DEPTH + 1) // 2
N_MOE = DEPTH // 2

kernel_name = 'hybrid_diffusion_prefix_trunk_step'


def linear(x, w):
    return jnp.einsum('btd,de->bte', x, w)


def rmsnorm(x, g):
    xf = x.astype(jnp.float32)
    y = xf * lax.rsqrt(jnp.mean(xf * xf, axis=-1, keepdims=True) + NORM_EPS)
    return (y * g.astype(jnp.float32)).astype(x.dtype)


def adaln(cond, w, b):
    m = jnp.einsum('nd,de->ne', jax.nn.silu(cond), w) + b
    return jnp.split(m[:, None, :], 6, axis=-1)


def modulate(h, shift, scale):
    return h * (1 + scale) + shift


def grid_angles(n_tokens, rot_dim):
    t = jnp.arange(n_tokens)
    n_freq = rot_dim // 4
    inv_freq = ROPE_THETA ** (-jnp.arange(n_freq, dtype=jnp.float32) / n_freq)
    row = (t // GRID_W).astype(jnp.float32)
    col = (t % GRID_W).astype(jnp.float32)
    return row[:, None] * inv_freq, col[:, None] * inv_freq


def _rotate(x, ang):
    cos = jnp.cos(ang)[None, :, None, :]
    sin = jnp.sin(ang)[None, :, None, :]
    x1, x2 = jnp.split(x.astype(jnp.float32), 2, axis=-1)
    return jnp.concatenate([x1 * cos - x2 * sin, x2 * cos + x1 * sin], axis=-1)


def axial_rope(x, ang_row, ang_col):
    half = x.shape[-1] // 2
    out = jnp.concatenate([_rotate(x[..., :half], ang_row), _rotate(x[..., half:], ang_col)], axis=-1)
    return out.astype(x.dtype)


def attn_probs(s, sink):
    if sink is None:
        return jax.nn.softmax(s, axis=-1)
    sk = sink.astype(jnp.float32)[:, :, None, None]
    m = jnp.maximum(jnp.max(s, axis=-1, keepdims=True), sk)
    p = jnp.exp(s - m)
    return p / (jnp.sum(p, axis=-1, keepdims=True) + jnp.exp(sk - m))


def dense_attention(q, k, v, sink):
    B, Tq, Hk, G, dk = q.shape
    nb = Tq // Q_BLOCK
    scale = dk ** -0.5
    qb = jnp.moveaxis(q.reshape(B, nb, Q_BLOCK, Hk, G, dk), 1, 0)

    def block(q_i):
        s = jnp.einsum('bqhgd,bkhd->bhgqk', q_i, k).astype(jnp.float32) * scale
        p = attn_probs(s, sink).astype(v.dtype)
        return jnp.einsum('bhgqk,bkhd->bqhgd', p, v)

    o = lax.map(block, qb)
    return jnp.moveaxis(o, 0, 1).reshape(B, Tq, Hk, G, v.shape[-1])


def na_project(h, w_qkv):
    B, T, _ = h.shape
    qkv = linear(h, w_qkv).reshape(B, T, 3, NA_HEADS, NA_HEAD_DIM)
    return qkv[:, :, 0], qkv[:, :, 1], qkv[:, :, 2]


def na_context(h, w_qkv, w_o):
    B, T, _ = h.shape
    q, k, v = na_project(h, w_qkv)
    o = dense_attention(q[:, :, :, None], k, v, None)
    return linear(o.reshape(B, T, NA_HEADS * NA_HEAD_DIM), w_o), (k, v)


def na_latent(h, w_qkv, w_o, rel_bias, k_ctx, v_ctx):
    B, T, _ = h.shape
    rows = T // GRID_W
    kr = min(NA_WIN_ROWS, rows)
    q, k, v = na_project(h, w_qkv)
    q = q.reshape(B, rows, GRID_W, NA_HEADS, NA_HEAD_DIM)
    k = k.reshape(B, rows, GRID_W, NA_HEADS, NA_HEAD_DIM)
    v = v.reshape(B, rows, GRID_W, NA_HEADS, NA_HEAD_DIM)
    scale = NA_HEAD_DIM ** -0.5
    cols = jnp.arange(GRID_W)
    col_start = jnp.clip(cols - NA_WIN_COLS // 2, 0, GRID_W - NA_WIN_COLS)
    col_ok = (cols[None, :] >= col_start[:, None]) & (cols[None, :] < col_start[:, None] + NA_WIN_COLS)
    dc = jnp.clip(cols[None, :] - cols[:, None] + NA_WIN_COLS - 1, 0, 2 * NA_WIN_COLS - 2)
    bias_cols = rel_bias.astype(jnp.float32)[:, :, dc]
    n_loc = kr * GRID_W

    def row_block(args):
        q_r, r = args
        start = jnp.clip(r - kr // 2, 0, rows - kr)
        k_r = lax.dynamic_slice_in_dim(k, start, kr, axis=1).reshape(B, n_loc, NA_HEADS, NA_HEAD_DIM)
        v_r = lax.dynamic_slice_in_dim(v, start, kr, axis=1).reshape(B, n_loc, NA_HEADS, NA_HEAD_DIM)
        dr = start + jnp.arange(kr) - r + NA_WIN_ROWS - 1
        bias = jnp.where(col_ok[None, None], bias_cols[:, dr], NEG_INF)
        bias = bias.transpose(0, 2, 1, 3).reshape(NA_HEADS, GRID_W, n_loc)
        s_loc = jnp.einsum('bqhd,bkhd->bhqk', q_r, k_r).astype(jnp.float32) * scale + bias
        s_ctx = jnp.einsum('bqhd,bchd->bhqc', q_r, k_ctx).astype(jnp.float32) * scale
        p = jax.nn.softmax(jnp.concatenate([s_loc, s_ctx], axis=-1), axis=-1).astype(v.dtype)
        return (jnp.einsum('bhqk,bkhd->bqhd', p[..., :n_loc], v_r)
                + jnp.einsum('bhqc,bchd->bqhd', p[..., n_loc:], v_ctx))

    o = lax.map(row_block, (jnp.moveaxis(q, 1, 0), jnp.arange(rows)))
    o = jnp.moveaxis(o, 0, 1).reshape(B, T, NA_HEADS * NA_HEAD_DIM)
    return linear(o, w_o)


def mla_project(h, wq_a, q_norm, wq_b, wkv_a, kv_norm):
    B, T, _ = h.shape
    q = linear(rmsnorm(linear(h, wq_a), q_norm), wq_b).reshape(B, T, MLA_HEADS, MLA_NOPE + MLA_ROPE)
    kv = linear(h, wkv_a)
    return q, rmsnorm(kv[..., :MLA_KV_LORA], kv_norm), kv[..., MLA_KV_LORA:]


def mla_expand(ckv, kpe, wkv_b):
    B, T, _ = ckv.shape
    kv = linear(ckv, wkv_b).reshape(B, T, MLA_HEADS, MLA_NOPE + MLA_V)
    k_pe = jnp.broadcast_to(kpe[:, :, None, :], (B, T, MLA_HEADS, MLA_ROPE))
    return jnp.concatenate([kv[..., :MLA_NOPE], k_pe], axis=-1), kv[..., MLA_NOPE:]


def mla_context(h, wq_a, q_norm, wq_b, wkv_a, kv_norm, wkv_b, w_o):
    B, T, _ = h.shape
    q, ckv, kpe = mla_project(h, wq_a, q_norm, wq_b, wkv_a, kv_norm)
    k, v = mla_expand(ckv, kpe, wkv_b)
    o = dense_attention(q[:, :, :, None], k, v, None)
    return linear(o.reshape(B, T, MLA_HEADS * MLA_V), w_o), (ckv, kpe)


def mla_latent(h, wq_a, q_norm, wq_b, wkv_a, kv_norm, wkv_b, w_o, ckv_ctx, kpe_ctx):
    B, T, _ = h.shape
    q, ckv, kpe = mla_project(h, wq_a, q_norm, wq_b, wkv_a, kv_norm)
    ang_r, ang_c = grid_angles(T, MLA_ROPE)
    q = jnp.concatenate([q[..., :MLA_NOPE], axial_rope(q[..., MLA_NOPE:], ang_r, ang_c)], axis=-1)
    kpe = axial_rope(kpe[:, :, None, :], ang_r, ang_c)[:, :, 0]
    k_lat, v_lat = mla_expand(ckv, kpe, wkv_b)
    k_ctx, v_ctx = mla_expand(ckv_ctx, kpe_ctx, wkv_b)
    k = jnp.concatenate([k_ctx, k_lat], axis=1)
    v = jnp.concatenate([v_ctx, v_lat], axis=1)
    o = dense_attention(q[:, :, :, None], k, v, None)
    return linear(o.reshape(B, T, MLA_HEADS * MLA_V), w_o)


def gqa_project(h, w_qkv):
    B, T, _ = h.shape
    qkv = linear(h, w_qkv)
    nq, nk = GQA_HEADS * GQA_HEAD_DIM, GQA_KV_HEADS * GQA_HEAD_DIM
    q = qkv[..., :nq].reshape(B, T, GQA_HEADS, GQA_HEAD_DIM)
    k = qkv[..., nq:nq + nk].reshape(B, T, GQA_KV_HEADS, GQA_HEAD_DIM)
    v = qkv[..., nq + nk:].reshape(B, T, GQA_KV_HEADS, GQA_HEAD_DIM)
    return q, k, v


def gqa_context(h, w_qkv, w_o, sink):
    B, T, _ = h.shape
    q, k, v = gqa_project(h, w_qkv)
    q = q.reshape(B, T, GQA_KV_HEADS, GQA_GROUP, GQA_HEAD_DIM)
    o = dense_attention(q, k, v, sink.reshape(GQA_KV_HEADS, GQA_GROUP))
    return linear(o.reshape(B, T, GQA_HEADS * GQA_HEAD_DIM), w_o), (k, v)


def gqa_latent(h, w_qkv, w_o, sink, k_ctx, v_ctx):
    B, T, _ = h.shape
    q, k, v = gqa_project(h, w_qkv)
    ang_r, ang_c = grid_angles(T, GQA_HEAD_DIM)
    q = axial_rope(q, ang_r, ang_c).reshape(B, T, GQA_KV_HEADS, GQA_GROUP, GQA_HEAD_DIM)
    k = axial_rope(k, ang_r, ang_c)
    nb = T // GQA_BLOCK
    pad = ((0, 0), (GQA_BLOCK, GQA_BLOCK), (0, 0), (0, 0))
    kp, vp = jnp.pad(k, pad), jnp.pad(v, pad)
    qb = jnp.moveaxis(q.reshape(B, nb, GQA_BLOCK, GQA_KV_HEADS, GQA_GROUP, GQA_HEAD_DIM), 1, 0)
    n_loc = 3 * GQA_BLOCK
    kk = jnp.arange(n_loc)
    rel = kk[None, :] - GQA_BLOCK - jnp.arange(GQA_BLOCK)[:, None]
    band = jnp.abs(rel) <= GQA_WINDOW
    sink_hg = sink.reshape(GQA_KV_HEADS, GQA_GROUP)
    scale = GQA_HEAD_DIM ** -0.5

    def block(args):
        q_i, b = args
        k_i = lax.dynamic_slice_in_dim(kp, b * GQA_BLOCK, n_loc, axis=1)
        v_i = lax.dynamic_slice_in_dim(vp, b * GQA_BLOCK, n_loc, axis=1)
        kpos = (b - 1) * GQA_BLOCK + kk
        ok = band & ((kpos >= 0) & (kpos < T))[None, :]
        s_loc = jnp.einsum('bqhgd,bkhd->bhgqk', q_i, k_i).astype(jnp.float32) * scale
        s_loc = jnp.where(ok, s_loc, NEG_INF)
        s_ctx = jnp.einsum('bqhgd,bchd->bhgqc', q_i, k_ctx).astype(jnp.float32) * scale
        p = attn_probs(jnp.concatenate([s_loc, s_ctx], axis=-1), sink_hg).astype(v.dtype)
        return (jnp.einsum('bhgqk,bkhd->bqhgd', p[..., :n_loc], v_i)
                + jnp.einsum('bhgqc,bchd->bqhgd', p[..., n_loc:], v_ctx))

    o = lax.map(block, (qb, jnp.arange(nb)))
    o = jnp.moveaxis(o, 0, 1).reshape(B, T, GQA_HEADS * GQA_HEAD_DIM)
    return linear(o, w_o)


def swiglu(h, w_gate, w_up, w_down):
    return linear(jax.nn.silu(linear(h, w_gate)) * linear(h, w_up), w_down)


def moe_swiglu(h, router, w_gate, w_up, w_down):
    logits = linear(h, router).astype(jnp.float32)
    top_v, top_i = lax.top_k(logits, TOP_K)
    top_w = jax.nn.softmax(top_v, axis=-1)
    gates = jnp.sum(jax.nn.one_hot(top_i, N_EXPERTS, dtype=jnp.float32) * top_w[..., None], axis=-2).astype(h.dtype)
    out = jnp.zeros_like(h)
    for e in range(N_EXPERTS):
        out = out + gates[..., e:e + 1] * swiglu(h, w_gate[e], w_up[e], w_down[e])
    return out


def token_mixer(i, h, P, cache):
    kind, s = i % N_MIXERS, i // N_MIXERS
    if kind == 0:
        w = (P['na_w_qkv'][s], P['na_w_o'][s])
        if cache is None:
            return na_context(h, *w)
        return na_latent(h, *w, P['na_rel_bias'][s], *cache), ()
    if kind == 1:
        w = (P['mla_wq_a'][s], P['mla_q_norm'][s], P['mla_wq_b'][s], P['mla_wkv_a'][s],
             P['mla_kv_norm'][s], P['mla_wkv_b'][s], P['mla_w_o'][s])
        if cache is None:
            return mla_context(h, *w)
        return mla_latent(h, *w, *cache), ()
    w = (P['gqa_w_qkv'][s], P['gqa_w_o'][s], P['gqa_sink'][s])
    if cache is None:
        return gqa_context(h, *w)
    return gqa_latent(h, *w, *cache), ()


def channel_mixer(i, h, P):
    s = i // 2
    if i % 2 == 0:
        return swiglu(h, P['ffn_w_gate'][s], P['ffn_w_up'][s], P['ffn_w_down'][s])
    return moe_swiglu(h, P['moe_router'][s], P['moe_w_gate'][s], P['moe_w_up'][s], P['moe_w_down'][s])


def run_trunk(x, cond, caches, P):
    new_state = []
    for i in range(DEPTH):
        sh_m, sc_m, g_m, sh_f, sc_f, g_f = adaln(cond, P['ada_w'][i], P['ada_b'][i])
        h = modulate(rmsnorm(x, P['norm_mix'][i]), sh_m, sc_m)
        o, st = token_mixer(i, h, P, None if caches is None else caches[i])
        x = x + g_m * o
        h = modulate(rmsnorm(x, P['norm_ffn'][i]), sh_f, sc_f)
        x = x + g_f * channel_mixer(i, h, P)
        new_state.extend(st)
    return rmsnorm(x, P['norm_final']), new_state


def setup_inputs(seed: int = 0) -> dict:
    key = jax.random.key(seed)
    ks = iter(jax.random.split(key, 64))
    f32 = jnp.float32

    def nrm(shape, scale=1.0):
        return scale * jax.random.normal(next(ks), shape, f32)

    def lin(shape):
        return nrm(shape, shape[-2] ** -0.5)

    def gain(shape):
        return 1.0 + nrm(shape, 0.05)

    D = D_MODEL
    na_kv = (DEC_BATCH, PAST_LEN, NA_HEADS, NA_HEAD_DIM)
    gqa_kv = (DEC_BATCH, PAST_LEN, GQA_KV_HEADS, GQA_HEAD_DIM)
    return {
        'x_prompt': nrm((BATCH, SEQ, D)),
        'x_sample': nrm((DEC_BATCH, DEC_SEQ, D)),
        'cache_l0_k': nrm(na_kv),
        'cache_l0_v': nrm(na_kv),
        'cache_l1_ckv': nrm((DEC_BATCH, PAST_LEN, MLA_KV_LORA)),
        'cache_l1_kpe': nrm((DEC_BATCH, PAST_LEN, MLA_ROPE)),
        'cache_l2_k': nrm(gqa_kv),
        'cache_l2_v': nrm(gqa_kv),
        'cache_l3_k': nrm(na_kv),
        'cache_l3_v': nrm(na_kv),
        'c': nrm((DEC_BATCH, D)),
        'c_ctx': nrm((D,)),
        'ada_w': nrm((DEPTH, D, 6 * D), 0.5 * D ** -0.5),
        'ada_b': nrm((DEPTH, 6 * D), 0.02),
        'norm_mix': gain((DEPTH, D)),
        'norm_ffn': gain((DEPTH, D)),
        'norm_final': gain((D,)),
        'na_w_qkv': lin((N_NA, D, 3 * NA_HEADS * NA_HEAD_DIM)),
        'na_w_o': lin((N_NA, NA_HEADS * NA_HEAD_DIM, D)),
        'na_rel_bias': nrm((N_NA, NA_HEADS, 2 * NA_WIN_ROWS - 1, 2 * NA_WIN_COLS - 1), 0.1),
        'mla_wq_a': lin((N_MLA, D, MLA_Q_LORA)),
        'mla_q_norm': gain((N_MLA, MLA_Q_LORA)),
        'mla_wq_b': lin((N_MLA, MLA_Q_LORA, MLA_HEADS * (MLA_NOPE + MLA_ROPE))),
        'mla_wkv_a': lin((N_MLA, D, MLA_KV_LORA + MLA_ROPE)),
        'mla_kv_norm': gain((N_MLA, MLA_KV_LORA)),
        'mla_wkv_b': lin((N_MLA, MLA_KV_LORA, MLA_HEADS * (MLA_NOPE + MLA_V))),
        'mla_w_o': lin((N_MLA, MLA_HEADS * MLA_V, D)),
        'gqa_w_qkv': lin((N_GQA, D, (GQA_HEADS + 2 * GQA_KV_HEADS) * GQA_HEAD_DIM)),
        'gqa_w_o': lin((N_GQA, GQA_HEADS * GQA_HEAD_DIM, D)),
        'gqa_sink': nrm((N_GQA, GQA_HEADS), 0.5),
        'ffn_w_gate': lin((N_DENSE, D, FFN_DIM)),
        'ffn_w_up': lin((N_DENSE, D, FFN_DIM)),
        'ffn_w_down': lin((N_DENSE, FFN_DIM, D)),
        'moe_router': lin((N_MOE, D, N_EXPERTS)),
        'moe_w_gate': lin((N_MOE, N_EXPERTS, D, EXPERT_DIM)),
        'moe_w_up': lin((N_MOE, N_EXPERTS, D, EXPERT_DIM)),
        'moe_w_down': lin((N_MOE, N_EXPERTS, EXPERT_DIM, D)),
    }


def reference(x_prompt, x_sample, cache_l0_k, cache_l0_v, cache_l1_ckv, cache_l1_kpe,
              cache_l2_k, cache_l2_v, cache_l3_k, cache_l3_v, c, c_ctx,
              ada_w, ada_b, norm_mix, norm_ffn, norm_final,
              na_w_qkv, na_w_o, na_rel_bias,
              mla_wq_a, mla_q_norm, mla_wq_b, mla_wkv_a, mla_kv_norm, mla_wkv_b, mla_w_o,
              gqa_w_qkv, gqa_w_o, gqa_sink,
              ffn_w_gate, ffn_w_up, ffn_w_down,
              moe_router, moe_w_gate, moe_w_up, moe_w_down):
    P = dict(ada_w=ada_w, ada_b=ada_b, norm_mix=norm_mix, norm_ffn=norm_ffn, norm_final=norm_final,
             na_w_qkv=na_w_qkv, na_w_o=na_w_o, na_rel_bias=na_rel_bias,
             mla_wq_a=mla_wq_a, mla_q_norm=mla_q_norm, mla_wq_b=mla_wq_b, mla_wkv_a=mla_wkv_a,
             mla_kv_norm=mla_kv_norm, mla_wkv_b=mla_wkv_b, mla_w_o=mla_w_o,
             gqa_w_qkv=gqa_w_qkv, gqa_w_o=gqa_w_o, gqa_sink=gqa_sink,
             ffn_w_gate=ffn_w_gate, ffn_w_up=ffn_w_up, ffn_w_down=ffn_w_down,
             moe_router=moe_router, moe_w_gate=moe_w_gate, moe_w_up=moe_w_up, moe_w_down=moe_w_down)
    y_prompt, new_state = run_trunk(x_prompt, c_ctx[None, :], None, P)
    caches = [(cache_l0_k, cache_l0_v), (cache_l1_ckv, cache_l1_kpe),
              (cache_l2_k, cache_l2_v), (cache_l3_k, cache_l3_v)]
    y_sample, _ = run_trunk(x_sample, c, caches, P)
    return (y_prompt, y_sample, *new_state)
```

```python
import functools
import math

import numpy as np
import jax
import jax.numpy as jnp
from jax import lax
from jax.experimental import pallas as pl
from jax.experimental.pallas import tpu as pltpu

F32 = jnp.float32
BF16 = jnp.bfloat16

D_MODEL = 1024
BATCH = 16
SEQ = 256
DEPTH = 4
DEC_BATCH = 4
DEC_SEQ = 2048
PAST_LEN = 512
GRID_W = 64
GRID_ROWS = DEC_SEQ // GRID_W
NORM_EPS = 1e-6
NEG_INF = -1e30
ROPE_THETA = 10000.0
NA_HEADS = 16
NA_HEAD_DIM = 64
NA_WIN_ROWS = 8
NA_WIN_COLS = 16
MLA_HEADS = 16
MLA_Q_LORA = 512
MLA_KV_LORA = 256
MLA_NOPE = 64
MLA_ROPE = 32
MLA_V = 64
GQA_HEADS = 16
GQA_KV_HEADS = 4
GQA_HEAD_DIM = 64
GQA_WINDOW = 128
FFN_DIM = 2816
N_EXPERTS = 8
EXPERT_DIM = 3584

N_PROMPT = BATCH * SEQ
N_SAMPLE = DEC_BATCH * DEC_SEQ
N_TOK = N_PROMPT + N_SAMPLE
N_COND = 8

LANES = 128
SUBLANES = 8
VMEM_LIMIT = 56 * 1024 * 1024

ROW_TILE = 512
FFN_CHUNK = 256
N_FFN_CHUNKS = FFN_DIM // FFN_CHUNK
EXP_TILE = 1024
EXP_CHUNK = 512
N_EXP_CHUNKS = EXPERT_DIM // EXP_CHUNK
N_EXP_TILES = (2 * N_TOK) // EXP_TILE + N_EXPERTS
DISPATCH_TILE = 512
COMBINE_TILE = 256
NA_QROWS = 4
NA_UNION = NA_QROWS + NA_WIN_ROWS
MLA_KEYS = PAST_LEN + DEC_SEQ


def _cparams(sem):
    return pltpu.CompilerParams(dimension_semantics=sem, vmem_limit_bytes=VMEM_LIMIT)


def _cond_of_row(row):
    return jnp.where(row < N_PROMPT, 0, 1 + (row - N_PROMPT) // DEC_SEQ)


def _nt_dot(a, b):
    return lax.dot_general(a, b, (((1,), (1,)), ((), ())), preferred_element_type=F32)


def _dot(a, b):
    return jnp.dot(a, b, preferred_element_type=F32)


def _rms(x, gain):
    return x * lax.rsqrt(jnp.mean(x * x, axis=-1, keepdims=True) + NORM_EPS) * gain


def _silu(x):
    return x / (1.0 + jnp.exp(-x))


def _split_bf16(x):
    hi = x.astype(BF16)
    lo = (x - hi.astype(F32)).astype(BF16)
    return hi, lo


def _resident(shape):
    nd = len(shape)
    return pl.BlockSpec(shape, lambda *_: (0,) * nd, pipeline_mode=pl.Buffered(1))


def _mod_spec(tile):
    return pl.BlockSpec((1, 1, D_MODEL), lambda i: (_cond_of_row(i * tile), 0, 0))


ADA_TILE = 1536


def _adaln_kernel(c_ref, w_ref, b_ref, o_ref):
    a_hi, a_lo = _split_bf16(_silu(c_ref[...]))
    w_hi, w_lo = _split_bf16(w_ref[0])
    o_ref[0] = _dot(a_hi, w_hi) + _dot(a_lo, w_hi) + _dot(a_hi, w_lo) + b_ref[0]


def _adaln(cond, ada_w, ada_b):
    n = 6 * D_MODEL
    return pl.pallas_call(
        _adaln_kernel,
        grid=(DEPTH, n // ADA_TILE),
        in_specs=[
            pl.BlockSpec((N_COND, D_MODEL), lambda l, j: (0, 0)),
            pl.BlockSpec((1, D_MODEL, ADA_TILE), lambda l, j: (l, 0, j)),
            pl.BlockSpec((1, 1, ADA_TILE), lambda l, j: (l, 0, j)),
        ],
        out_specs=pl.BlockSpec((1, N_COND, ADA_TILE), lambda l, j: (l, 0, j)),
        out_shape=jax.ShapeDtypeStruct((DEPTH, N_COND, n), F32),
        compiler_params=_cparams(("arbitrary", "arbitrary")),
    )(cond, ada_w, ada_b.reshape(DEPTH, 1, n))


PROJ_COLS = 256


def _norm_proj_kernel(x_ref, g_ref, sh_ref, sc_ref, w_ref, *rest, n_out, n_rope):
    if n_rope:
        w2_ref, cos_ref, sin_ref, o_ref = rest
    else:
        (o_ref,) = rest
    h = _rms(x_ref[...], g_ref[...]) * (1.0 + sc_ref[0]) + sh_ref[0]
    hb = h.astype(BF16)
    for j in range(n_out // PROJ_COLS):
        cols = slice(j * PROJ_COLS, (j + 1) * PROJ_COLS)
        y = _dot(hb, w_ref[:, cols])
        if j * PROJ_COLS < n_rope:
            ys = _dot(hb, w2_ref[:, cols])
            reps = PROJ_COLS // LANES
            cos = jnp.concatenate([cos_ref[...]] * reps, axis=1)
            sin = jnp.concatenate([sin_ref[...]] * reps, axis=1)
            y = y * cos + ys * sin
        o_ref[:, cols] = y


def _rope_block(i):
    row = i * ROW_TILE
    return jnp.where(row < N_PROMPT, DEC_SEQ // ROW_TILE, ((row - N_PROMPT) % DEC_SEQ) // ROW_TILE)


def _norm_proj(x, gain, shift, scale, w, rope=None):
    n_out = w.shape[1]
    in_specs = [
        pl.BlockSpec((ROW_TILE, D_MODEL), lambda i: (i, 0)),
        pl.BlockSpec((1, D_MODEL), lambda i: (0, 0)),
        _mod_spec(ROW_TILE),
        _mod_spec(ROW_TILE),
        _resident(w.shape),
    ]
    args = [x, gain, shift, scale, w]
    n_rope = 0
    if rope is not None:
        w2, cos, sin = rope
        n_rope = w2.shape[1]
        in_specs += [
            _resident(w2.shape),
            pl.BlockSpec((ROW_TILE, LANES), lambda i: (_rope_block(i), 0)),
            pl.BlockSpec((ROW_TILE, LANES), lambda i: (_rope_block(i), 0)),
        ]
        args += [w2, cos, sin]
    return pl.pallas_call(
        functools.partial(_norm_proj_kernel, n_out=n_out, n_rope=n_rope),
        grid=(N_TOK // ROW_TILE,),
        in_specs=in_specs,
        out_specs=pl.BlockSpec((ROW_TILE, n_out), lambda i: (i, 0)),
        out_shape=jax.ShapeDtypeStruct((N_TOK, n_out), F32),
        compiler_params=_cparams(("arbitrary",)),
    )(*args)


MLA_Q_COLS = 2 * MLA_HEADS * MLA_NOPE


def _mla_proj_kernel(x_ref, g_ref, sh_ref, sc_ref, wa_ref, qn_ref, kvn_ref, wb_ref, wbs_ref,
                     cos_ref, sin_ref, q_ref, ckv_ref, kpe_ref):
    h = _rms(x_ref[...], g_ref[...]) * (1.0 + sc_ref[0]) + sh_ref[0]
    a = _dot(h.astype(BF16), wa_ref[...])
    c0 = MLA_Q_LORA
    c1 = c0 + MLA_KV_LORA
    cos = cos_ref[...]
    sin = sin_ref[...]
    ckv_ref[...] = _rms(a[:, c0:c1], kvn_ref[...])
    kpe_ref[...] = a[:, c1:c1 + LANES] * cos + a[:, c1 + LANES:c1 + 2 * LANES] * sin
    qn = _rms(a[:, :c0], qn_ref[...]).astype(BF16)
    half = MLA_Q_COLS // 2
    q_ref[:, :half] = _dot(qn, wb_ref[:, :half]).astype(BF16)
    q_rope = _dot(qn, wb_ref[:, half:])
    q_swap = _dot(qn, wbs_ref[...])
    reps = half // LANES
    cos_r = jnp.concatenate([cos] * reps, axis=1)
    sin_r = jnp.concatenate([sin] * reps, axis=1)
    q_ref[:, half:] = (q_rope * cos_r + q_swap * sin_r).astype(BF16)


def _mla_proj(x, gain, shift, scale, wa, q_norm, kv_norm, wb, wbs, cos, sin):
    return pl.pallas_call(
        _mla_proj_kernel,
        grid=(N_TOK // ROW_TILE,),
        in_specs=[
            pl.BlockSpec((ROW_TILE, D_MODEL), lambda i: (i, 0)),
            pl.BlockSpec((1, D_MODEL), lambda i: (0, 0)),
            _mod_spec(ROW_TILE),
            _mod_spec(ROW_TILE),
            _resident(wa.shape),
            pl.BlockSpec((1, MLA_Q_LORA), lambda i: (0, 0)),
            pl.BlockSpec((1, MLA_KV_LORA), lambda i: (0, 0)),
            _resident(wb.shape),
            _resident(wbs.shape),
            pl.BlockSpec((ROW_TILE, LANES), lambda i: (_rope_block(i), 0)),
            pl.BlockSpec((ROW_TILE, LANES), lambda i: (_rope_block(i), 0)),
        ],
        out_specs=[
            pl.BlockSpec((ROW_TILE, MLA_Q_COLS), lambda i: (i, 0)),
            pl.BlockSpec((ROW_TILE, MLA_KV_LORA), lambda i: (i, 0)),
            pl.BlockSpec((ROW_TILE, LANES), lambda i: (i, 0)),
        ],
        out_shape=[
            jax.ShapeDtypeStruct((N_TOK, MLA_Q_COLS), BF16),
            jax.ShapeDtypeStruct((N_TOK, MLA_KV_LORA), F32),
            jax.ShapeDtypeStruct((N_TOK, LANES), F32),
        ],
        compiler_params=_cparams(("arbitrary",)),
    )(x, gain, shift, scale, wa, q_norm, kv_norm, wb, wbs, cos, sin)


def _rows_matmul_kernel(x_ref, w_ref, o_ref):
    o_ref[...] = _dot(x_ref[...].astype(BF16), w_ref[...]).astype(o_ref.dtype)


def _rows_matmul(x, w, out_dtype):
    rows, k = x.shape
    n = w.shape[1]
    return pl.pallas_call(
        _rows_matmul_kernel,
        grid=(rows // ROW_TILE,),
        in_specs=[pl.BlockSpec((ROW_TILE, k), lambda i: (i, 0)), _resident(w.shape)],
        out_specs=pl.BlockSpec((ROW_TILE, n), lambda i: (i, 0)),
        out_shape=jax.ShapeDtypeStruct((rows, n), out_dtype),
        compiler_params=_cparams(("arbitrary",)),
    )(x, w)


def _pair_attention(q, masks, keys, values, biases, q_scale, s_scale, sinks):
    outs = []
    for h in range(2):
        qh = jnp.where(masks[h], q, jnp.zeros_like(q))
        if q_scale != 1.0:
            qh = qh * q_scale
        qh = qh.astype(BF16)
        scores = []
        for k, bias in zip(keys, biases):
            s = _nt_dot(qh, k)
            if s_scale != 1.0:
                s = s * s_scale
            if bias is not None:
                s = s + bias(h)
            scores.append(s)
        m = functools.reduce(jnp.maximum, [jnp.max(s, axis=-1, keepdims=True) for s in scores])
        if sinks is not None:
            m = jnp.maximum(m, sinks[h])
        probs = [jnp.exp(s - m) for s in scores]
        denom = functools.reduce(lambda a, b: a + b, [jnp.sum(p, axis=-1, keepdims=True) for p in probs])
        if sinks is not None:
            denom = denom + jnp.exp(sinks[h] - m)
        o = functools.reduce(lambda a, b: a + b, [_dot(p.astype(BF16), v) for p, v in zip(probs, values)])
        outs.append(o / denom)
    lane = lax.broadcasted_iota(jnp.int32, (1, LANES), 1)
    return jnp.where(lane < LANES // 2, outs[0], outs[1])


def _half_masks(width=LANES):
    lane = lax.broadcasted_iota(jnp.int32, (1, width), 1)
    return [lane < LANES // 2, lane >= LANES // 2]


HEAD_SCALE = NA_HEAD_DIM ** -0.5


def _ctx_attn_kernel(sink_ref, q_ref, k_ref, v_ref, o_ref, *, kv_blocks, use_sink):
    masks = _half_masks()
    n_q_blocks = o_ref.shape[1] // LANES
    for qb in range(n_q_blocks):
        kb = qb * kv_blocks // n_q_blocks
        cols = slice(kb * LANES, (kb + 1) * LANES)
        sinks = None
        if use_sink:
            base = (qb // 4) * 8 + qb % 4
            sinks = [sink_ref[base], sink_ref[base + 4]]
        o_ref[:, qb * LANES:(qb + 1) * LANES] = _pair_attention(
            q_ref[:, qb * LANES:(qb + 1) * LANES], masks,
            [k_ref[:, cols].astype(BF16)], [v_ref[:, cols].astype(BF16)], [None],
            HEAD_SCALE, 1.0, sinks)


def _ctx_attn(qkv, sink, kv_width, use_sink):
    qw = NA_HEADS * NA_HEAD_DIM
    k_blk = qw // kv_width
    return pl.pallas_call(
        functools.partial(_ctx_attn_kernel, kv_blocks=kv_width // LANES, use_sink=use_sink),
        grid_spec=pltpu.PrefetchScalarGridSpec(
            num_scalar_prefetch=1,
            grid=(BATCH,),
            in_specs=[
                pl.BlockSpec((SEQ, qw), lambda b, s: (b, 0)),
                pl.BlockSpec((SEQ, kv_width), lambda b, s: (b, k_blk)),
                pl.BlockSpec((SEQ, kv_width), lambda b, s: (b, k_blk + 1)),
            ],
            out_specs=pl.BlockSpec((SEQ, qw), lambda b, s: (b, 0)),
        ),
        out_shape=jax.ShapeDtypeStruct((N_PROMPT, qw), F32),
        compiler_params=_cparams(("arbitrary",)),
    )(sink, qkv, qkv, qkv)


def _na_union_start(rb):
    return jnp.clip(rb * NA_QROWS - NA_WIN_ROWS // 2, 0, GRID_ROWS - NA_UNION)


def _na_latent_kernel(q_ref, k_ref, v_ref, kc_ref, vc_ref, bias_ref, o_ref):
    rb = pl.program_id(2)
    start = pl.multiple_of(_na_union_start(rb) * GRID_W, GRID_W)
    n_loc = NA_UNION * GRID_W
    k_loc = k_ref[pl.ds(start, n_loc), :].astype(BF16)
    v_loc = v_ref[pl.ds(start, n_loc), :].astype(BF16)
    o_ref[...] = _pair_attention(
        q_ref[...], _half_masks(),
        [k_loc, kc_ref[0].astype(BF16)], [v_loc, vc_ref[0].astype(BF16)],
        [lambda h: bias_ref[0, h], None], HEAD_SCALE, 1.0, None)


def _na_bias_variant(rb):
    n = GRID_ROWS // NA_QROWS
    return jnp.where(rb == 0, 0, jnp.where(rb == n - 1, 2, 1))


def _na_latent_attn(qkv, k_ctx, v_ctx, bias):
    qw = NA_HEADS * NA_HEAD_DIM
    n_pairs = qw // LANES
    q_rows = NA_QROWS * GRID_W
    n_rb = DEC_SEQ // q_rows
    q_off = N_PROMPT // q_rows
    kv_off = N_PROMPT // DEC_SEQ
    return pl.pallas_call(
        _na_latent_kernel,
        grid=(DEC_BATCH, n_pairs, n_rb),
        in_specs=[
            pl.BlockSpec((q_rows, LANES), lambda b, j, r: (q_off + b * n_rb + r, j)),
            pl.BlockSpec((DEC_SEQ, LANES), lambda b, j, r: (kv_off + b, n_pairs + j)),
            pl.BlockSpec((DEC_SEQ, LANES), lambda b, j, r: (kv_off + b, 2 * n_pairs + j)),
            pl.BlockSpec((1, PAST_LEN, LANES), lambda b, j, r: (b, 0, j)),
            pl.BlockSpec((1, PAST_LEN, LANES), lambda b, j, r: (b, 0, j)),
            pl.BlockSpec((1, 2, q_rows, NA_UNION * GRID_W), lambda b, j, r: (_na_bias_variant(r), j, 0, 0)),
        ],
        out_specs=pl.BlockSpec((q_rows, LANES), lambda b, j, r: (b * n_rb + r, j)),
        out_shape=jax.ShapeDtypeStruct((N_SAMPLE, qw), F32),
        compiler_params=_cparams(("arbitrary", "arbitrary", "arbitrary")),
    )(qkv, qkv, qkv, k_ctx, v_ctx, bias)


def _na_bias_table(rel_bias):
    cols = np.arange(GRID_W)
    col_start = np.clip(cols - NA_WIN_COLS // 2, 0, GRID_W - NA_WIN_COLS)
    col_ok = (cols[None, :] >= col_start[:, None]) & (cols[None, :] < col_start[:, None] + NA_WIN_COLS)
    dc = np.clip(cols[None, :] - cols[:, None] + NA_WIN_COLS - 1, 0, 2 * NA_WIN_COLS - 2)
    bias_cols = jnp.where(col_ok[None, None], rel_bias.astype(F32)[:, :, dc], NEG_INF)
    n_rb = GRID_ROWS // NA_QROWS
    dr = np.zeros((3, NA_QROWS, NA_UNION), np.int32)
    ok = np.zeros((3, NA_QROWS, NA_UNION), bool)
    for variant, rb in enumerate((0, 1, n_rb - 1)):
        u0 = int(np.clip(rb * NA_QROWS - NA_WIN_ROWS // 2, 0, GRID_ROWS - NA_UNION))
        for j in range(NA_QROWS):
            r = rb * NA_QROWS + j
            start = int(np.clip(r - NA_WIN_ROWS // 2, 0, GRID_ROWS - NA_WIN_ROWS))
            for i in range(NA_UNION):
                key_row = u0 + i
                ok[variant, j, i] = start <= key_row < start + NA_WIN_ROWS
                dr[variant, j, i] = np.clip(key_row - r + NA_WIN_ROWS - 1, 0, 2 * NA_WIN_ROWS - 2)
    t = bias_cols[:, dr]
    t = jnp.where(ok[None, :, :, :, None, None], t, NEG_INF)
    t = t.transpose(1, 0, 2, 4, 3, 5)
    return t.reshape(3, NA_HEADS, NA_QROWS * GRID_W, NA_UNION * GRID_W)


GQA_Q_BLOCK = GQA_WINDOW
GQA_GROUP = GQA_HEADS // GQA_KV_HEADS


def _gqa_latent_kernel(sink_ref, q_ref, kp_ref, kc_ref, kn_ref, vp_ref, vc_ref, vn_ref,
                       kx_ref, vx_ref, o_ref):
    pair = pl.program_id(1)
    qb = pl.program_id(2)
    n_qb = pl.num_programs(2)
    blk = GQA_Q_BLOCK
    rows = GQA_GROUP * blk
    qq = lax.broadcasted_iota(jnp.int32, (rows, blk), 0) % blk
    kk = lax.broadcasted_iota(jnp.int32, (rows, blk), 1)
    neg = jnp.float32(NEG_INF)
    bias_prev = jnp.where((kk >= qq) & (qb > 0), 0.0, neg)
    bias_next = jnp.where((kk <= qq) & (qb < n_qb - 1), 0.0, neg)
    row_grp = lax.broadcasted_iota(jnp.int32, (rows, 1), 0) // blk
    q = jnp.concatenate([q_ref[:, m * LANES:(m + 1) * LANES] for m in range(GQA_GROUP)], axis=0)
    sinks = []
    for h in range(2):
        base = pair * 2 * GQA_GROUP + h * GQA_GROUP
        col = jnp.zeros((rows, 1), F32)
        for m in range(GQA_GROUP):
            col = jnp.where(row_grp == m, sink_ref[base + m], col)
        sinks.append(col)
    o = _pair_attention(
        q, _half_masks(),
        [kp_ref[...].astype(BF16), kc_ref[...].astype(BF16), kn_ref[...].astype(BF16), kx_ref[0].astype(BF16)],
        [vp_ref[...].astype(BF16), vc_ref[...].astype(BF16), vn_ref[...].astype(BF16), vx_ref[0].astype(BF16)],
        [lambda h: bias_prev, None, lambda h: bias_next, None], HEAD_SCALE, 1.0, sinks)
    for m in range(GQA_GROUP):
        o_ref[:, m * LANES:(m + 1) * LANES] = o[m * blk:(m + 1) * blk]


def _gqa_latent_attn(qkv, sink, k_ctx, v_ctx):
    qw = GQA_HEADS * GQA_HEAD_DIM
    blk = GQA_Q_BLOCK
    n_qb = DEC_SEQ // blk
    q_off = N_PROMPT // blk
    q_cols = GQA_GROUP * LANES
    n_pairs = GQA_KV_HEADS // 2
    k_col = qw // LANES
    v_col = k_col + n_pairs

    def kv_spec(col0, shift):
        def imap(b, p, i, s):
            return (q_off + b * n_qb + jnp.clip(i + shift, 0, n_qb - 1), col0 + p)
        return pl.BlockSpec((blk, LANES), imap)

    ctx_spec = pl.BlockSpec((1, PAST_LEN, LANES), lambda b, p, i, s: (b, 0, p))
    return pl.pallas_call(
        _gqa_latent_kernel,
        grid_spec=pltpu.PrefetchScalarGridSpec(
            num_scalar_prefetch=1,
            grid=(DEC_BATCH, n_pairs, n_qb),
            in_specs=[
                pl.BlockSpec((blk, q_cols), lambda b, p, i, s: (q_off + b * n_qb + i, p)),
                kv_spec(k_col, -1), kv_spec(k_col, 0), kv_spec(k_col, 1),
                kv_spec(v_col, -1), kv_spec(v_col, 0), kv_spec(v_col, 1),
                ctx_spec, ctx_spec,
            ],
            out_specs=pl.BlockSpec((blk, q_cols), lambda b, p, i, s: (b * n_qb + i, p)),
        ),
        out_shape=jax.ShapeDtypeStruct((N_SAMPLE, qw), F32),
        compiler_params=_cparams(("arbitrary", "arbitrary", "arbitrary")),
    )(sink, qkv, qkv, qkv, qkv, qkv, qkv, qkv, k_ctx, v_ctx)


MLA_SCALE = (MLA_NOPE + MLA_ROPE) ** -0.5
MLA_PAIRS = MLA_HEADS // 2


def _mla_masks():
    lane = lax.broadcasted_iota(jnp.int32, (1, 2 * LANES), 1)
    half = LANES // 2
    even = (lane < half) | ((lane >= LANES) & (lane < LANES + MLA_ROPE))
    odd = ((lane >= half) & (lane < LANES)) | ((lane >= LANES + MLA_ROPE) & (lane < LANES + 2 * MLA_ROPE))
    return [even, odd]


def _mla_ctx_kernel(q_ref, kv_ref, kpe_ref, o_ref):
    masks = _mla_masks()
    kpe = kpe_ref[...].astype(BF16)
    half = MLA_Q_COLS // 2
    for j in range(MLA_PAIRS):
        cols = slice(j * LANES, (j + 1) * LANES)
        rcols = slice(half + j * LANES, half + (j + 1) * LANES)
        q = jnp.concatenate([q_ref[:, cols], q_ref[:, rcols]], axis=1)
        k = jnp.concatenate([kv_ref[:, cols], kpe], axis=1)
        o_ref[:, cols] = _pair_attention(q, masks, [k], [kv_ref[:, rcols]], [None], 1.0, MLA_SCALE, None)


def _mla_ctx_attn(q, kv, kpe):
    ow = MLA_HEADS * MLA_V
    return pl.pallas_call(
        _mla_ctx_kernel,
        grid=(BATCH,),
        in_specs=[
            pl.BlockSpec((SEQ, MLA_Q_COLS), lambda b: (b, 0)),
            pl.BlockSpec((SEQ, MLA_Q_COLS), lambda b: (b, 0)),
            pl.BlockSpec((SEQ, LANES), lambda b: (b, 0)),
        ],
        out_specs=pl.BlockSpec((SEQ, ow), lambda b: (b, 0)),
        out_shape=jax.ShapeDtypeStruct((N_PROMPT, ow), F32),
        compiler_params=_cparams(("arbitrary",)),
    )(q, kv, kpe)


MLA_Q_BLOCK = 256


def _mla_latent_kernel(qn_ref, qr_ref, ka_ref, kpe_ref, v_ref, o_ref, kcat_ref):
    @pl.when(pl.program_id(2) == 0)
    def _():
        kcat_ref[:, :LANES] = ka_ref[...]
        kcat_ref[:, LANES:] = kpe_ref[...].astype(BF16)

    q = jnp.concatenate([qn_ref[...], qr_ref[...]], axis=1)
    o_ref[...] = _pair_attention(q, _mla_masks(), [kcat_ref[...]], [v_ref[...]], [None], 1.0, MLA_SCALE, None)


def _mla_latent_attn(q, kv, kpe):
    ow = MLA_HEADS * MLA_V
    n_qb = DEC_SEQ // MLA_Q_BLOCK
    q_off = N_PROMPT // MLA_Q_BLOCK
    return pl.pallas_call(
        _mla_latent_kernel,
        grid=(DEC_BATCH, MLA_PAIRS, n_qb),
        in_specs=[
            pl.BlockSpec((MLA_Q_BLOCK, LANES), lambda b, j, i: (q_off + b * n_qb + i, j)),
            pl.BlockSpec((MLA_Q_BLOCK, LANES), lambda b, j, i: (q_off + b * n_qb + i, MLA_PAIRS + j)),
            pl.BlockSpec((MLA_KEYS, LANES), lambda b, j, i: (b, j)),
            pl.BlockSpec((MLA_KEYS, LANES), lambda b, j, i: (b, 0)),
            pl.BlockSpec((MLA_KEYS, LANES), lambda b, j, i: (b, MLA_PAIRS + j)),
        ],
        out_specs=pl.BlockSpec((MLA_Q_BLOCK, LANES), lambda b, j, i: (b * n_qb + i, j)),
        out_shape=jax.ShapeDtypeStruct((N_SAMPLE, ow), F32),
        scratch_shapes=[pltpu.VMEM((MLA_KEYS, 2 * LANES), BF16)],
        compiler_params=_cparams(("arbitrary", "arbitrary", "arbitrary")),
    )(q, q, kv, kpe, kv)


def _out_proj_kernel(x_ref, o_ref, gate_ref, w_ref, y_ref):
    y_ref[...] = x_ref[...] + gate_ref[0] * _dot(o_ref[...].astype(BF16), w_ref[...])


def _out_proj(x, o_prompt, o_sample, gate, w):
    n_p = N_PROMPT // ROW_TILE
    k = w.shape[0]
    last_p = n_p - 1
    return pl.pallas_call(
        _out_proj_kernel_two,
        grid=(N_TOK // ROW_TILE,),
        in_specs=[
            pl.BlockSpec((ROW_TILE, D_MODEL), lambda i: (i, 0)),
            pl.BlockSpec((ROW_TILE, k), lambda i: (jnp.minimum(i, last_p), 0)),
            pl.BlockSpec((ROW_TILE, k), lambda i: (jnp.maximum(i - n_p, 0), 0)),
            _mod_spec(ROW_TILE),
            _resident(w.shape),
        ],
        out_specs=pl.BlockSpec((ROW_TILE, D_MODEL), lambda i: (i, 0)),
        out_shape=jax.ShapeDtypeStruct((N_TOK, D_MODEL), F32),
        compiler_params=_cparams(("arbitrary",)),
    )(x, o_prompt, o_sample, gate, w)


def _out_proj_kernel_two(x_ref, op_ref, os_ref, gate_ref, w_ref, y_ref):
    is_prompt = pl.program_id(0) < N_PROMPT // ROW_TILE

    @pl.when(is_prompt)
    def _():
        _out_proj_kernel(x_ref, op_ref, gate_ref, w_ref, y_ref)

    @pl.when(jnp.logical_not(is_prompt))
    def _():
        _out_proj_kernel(x_ref, os_ref, gate_ref, w_ref, y_ref)


def _ffn_kernel(x_ref, g_ref, sh_ref, sc_ref, gate_ref, wg_ref, wu_ref, wd_ref, y_ref, acc_ref):
    x = x_ref[...]
    hb = (_rms(x, g_ref[...]) * (1.0 + sc_ref[0]) + sh_ref[0]).astype(BF16)
    acc_ref[...] = jnp.zeros_like(acc_ref)

    def chunk(c, carry):
        a = _silu(_dot(hb, wg_ref[c])) * _dot(hb, wu_ref[c])
        acc_ref[...] += _dot(a.astype(BF16), wd_ref[c])
        return carry

    lax.fori_loop(0, N_FFN_CHUNKS, chunk, 0)
    y_ref[...] = x + gate_ref[0] * acc_ref[...]


def _ffn(x, gain, shift, scale, gate, wg, wu, wd):
    return pl.pallas_call(
        _ffn_kernel,
        grid=(N_TOK // ROW_TILE,),
        in_specs=[
            pl.BlockSpec((ROW_TILE, D_MODEL), lambda i: (i, 0)),
            pl.BlockSpec((1, D_MODEL), lambda i: (0, 0)),
            _mod_spec(ROW_TILE), _mod_spec(ROW_TILE), _mod_spec(ROW_TILE),
            _resident(wg.shape), _resident(wu.shape), _resident(wd.shape),
        ],
        out_specs=pl.BlockSpec((ROW_TILE, D_MODEL), lambda i: (i, 0)),
        out_shape=jax.ShapeDtypeStruct((N_TOK, D_MODEL), F32),
        scratch_shapes=[pltpu.VMEM((ROW_TILE, D_MODEL), F32)],
        compiler_params=_cparams(("arbitrary",)),
    )(x, gain, shift, scale, gate, wg, wu, wd)


SLAB = D_MODEL // LANES


def _store_slabs(ref, x):
    n = x.shape[0]
    for s in range(SLAB):
        ref[pl.ds(s, n, stride=SLAB), :] = x[:, s * LANES:(s + 1) * LANES]


def _load_slabs(ref, n, dtype, row0=0):
    return jnp.concatenate([ref[pl.ds(row0 + s, n, stride=SLAB), :].astype(dtype) for s in range(SLAB)], axis=1)


META_E1, META_E2, META_W1, META_W2, META_R1, META_R2 = range(6)


def _router_kernel(x_ref, g_ref, sh_ref, sc_ref, whi_ref, wlo_ref, tri_ref, h_ref, meta_ref, cnt_ref, carry_ref):
    @pl.when(pl.program_id(0) == 0)
    def _():
        carry_ref[...] = jnp.zeros_like(carry_ref)

    h = _rms(x_ref[...], g_ref[...]) * (1.0 + sc_ref[0]) + sh_ref[0]
    _store_slabs(h_ref, h)
    h_hi, h_lo = _split_bf16(h)
    logits = _dot(h_hi, whi_ref[...]) + _dot(h_lo, whi_ref[...]) + _dot(h_hi, wlo_ref[...])
    lane = lax.broadcasted_iota(jnp.int32, logits.shape, 1)
    lane_f = lane.astype(F32)
    ninf = jnp.float32(-jnp.inf)
    lg = jnp.where(lane < N_EXPERTS, logits, ninf)
    m1 = jnp.max(lg, axis=-1, keepdims=True)
    i1 = jnp.min(jnp.where(lg == m1, lane_f, float(LANES)), axis=-1, keepdims=True)
    lg2 = jnp.where(lane_f == i1, ninf, lg)
    m2 = jnp.max(lg2, axis=-1, keepdims=True)
    i2 = jnp.min(jnp.where(lg2 == m2, lane_f, float(LANES)), axis=-1, keepdims=True)
    e = jnp.exp(m2 - m1)
    w1 = 1.0 / (1.0 + e)
    w2 = e / (1.0 + e)
    sel1 = lane_f == i1
    sel2 = lane_f == i2
    onehot = jnp.where(sel1 | sel2, 1.0, 0.0)
    ranks = _dot(tri_ref[...], onehot.astype(BF16)) + carry_ref[...]
    r1 = jnp.sum(jnp.where(sel1, ranks, 0.0), axis=-1, keepdims=True)
    r2 = jnp.sum(jnp.where(sel2, ranks, 0.0), axis=-1, keepdims=True)
    carry_ref[...] += jnp.sum(onehot, axis=0, keepdims=True)
    cnt_ref[...] = carry_ref[...]
    meta = jnp.zeros(logits.shape, F32)
    for idx, val in ((META_E1, i1), (META_E2, i2), (META_W1, w1), (META_W2, w2), (META_R1, r1), (META_R2, r2)):
        meta = jnp.where(lane == idx, val, meta)
    meta_ref[...] = meta


def _router(x, gain, shift, scale, w_hi, w_lo, tri):
    return pl.pallas_call(
        _router_kernel,
        grid=(N_TOK // ROW_TILE,),
        in_specs=[
            pl.BlockSpec((ROW_TILE, D_MODEL), lambda i: (i, 0)),
            pl.BlockSpec((1, D_MODEL), lambda i: (0, 0)),
            _mod_spec(ROW_TILE), _mod_spec(ROW_TILE),
            _resident(w_hi.shape), _resident(w_lo.shape), _resident(tri.shape),
        ],
        out_specs=[
            pl.BlockSpec((ROW_TILE * SLAB, LANES), lambda i: (i, 0)),
            pl.BlockSpec((ROW_TILE, LANES), lambda i: (i, 0)),
            pl.BlockSpec((1, LANES), lambda i: (0, 0)),
        ],
        out_shape=[
            jax.ShapeDtypeStruct((N_TOK * SLAB, LANES), F32),
            jax.ShapeDtypeStruct((N_TOK, LANES), F32),
            jax.ShapeDtypeStruct((1, LANES), F32),
        ],
        scratch_shapes=[pltpu.VMEM((1, LANES), F32)],
        compiler_params=_cparams(("arbitrary",)),
    )(x, gain, shift, scale, w_hi, w_lo, tri)


def _route_plan(meta, counts):
    cnt = counts[0, :N_EXPERTS].astype(jnp.int32)
    tiles = (cnt + EXP_TILE - 1) // EXP_TILE
    tile_end = jnp.cumsum(tiles)
    tile_start = tile_end - tiles
    offs = tile_start * EXP_TILE
    e1 = meta[:, META_E1].astype(jnp.int32)
    e2 = meta[:, META_E2].astype(jnp.int32)
    pos1 = offs[e1] + meta[:, META_R1].astype(jnp.int32)
    pos2 = offs[e2] + meta[:, META_R2].astype(jnp.int32)
    t = jnp.arange(N_EXP_TILES, dtype=jnp.int32)
    tile_expert = jnp.minimum(jnp.sum(t[:, None] >= tile_end[None, :], axis=1), N_EXPERTS - 1).astype(jnp.int32)
    in_group = (t - tile_start[tile_expert]) * EXP_TILE
    tile_rows = jnp.where(t < tile_end[-1], jnp.clip(cnt[tile_expert] - in_group, 0, EXP_TILE), 0).astype(jnp.int32)
    return pos1, pos2, tile_expert, tile_rows


def _dispatch_kernel(pos1_ref, pos2_ref, h_ref, init_ref, xs_ref, sem):
    del init_ref
    base = pl.program_id(0) * DISPATCH_TILE

    def row_copy(tok, dst):
        return pltpu.make_async_copy(
            h_ref.at[pl.ds(pl.multiple_of(tok * SLAB, SLAB), SLAB)],
            xs_ref.at[pl.ds(pl.multiple_of(dst * SLAB, SLAB), SLAB)], sem)

    def issue(t, carry):
        tok = base + t
        row_copy(tok, pos1_ref[tok]).start()
        row_copy(tok, pos2_ref[tok]).start()
        return carry

    lax.fori_loop(0, DISPATCH_TILE, issue, 0)

    def drain(t, carry):
        row_copy(0, 0).wait()
        row_copy(0, 0).wait()
        return carry

    lax.fori_loop(0, DISPATCH_TILE, drain, 0)


def _dispatch(pos1, pos2, h_slabs, xs_init):
    return pl.pallas_call(
        _dispatch_kernel,
        grid_spec=pltpu.PrefetchScalarGridSpec(
            num_scalar_prefetch=2,
            grid=(N_TOK // DISPATCH_TILE,),
            in_specs=[pl.BlockSpec(memory_space=pl.ANY), pl.BlockSpec(memory_space=pl.ANY)],
            out_specs=pl.BlockSpec(memory_space=pl.ANY),
            scratch_shapes=[pltpu.SemaphoreType.DMA(())],
        ),
        out_shape=jax.ShapeDtypeStruct(xs_init.shape, F32),
        input_output_aliases={3: 0},
        compiler_params=pltpu.CompilerParams(dimension_semantics=("arbitrary",), has_side_effects=True),
    )(pos1, pos2, h_slabs, xs_init)


def _expert_kernel(te_ref, tr_ref, xs_ref, wg_ref, wu_ref, wd_ref, ys_ref, xb_ref, acc_ref):
    t = pl.program_id(0)
    c = pl.program_id(1)

    @pl.when(c == 0)
    def _():
        xb_ref[...] = _load_slabs(xs_ref, EXP_TILE, BF16)
        acc_ref[...] = jnp.zeros_like(acc_ref)

    @pl.when(tr_ref[t] > 0)
    def _():
        xb = xb_ref[...]
        a = _silu(_dot(xb, wg_ref[0].astype(BF16))) * _dot(xb, wu_ref[0].astype(BF16))
        acc_ref[...] += _dot(a.astype(BF16), wd_ref[0].astype(BF16))

    @pl.when(c == N_EXP_CHUNKS - 1)
    def _():
        _store_slabs(ys_ref, acc_ref[...])


def _experts(tile_expert, tile_rows, xs, wg, wu, wd):
    def chunk_of(t, c, tr):
        return jnp.where(tr[t] > 0, c, N_EXP_CHUNKS - 1)

    return pl.pallas_call(
        _expert_kernel,
        grid_spec=pltpu.PrefetchScalarGridSpec(
            num_scalar_prefetch=2,
            grid=(N_EXP_TILES, N_EXP_CHUNKS),
            in_specs=[
                pl.BlockSpec((EXP_TILE * SLAB, LANES), lambda t, c, te, tr: (t, 0)),
                pl.BlockSpec((1, D_MODEL, EXP_CHUNK), lambda t, c, te, tr: (te[t], 0, chunk_of(t, c, tr))),
                pl.BlockSpec((1, D_MODEL, EXP_CHUNK), lambda t, c, te, tr: (te[t], 0, chunk_of(t, c, tr))),
                pl.BlockSpec((1, EXP_CHUNK, D_MODEL), lambda t, c, te, tr: (te[t], chunk_of(t, c, tr), 0)),
            ],
            out_specs=pl.BlockSpec((EXP_TILE * SLAB, LANES), lambda t, c, te, tr: (t, 0)),
            scratch_shapes=[pltpu.VMEM((EXP_TILE, D_MODEL), BF16), pltpu.VMEM((EXP_TILE, D_MODEL), F32)],
        ),
        out_shape=jax.ShapeDtypeStruct(xs.shape, F32),
        compiler_params=_cparams(("arbitrary", "arbitrary")),
    )(tile_expert, tile_rows, xs, wg, wu, wd)


def _combine_kernel(pos1_ref, pos2_ref, x_ref, gate_ref, meta_ref, ys_ref, y_ref, buf_ref, sem):
    base = pl.program_id(0) * COMBINE_TILE
    n = COMBINE_TILE

    def row_copy(src, slot):
        return pltpu.make_async_copy(
            ys_ref.at[pl.ds(pl.multiple_of(src * SLAB, SLAB), SLAB)],
            buf_ref.at[pl.ds(pl.multiple_of(slot * SLAB, SLAB), SLAB)], sem)

    def issue(t, carry):
        row_copy(pos1_ref[base + t], t).start()
        row_copy(pos2_ref[base + t], n + t).start()
        return carry

    lax.fori_loop(0, n, issue, 0)

    def drain(t, carry):
        row_copy(0, 0).wait()
        row_copy(0, 0).wait()
        return carry

    lax.fori_loop(0, n, drain, 0)

    meta = meta_ref[...]
    w1 = meta[:, META_W1:META_W1 + 1]
    w2 = meta[:, META_W2:META_W2 + 1]
    y1 = _load_slabs(buf_ref, n, F32)
    y2 = _load_slabs(buf_ref, n, F32, row0=n * SLAB)
    y_ref[...] = x_ref[...] + gate_ref[0] * (w1 * y1 + w2 * y2)


def _combine(pos1, pos2, x, gate, meta, ys):
    return pl.pallas_call(
        _combine_kernel,
        grid_spec=pltpu.PrefetchScalarGridSpec(
            num_scalar_prefetch=2,
            grid=(N_TOK // COMBINE_TILE,),
            in_specs=[
                pl.BlockSpec((COMBINE_TILE, D_MODEL), lambda i, p1, p2: (i, 0)),
                pl.BlockSpec((1, 1, D_MODEL), lambda i, p1, p2: (_cond_of_row(i * COMBINE_TILE), 0, 0)),
                pl.BlockSpec((COMBINE_TILE, LANES), lambda i, p1, p2: (i, 0)),
                pl.BlockSpec(memory_space=pl.ANY),
            ],
            out_specs=pl.BlockSpec((COMBINE_TILE, D_MODEL), lambda i, p1, p2: (i, 0)),
            scratch_shapes=[pltpu.VMEM((2 * COMBINE_TILE * SLAB, LANES), F32), pltpu.SemaphoreType.DMA(())],
        ),
        out_shape=jax.ShapeDtypeStruct((N_TOK, D_MODEL), F32),
        compiler_params=_cparams(("arbitrary",)),
    )(pos1, pos2, x, gate, meta, ys)


def _moe(x, gain, shift, scale, gate, router_w, wg, wu, wd, tri):
    w = jnp.pad(router_w, ((0, 0), (0, LANES - N_EXPERTS)))
    w_hi = w.astype(BF16)
    w_lo = (w - w_hi.astype(F32)).astype(BF16)
    h_slabs, meta, counts = _router(x, gain, shift, scale, w_hi, w_lo, tri)
    pos1, pos2, tile_expert, tile_rows = _route_plan(meta, counts)
    xs = _dispatch(pos1, pos2, h_slabs, jnp.zeros((N_EXP_TILES * EXP_TILE * SLAB, LANES), F32))
    ys = _experts(tile_expert, tile_rows, xs, wg, wu, wd)
    return _combine(pos1, pos2, x, gate, meta, ys)


def _final_norm_kernel(x_ref, g_ref, y_ref):
    y_ref[...] = _rms(x_ref[...], g_ref[...])


def _final_norm(x, gain, row0, rows):
    off = row0 // ROW_TILE
    return pl.pallas_call(
        _final_norm_kernel,
        grid=(rows // ROW_TILE,),
        in_specs=[
            pl.BlockSpec((ROW_TILE, D_MODEL), lambda i: (off + i, 0)),
            pl.BlockSpec((1, D_MODEL), lambda i: (0, 0)),
        ],
        out_specs=pl.BlockSpec((ROW_TILE, D_MODEL), lambda i: (i, 0)),
        out_shape=jax.ShapeDtypeStruct((rows, D_MODEL), F32),
        compiler_params=_cparams(("arbitrary",)),
    )(x, gain)


def _rope_tables(rot_dim, reps):
    n_freq = rot_dim // 4
    inv_freq = ROPE_THETA ** (-np.arange(n_freq, dtype=np.float64) / n_freq)
    t = np.arange(DEC_SEQ)
    ang_r = (t // GRID_W)[:, None] * inv_freq
    ang_c = (t % GRID_W)[:, None] * inv_freq
    cos = np.concatenate([np.cos(ang_r), np.cos(ang_r), np.cos(ang_c), np.cos(ang_c)], axis=1)
    sin = np.concatenate([-np.sin(ang_r), np.sin(ang_r), -np.sin(ang_c), np.sin(ang_c)], axis=1)
    pad = LANES - reps * rot_dim
    cos = np.concatenate([cos] * reps + [np.ones((DEC_SEQ, pad))], axis=1)
    sin = np.concatenate([sin] * reps + [np.zeros((DEC_SEQ, pad))], axis=1)
    cos = np.concatenate([cos, np.ones((ROW_TILE, LANES))], axis=0)
    sin = np.concatenate([sin, np.zeros((ROW_TILE, LANES))], axis=0)
    return jnp.asarray(cos, F32), jnp.asarray(sin, F32)


def _swap_index(rot_dim):
    q = rot_dim // 4
    idx = np.arange(rot_dim)
    return np.where((idx // q) % 2 == 0, idx + q, idx - q)


def _gqa_q_perm():
    heads = []
    for p in range(GQA_KV_HEADS // 2):
        for m in range(GQA_GROUP):
            heads += [2 * p * GQA_GROUP + m, (2 * p + 1) * GQA_GROUP + m]
    return np.concatenate([np.arange(h * GQA_HEAD_DIM, (h + 1) * GQA_HEAD_DIM) for h in heads])


def _mla_layouts(wq_b, wkv_a, wkv_b):
    hd = MLA_NOPE + MLA_ROPE
    nope_cols = np.concatenate([np.arange(h * hd, h * hd + MLA_NOPE) for h in range(MLA_HEADS)])
    swap = _swap_index(MLA_ROPE)
    zeros_q = jnp.zeros((MLA_Q_LORA, LANES - 2 * MLA_ROPE), F32)
    rope_blocks, rope_swapped = [], []
    for j in range(MLA_PAIRS):
        for blocks, idx in ((rope_blocks, np.arange(MLA_ROPE)), (rope_swapped, swap)):
            blocks += [wq_b[:, (2 * j) * hd + MLA_NOPE + idx], wq_b[:, (2 * j + 1) * hd + MLA_NOPE + idx], zeros_q]
    wb = jnp.concatenate([wq_b[:, nope_cols]] + rope_blocks, axis=1)
    wbs = jnp.concatenate(rope_swapped, axis=1)
    zeros_a = jnp.zeros((D_MODEL, LANES - 2 * MLA_ROPE), F32)
    kpe_w = wkv_a[:, MLA_KV_LORA:]
    kpe_ws = kpe_w[:, swap]
    wa_tail = jnp.concatenate([wkv_a[:, :MLA_KV_LORA], kpe_w, kpe_w, zeros_a, kpe_ws, kpe_ws, zeros_a], axis=1)
    kvd = MLA_NOPE + MLA_V
    k_cols = np.concatenate([np.arange(h * kvd, h * kvd + MLA_NOPE) for h in range(MLA_HEADS)])
    v_cols = k_cols + MLA_NOPE
    wkvb = jnp.concatenate([wkv_b[:, k_cols], wkv_b[:, v_cols]], axis=1)
    return wb, wbs, wa_tail, wkvb


def _chunk_cols(w, chunk):
    k, n = w.shape
    return w.reshape(k, n // chunk, chunk).transpose(1, 0, 2).astype(BF16)


def kernel(x_prompt, x_sample, cache_l0_k, cache_l0_v, cache_l1_ckv, cache_l1_kpe, cache_l2_k, cache_l2_v, cache_l3_k, cache_l3_v, c, c_ctx, ada_w, ada_b, norm_mix, norm_ffn, norm_final, na_w_qkv, na_w_o, na_rel_bias, mla_wq_a, mla_q_norm, mla_wq_b, mla_wkv_a, mla_kv_norm, mla_wkv_b, mla_w_o, gqa_w_qkv, gqa_w_o, gqa_sink, ffn_w_gate, ffn_w_up, ffn_w_down, moe_router, moe_w_gate, moe_w_up, moe_w_down):
    x = jnp.concatenate([x_prompt.reshape(N_PROMPT, D_MODEL), x_sample.reshape(N_SAMPLE, D_MODEL)], axis=0)
    cond = jnp.concatenate([c_ctx[None, :], c, jnp.zeros((N_COND - 1 - DEC_BATCH, D_MODEL), F32)], axis=0)
    mods = _adaln(cond, ada_w, ada_b).reshape(DEPTH, N_COND, 6, 1, D_MODEL)
    tri = jnp.asarray(np.tril(np.ones((ROW_TILE, ROW_TILE), np.float32), -1), BF16)
    no_sink = jnp.zeros((GQA_HEADS,), F32)
    na_caches = {0: (cache_l0_k, cache_l0_v), 3: (cache_l3_k, cache_l3_v)}
    state = []

    for i in range(DEPTH):
        sh_m, sc_m, g_m, sh_f, sc_f, g_f = [mods[i, :, j] for j in range(6)]
        gain_m = norm_mix[i][None, :]
        gain_f = norm_ffn[i][None, :]
        kind, s = i % 3, i // 3
        if kind == 0:
            qkv = _norm_proj(x, gain_m, sh_m, sc_m, na_w_qkv[s].astype(BF16))
            qw = NA_HEADS * NA_HEAD_DIM
            o_p = _ctx_attn(qkv, no_sink, qw, False)
            ck, cv = na_caches[i]
            o_s = _na_latent_attn(qkv, ck.reshape(DEC_BATCH, PAST_LEN, qw), cv.reshape(DEC_BATCH, PAST_LEN, qw),
                                  _na_bias_table(na_rel_bias[s]))
            w_o = na_w_o[s].astype(BF16)
            kv_shape = (BATCH, SEQ, NA_HEADS, NA_HEAD_DIM)
            state += [qkv[:N_PROMPT, qw:2 * qw].reshape(kv_shape), qkv[:N_PROMPT, 2 * qw:].reshape(kv_shape)]
        elif kind == 1:
            wb, wbs, wa_tail, wkvb = _mla_layouts(mla_wq_b[s], mla_wkv_a[s], mla_wkv_b[s])
            wa = jnp.concatenate([mla_wq_a[s], wa_tail], axis=1).astype(BF16)
            cos, sin = _rope_tables(MLA_ROPE, 2)
            q, ckv, kpe = _mla_proj(x, gain_m, sh_m, sc_m, wa, mla_q_norm[s][None, :], mla_kv_norm[s][None, :],
                                    wb.astype(BF16), wbs.astype(BF16), cos, sin)
            wkvb = wkvb.astype(BF16)
            kv_p = _rows_matmul(ckv[:N_PROMPT], wkvb, BF16)
            o_p = _mla_ctx_attn(q, kv_p, kpe)
            ckv_s = ckv[N_PROMPT:].reshape(DEC_BATCH, DEC_SEQ, MLA_KV_LORA)
            ckv_all = jnp.concatenate([cache_l1_ckv, ckv_s], axis=1).reshape(DEC_BATCH * MLA_KEYS, MLA_KV_LORA)
            kv_s = _rows_matmul(ckv_all, wkvb, BF16)
            kpe_cache = jnp.concatenate(
                [cache_l1_kpe, cache_l1_kpe, jnp.zeros((DEC_BATCH, PAST_LEN, LANES - 2 * MLA_ROPE), F32)], axis=2)
            kpe_all = jnp.concatenate([kpe_cache, kpe[N_PROMPT:].reshape(DEC_BATCH, DEC_SEQ, LANES)], axis=1)
            o_s = _mla_latent_attn(q, kv_s, kpe_all.reshape(DEC_BATCH * MLA_KEYS, LANES))
            w_o = mla_w_o[s].astype(BF16)
            state += [ckv[:N_PROMPT].reshape(BATCH, SEQ, MLA_KV_LORA),
                      kpe[:N_PROMPT, :MLA_ROPE].reshape(BATCH, SEQ, MLA_ROPE)]
        else:
            qw = GQA_HEADS * GQA_HEAD_DIM
            kw = GQA_KV_HEADS * GQA_HEAD_DIM
            perm = _gqa_q_perm()
            w_qkv = jnp.concatenate([gqa_w_qkv[s][:, perm], gqa_w_qkv[s][:, qw:]], axis=1)
            swap = np.concatenate([h * GQA_HEAD_DIM + _swap_index(GQA_HEAD_DIM) for h in range((qw + kw) // GQA_HEAD_DIM)])
            cos, sin = _rope_tables(GQA_HEAD_DIM, 2)
            qkv = _norm_proj(x, gain_m, sh_m, sc_m, w_qkv.astype(BF16),
                             rope=(w_qkv[:, swap].astype(BF16), cos, sin))
            sink = gqa_sink[s]
            o_p = _ctx_attn(qkv, sink, kw, True)
            o_s = _gqa_latent_attn(qkv, sink, cache_l2_k.reshape(DEC_BATCH, PAST_LEN, kw),
                                   cache_l2_v.reshape(DEC_BATCH, PAST_LEN, kw))
            w_o = gqa_w_o[s][perm].astype(BF16)
            kv_shape = (BATCH, SEQ, GQA_KV_HEADS, GQA_HEAD_DIM)
            state += [qkv[:N_PROMPT, qw:qw + kw].reshape(kv_shape), qkv[:N_PROMPT, qw + kw:].reshape(kv_shape)]
        x = _out_proj(x, o_p, o_s, g_m, w_o)

        if i % 2 == 0:
            s = i // 2
            x = _ffn(x, gain_f, sh_f, sc_f, g_f, _chunk_cols(ffn_w_gate[s], FFN_CHUNK),
                     _chunk_cols(ffn_w_up[s], FFN_CHUNK),
                     ffn_w_down[s].reshape(N_FFN_CHUNKS, FFN_CHUNK, D_MODEL).astype(BF16))
        else:
            s = i // 2
            x = _moe(x, gain_f, sh_f, sc_f, g_f, moe_router[s], moe_w_gate[s], moe_w_up[s], moe_w_down[s], tri)

    y_prompt = _final_norm(x, norm_final[None, :], 0, N_PROMPT).reshape(BATCH, SEQ, D_MODEL)
    y_sample = _final_norm(x, norm_final[None, :], N_PROMPT, N_SAMPLE).reshape(DEC_BATCH, DEC_SEQ, D_MODEL)
    return (y_prompt, y_sample, *state)
```

```python
import functools
import math

import numpy as np
import jax
import jax.numpy as jnp
from jax import lax
from jax.experimental import pallas as pl
from jax.experimental.pallas import tpu as pltpu

F32 = jnp.float32
BF16 = jnp.bfloat16

D_MODEL = 1024
BATCH = 16
SEQ = 256
DEPTH = 4
DEC_BATCH = 4
DEC_SEQ = 2048
PAST_LEN = 512
GRID_W = 64
GRID_ROWS = DEC_SEQ // GRID_W
NORM_EPS = 1e-6
NEG_INF = -1e30
ROPE_THETA = 10000.0
NA_HEADS = 16
NA_HEAD_DIM = 64
NA_WIN_ROWS = 8
NA_WIN_COLS = 16
MLA_HEADS = 16
MLA_Q_LORA = 512
MLA_KV_LORA = 256
MLA_NOPE = 64
MLA_ROPE = 32
MLA_V = 64
GQA_HEADS = 16
GQA_KV_HEADS = 4
GQA_HEAD_DIM = 64
GQA_WINDOW = 128
FFN_DIM = 2816
N_EXPERTS = 8
EXPERT_DIM = 3584

N_PROMPT = BATCH * SEQ
N_SAMPLE = DEC_BATCH * DEC_SEQ
N_TOK = N_PROMPT + N_SAMPLE
N_COND = 8
LOG2E = math.log2(math.e)
HEAD_SCALE = NA_HEAD_DIM ** -0.5 * LOG2E
MLA_SCALE = (MLA_NOPE + MLA_ROPE) ** -0.5 * LOG2E

LANES = 128
SUBLANES = 8
VMEM_LIMIT = 56 * 1024 * 1024

ROW_TILE = 512
FFN_CHUNK = 256
N_FFN_CHUNKS = FFN_DIM // FFN_CHUNK
EXP_TILE = 1024
EXP_PART = 512
EXP_CHUNK = 512
N_EXP_CHUNKS = EXPERT_DIM // EXP_CHUNK
N_EXP_TILES = (2 * N_TOK) // EXP_TILE + N_EXPERTS
DISPATCH_TILE = 512
COMBINE_TILE = 256
NA_QROWS = 4
NA_UNION = NA_QROWS + NA_WIN_ROWS
MLA_KEYS = PAST_LEN + DEC_SEQ


def _cparams(sem):
    return pltpu.CompilerParams(dimension_semantics=sem, vmem_limit_bytes=VMEM_LIMIT)


def _cond_of_row(row):
    return jnp.where(row < N_PROMPT, 0, 1 + (row - N_PROMPT) // DEC_SEQ)


def _nt_dot(a, b):
    return lax.dot_general(a, b, (((1,), (1,)), ((), ())), preferred_element_type=F32)


def _dot(a, b):
    return jnp.dot(a, b, preferred_element_type=F32)


def _rms(x, gain):
    return x * lax.rsqrt(jnp.mean(x * x, axis=-1, keepdims=True) + NORM_EPS) * gain


def _silu(x):
    return x / (1.0 + jnp.exp(-x))


def _split_bf16(x):
    hi = x.astype(BF16)
    lo = (x - hi.astype(F32)).astype(BF16)
    return hi, lo


def _resident(shape):
    nd = len(shape)
    return pl.BlockSpec(shape, lambda *_: (0,) * nd, pipeline_mode=pl.Buffered(1))


def _mod_spec(tile):
    return pl.BlockSpec((1, 1, D_MODEL), lambda i: (_cond_of_row(i * tile), 0, 0))


ADA_TILE = 1536


def _adaln_kernel(c_ref, w_ref, b_ref, o_ref):
    a_hi, a_lo = _split_bf16(_silu(c_ref[...]))
    w_hi, w_lo = _split_bf16(w_ref[0])
    o_ref[0] = _dot(a_hi, w_hi) + _dot(a_lo, w_hi) + _dot(a_hi, w_lo) + b_ref[0]


def _adaln(cond, ada_w, ada_b):
    n = 6 * D_MODEL
    return pl.pallas_call(
        _adaln_kernel,
        grid=(DEPTH, n // ADA_TILE),
        in_specs=[
            pl.BlockSpec((N_COND, D_MODEL), lambda l, j: (0, 0)),
            pl.BlockSpec((1, D_MODEL, ADA_TILE), lambda l, j: (l, 0, j)),
            pl.BlockSpec((1, 1, ADA_TILE), lambda l, j: (l, 0, j)),
        ],
        out_specs=pl.BlockSpec((1, N_COND, ADA_TILE), lambda l, j: (l, 0, j)),
        out_shape=jax.ShapeDtypeStruct((DEPTH, N_COND, n), F32),
        compiler_params=_cparams(("arbitrary", "arbitrary")),
    )(cond, ada_w, ada_b.reshape(DEPTH, 1, n))


PROJ_COLS = 256


def _norm_proj_kernel(x_ref, g_ref, sh_ref, sc_ref, w_ref, *rest, n_out, n_rope):
    if n_rope:
        w2_ref, cos_ref, sin_ref, o_ref = rest
    else:
        (o_ref,) = rest
    h = _rms(x_ref[...], g_ref[...]) * (1.0 + sc_ref[0]) + sh_ref[0]
    hb = h.astype(BF16)
    for j in range(n_out // PROJ_COLS):
        cols = slice(j * PROJ_COLS, (j + 1) * PROJ_COLS)
        y = _dot(hb, w_ref[:, cols])
        if j * PROJ_COLS < n_rope:
            ys = _dot(hb, w2_ref[:, cols])
            reps = PROJ_COLS // LANES
            cos = jnp.concatenate([cos_ref[...]] * reps, axis=1)
            sin = jnp.concatenate([sin_ref[...]] * reps, axis=1)
            y = y * cos + ys * sin
        o_ref[:, cols] = y


def _rope_block(i):
    row = i * ROW_TILE
    return jnp.where(row < N_PROMPT, DEC_SEQ // ROW_TILE, ((row - N_PROMPT) % DEC_SEQ) // ROW_TILE)


def _norm_proj(x, gain, shift, scale, w, rope=None):
    n_out = w.shape[1]
    in_specs = [
        pl.BlockSpec((ROW_TILE, D_MODEL), lambda i: (i, 0)),
        pl.BlockSpec((1, D_MODEL), lambda i: (0, 0)),
        _mod_spec(ROW_TILE),
        _mod_spec(ROW_TILE),
        _resident(w.shape),
    ]
    args = [x, gain, shift, scale, w]
    n_rope = 0
    if rope is not None:
        w2, cos, sin = rope
        n_rope = w2.shape[1]
        in_specs += [
            _resident(w2.shape),
            pl.BlockSpec((ROW_TILE, LANES), lambda i: (_rope_block(i), 0)),
            pl.BlockSpec((ROW_TILE, LANES), lambda i: (_rope_block(i), 0)),
        ]
        args += [w2, cos, sin]
    return pl.pallas_call(
        functools.partial(_norm_proj_kernel, n_out=n_out, n_rope=n_rope),
        grid=(N_TOK // ROW_TILE,),
        in_specs=in_specs,
        out_specs=pl.BlockSpec((ROW_TILE, n_out), lambda i: (i, 0)),
        out_shape=jax.ShapeDtypeStruct((N_TOK, n_out), F32),
        compiler_params=_cparams(("arbitrary",)),
    )(*args)


MLA_Q_COLS = 2 * MLA_HEADS * MLA_NOPE


def _mla_proj_kernel(x_ref, g_ref, sh_ref, sc_ref, wa_ref, qn_ref, kvn_ref, wb_ref, wbs_ref,
                     cos_ref, sin_ref, q_ref, ckv_ref, kpe_ref):
    h = _rms(x_ref[...], g_ref[...]) * (1.0 + sc_ref[0]) + sh_ref[0]
    a = _dot(h.astype(BF16), wa_ref[...])
    c0 = MLA_Q_LORA
    c1 = c0 + MLA_KV_LORA
    cos = cos_ref[...]
    sin = sin_ref[...]
    ckv_ref[...] = _rms(a[:, c0:c1], kvn_ref[...])
    kpe_ref[...] = a[:, c1:c1 + LANES] * cos + a[:, c1 + LANES:c1 + 2 * LANES] * sin
    qn = _rms(a[:, :c0], qn_ref[...]).astype(BF16)
    half = MLA_Q_COLS // 2
    q_scale = MLA_SCALE
    q_ref[:, :half] = (_dot(qn, wb_ref[:, :half]) * q_scale).astype(BF16)
    q_rope = _dot(qn, wb_ref[:, half:])
    q_swap = _dot(qn, wbs_ref[...])
    reps = half // LANES
    cos_r = jnp.concatenate([cos] * reps, axis=1) * q_scale
    sin_r = jnp.concatenate([sin] * reps, axis=1) * q_scale
    q_ref[:, half:] = (q_rope * cos_r + q_swap * sin_r).astype(BF16)


def _mla_proj(x, gain, shift, scale, wa, q_norm, kv_norm, wb, wbs, cos, sin):
    return pl.pallas_call(
        _mla_proj_kernel,
        grid=(N_TOK // ROW_TILE,),
        in_specs=[
            pl.BlockSpec((ROW_TILE, D_MODEL), lambda i: (i, 0)),
            pl.BlockSpec((1, D_MODEL), lambda i: (0, 0)),
            _mod_spec(ROW_TILE),
            _mod_spec(ROW_TILE),
            _resident(wa.shape),
            pl.BlockSpec((1, MLA_Q_LORA), lambda i: (0, 0)),
            pl.BlockSpec((1, MLA_KV_LORA), lambda i: (0, 0)),
            _resident(wb.shape),
            _resident(wbs.shape),
            pl.BlockSpec((ROW_TILE, LANES), lambda i: (_rope_block(i), 0)),
            pl.BlockSpec((ROW_TILE, LANES), lambda i: (_rope_block(i), 0)),
        ],
        out_specs=[
            pl.BlockSpec((ROW_TILE, MLA_Q_COLS), lambda i: (i, 0)),
            pl.BlockSpec((ROW_TILE, MLA_KV_LORA), lambda i: (i, 0)),
            pl.BlockSpec((ROW_TILE, LANES), lambda i: (i, 0)),
        ],
        out_shape=[
            jax.ShapeDtypeStruct((N_TOK, MLA_Q_COLS), BF16),
            jax.ShapeDtypeStruct((N_TOK, MLA_KV_LORA), F32),
            jax.ShapeDtypeStruct((N_TOK, LANES), F32),
        ],
        compiler_params=_cparams(("arbitrary",)),
    )(x, gain, shift, scale, wa, q_norm, kv_norm, wb, wbs, cos, sin)


def _rows_matmul_kernel(x_ref, w_ref, o_ref):
    o_ref[...] = _dot(x_ref[...].astype(BF16), w_ref[...]).astype(o_ref.dtype)


def _rows_matmul(x, w, out_dtype):
    rows, k = x.shape
    n = w.shape[1]
    return pl.pallas_call(
        _rows_matmul_kernel,
        grid=(rows // ROW_TILE,),
        in_specs=[pl.BlockSpec((ROW_TILE, k), lambda i: (i, 0)), _resident(w.shape)],
        out_specs=pl.BlockSpec((ROW_TILE, n), lambda i: (i, 0)),
        out_shape=jax.ShapeDtypeStruct((rows, n), out_dtype),
        compiler_params=_cparams(("arbitrary",)),
    )(x, w)


def _pair_attention(q, masks, keys, values, biases, q_scale, sinks):
    half = LANES // 2
    lane = lax.broadcasted_iota(jnp.int32, (1, LANES), 1)
    outs = []
    for h in range(2):
        qh = jnp.where(masks[h], q, jnp.zeros_like(q))
        if q_scale != 1.0:
            qh = qh * q_scale
        qh = qh.astype(BF16)
        scores = []
        for k, bias in zip(keys, biases):
            s = _nt_dot(qh, k)
            if bias is not None:
                s = s + bias(h)
            scores.append(s)
        m = functools.reduce(jnp.maximum, [jnp.max(s, axis=-1, keepdims=True) for s in scores])
        if sinks is not None:
            m = jnp.maximum(m, sinks[h])
        own = (lane < half) if h == 0 else (lane >= half)
        acc = None
        for s, v in zip(scores, values):
            part = _dot(jnp.exp2(s - m).astype(BF16), jnp.where(own, v, jnp.ones_like(v)))
            acc = part if acc is None else acc + part
        ones_lane = half if h == 0 else 0
        denom = acc[:, ones_lane:ones_lane + 1]
        if sinks is not None:
            denom = denom + jnp.exp2(sinks[h] - m)
        outs.append(acc / denom)
    return jnp.where(lane < half, outs[0], outs[1])


def _half_masks(width=LANES):
    lane = lax.broadcasted_iota(jnp.int32, (1, width), 1)
    return [lane < LANES // 2, lane >= LANES // 2]


def _ctx_attn_kernel(sink_ref, q_ref, k_ref, v_ref, o_ref, *, kv_blocks, use_sink):
    masks = _half_masks()
    n_q_blocks = o_ref.shape[1] // LANES
    for qb in range(n_q_blocks):
        kb = qb * kv_blocks // n_q_blocks
        cols = slice(kb * LANES, (kb + 1) * LANES)
        sinks = None
        if use_sink:
            base = (qb // 4) * 8 + qb % 4
            sinks = [sink_ref[base] * LOG2E, sink_ref[base + 4] * LOG2E]
        o_ref[:, qb * LANES:(qb + 1) * LANES] = _pair_attention(
            q_ref[:, qb * LANES:(qb + 1) * LANES], masks,
            [k_ref[:, cols].astype(BF16)], [v_ref[:, cols].astype(BF16)], [None],
            HEAD_SCALE, sinks)


def _ctx_attn(qkv, sink, kv_width, use_sink):
    qw = NA_HEADS * NA_HEAD_DIM
    k_blk = qw // kv_width
    return pl.pallas_call(
        functools.partial(_ctx_attn_kernel, kv_blocks=kv_width // LANES, use_sink=use_sink),
        grid_spec=pltpu.PrefetchScalarGridSpec(
            num_scalar_prefetch=1,
            grid=(BATCH,),
            in_specs=[
                pl.BlockSpec((SEQ, qw), lambda b, s: (b, 0)),
                pl.BlockSpec((SEQ, kv_width), lambda b, s: (b, k_blk)),
                pl.BlockSpec((SEQ, kv_width), lambda b, s: (b, k_blk + 1)),
            ],
            out_specs=pl.BlockSpec((SEQ, qw), lambda b, s: (b, 0)),
        ),
        out_shape=jax.ShapeDtypeStruct((N_PROMPT, qw), F32),
        compiler_params=_cparams(("arbitrary",)),
    )(sink, qkv, qkv, qkv)


def _na_union_start(rb):
    return jnp.clip(rb * NA_QROWS - NA_WIN_ROWS // 2, 0, GRID_ROWS - NA_UNION)


def _na_latent_kernel(q_ref, k_ref, v_ref, kc_ref, vc_ref, bias_ref, o_ref):
    rb = pl.program_id(2)
    start = pl.multiple_of(_na_union_start(rb) * GRID_W, GRID_W)
    n_loc = NA_UNION * GRID_W
    k_loc = k_ref[pl.ds(start, n_loc), :].astype(BF16)
    v_loc = v_ref[pl.ds(start, n_loc), :].astype(BF16)
    o_ref[...] = _pair_attention(
        q_ref[...], _half_masks(),
        [k_loc, kc_ref[0].astype(BF16)], [v_loc, vc_ref[0].astype(BF16)],
        [lambda h: bias_ref[0, h], None], HEAD_SCALE, None)


def _na_bias_variant(rb):
    n = GRID_ROWS // NA_QROWS
    return jnp.where(rb == 0, 0, jnp.where(rb == n - 1, 2, 1))


def _na_latent_attn(qkv, k_ctx, v_ctx, bias):
    qw = NA_HEADS * NA_HEAD_DIM
    n_pairs = qw // LANES
    q_rows = NA_QROWS * GRID_W
    n_rb = DEC_SEQ // q_rows
    q_off = N_PROMPT // q_rows
    kv_off = N_PROMPT // DEC_SEQ
    return pl.pallas_call(
        _na_latent_kernel,
        grid=(DEC_BATCH, n_pairs, n_rb),
        in_specs=[
            pl.BlockSpec((q_rows, LANES), lambda b, j, r: (q_off + b * n_rb + r, j)),
            pl.BlockSpec((DEC_SEQ, LANES), lambda b, j, r: (kv_off + b, n_pairs + j)),
            pl.BlockSpec((DEC_SEQ, LANES), lambda b, j, r: (kv_off + b, 2 * n_pairs + j)),
            pl.BlockSpec((1, PAST_LEN, LANES), lambda b, j, r: (b, 0, j)),
            pl.BlockSpec((1, PAST_LEN, LANES), lambda b, j, r: (b, 0, j)),
            pl.BlockSpec((1, 2, q_rows, NA_UNION * GRID_W), lambda b, j, r: (_na_bias_variant(r), j, 0, 0)),
        ],
        out_specs=pl.BlockSpec((q_rows, LANES), lambda b, j, r: (b * n_rb + r, j)),
        out_shape=jax.ShapeDtypeStruct((N_SAMPLE, qw), F32),
        compiler_params=_cparams(("arbitrary", "arbitrary", "arbitrary")),
    )(qkv, qkv, qkv, k_ctx, v_ctx, bias)


def _na_bias_table(rel_bias):
    cols = np.arange(GRID_W)
    col_start = np.clip(cols - NA_WIN_COLS // 2, 0, GRID_W - NA_WIN_COLS)
    col_ok = (cols[None, :] >= col_start[:, None]) & (cols[None, :] < col_start[:, None] + NA_WIN_COLS)
    dc = np.clip(cols[None, :] - cols[:, None] + NA_WIN_COLS - 1, 0, 2 * NA_WIN_COLS - 2)
    bias_cols = jnp.where(col_ok[None, None], rel_bias.astype(F32)[:, :, dc] * LOG2E, NEG_INF)
    n_rb = GRID_ROWS // NA_QROWS
    dr = np.zeros((3, NA_QROWS, NA_UNION), np.int32)
    ok = np.zeros((3, NA_QROWS, NA_UNION), bool)
    for variant, rb in enumerate((0, 1, n_rb - 1)):
        u0 = int(np.clip(rb * NA_QROWS - NA_WIN_ROWS // 2, 0, GRID_ROWS - NA_UNION))
        for j in range(NA_QROWS):
            r = rb * NA_QROWS + j
            start = int(np.clip(r - NA_WIN_ROWS // 2, 0, GRID_ROWS - NA_WIN_ROWS))
            for i in range(NA_UNION):
                key_row = u0 + i
                ok[variant, j, i] = start <= key_row < start + NA_WIN_ROWS
                dr[variant, j, i] = np.clip(key_row - r + NA_WIN_ROWS - 1, 0, 2 * NA_WIN_ROWS - 2)
    t = bias_cols[:, dr]
    t = jnp.where(ok[None, :, :, :, None, None], t, NEG_INF)
    t = t.transpose(1, 0, 2, 4, 3, 5)
    return t.reshape(3, NA_HEADS, NA_QROWS * GRID_W, NA_UNION * GRID_W)


GQA_Q_BLOCK = GQA_WINDOW
GQA_GROUP = GQA_HEADS // GQA_KV_HEADS


def _gqa_latent_kernel(sink_ref, q_ref, kp_ref, kc_ref, kn_ref, vp_ref, vc_ref, vn_ref,
                       kx_ref, vx_ref, o_ref):
    pair = pl.program_id(1)
    qb = pl.program_id(2)
    n_qb = pl.num_programs(2)
    blk = GQA_Q_BLOCK
    rows = GQA_GROUP * blk
    qq = lax.broadcasted_iota(jnp.int32, (rows, blk), 0) % blk
    kk = lax.broadcasted_iota(jnp.int32, (rows, blk), 1)
    neg = jnp.float32(NEG_INF)
    bias_prev = jnp.where((kk >= qq) & (qb > 0), 0.0, neg)
    bias_next = jnp.where((kk <= qq) & (qb < n_qb - 1), 0.0, neg)
    row_grp = lax.broadcasted_iota(jnp.int32, (rows, 1), 0) // blk
    q = jnp.concatenate([q_ref[:, m * LANES:(m + 1) * LANES] for m in range(GQA_GROUP)], axis=0)
    sinks = []
    for h in range(2):
        base = pair * 2 * GQA_GROUP + h * GQA_GROUP
        col = jnp.zeros((rows, 1), F32)
        for m in range(GQA_GROUP):
            col = jnp.where(row_grp == m, sink_ref[base + m] * LOG2E, col)
        sinks.append(col)
    o = _pair_attention(
        q, _half_masks(),
        [kp_ref[...].astype(BF16), kc_ref[...].astype(BF16), kn_ref[...].astype(BF16), kx_ref[0].astype(BF16)],
        [vp_ref[...].astype(BF16), vc_ref[...].astype(BF16), vn_ref[...].astype(BF16), vx_ref[0].astype(BF16)],
        [lambda h: bias_prev, None, lambda h: bias_next, None], HEAD_SCALE, sinks)
    for m in range(GQA_GROUP):
        o_ref[:, m * LANES:(m + 1) * LANES] = o[m * blk:(m + 1) * blk]


def _gqa_latent_attn(qkv, sink, k_ctx, v_ctx):
    qw = GQA_HEADS * GQA_HEAD_DIM
    blk = GQA_Q_BLOCK
    n_qb = DEC_SEQ // blk
    q_off = N_PROMPT // blk
    q_cols = GQA_GROUP * LANES
    n_pairs = GQA_KV_HEADS // 2
    k_col = qw // LANES
    v_col = k_col + n_pairs

    def kv_spec(col0, shift):
        def imap(b, p, i, s):
            return (q_off + b * n_qb + jnp.clip(i + shift, 0, n_qb - 1), col0 + p)
        return pl.BlockSpec((blk, LANES), imap)

    ctx_spec = pl.BlockSpec((1, PAST_LEN, LANES), lambda b, p, i, s: (b, 0, p))
    return pl.pallas_call(
        _gqa_latent_kernel,
        grid_spec=pltpu.PrefetchScalarGridSpec(
            num_scalar_prefetch=1,
            grid=(DEC_BATCH, n_pairs, n_qb),
            in_specs=[
                pl.BlockSpec((blk, q_cols), lambda b, p, i, s: (q_off + b * n_qb + i, p)),
                kv_spec(k_col, -1), kv_spec(k_col, 0), kv_spec(k_col, 1),
                kv_spec(v_col, -1), kv_spec(v_col, 0), kv_spec(v_col, 1),
                ctx_spec, ctx_spec,
            ],
            out_specs=pl.BlockSpec((blk, q_cols), lambda b, p, i, s: (b * n_qb + i, p)),
        ),
        out_shape=jax.ShapeDtypeStruct((N_SAMPLE, qw), F32),
        compiler_params=_cparams(("arbitrary", "arbitrary", "arbitrary")),
    )(sink, qkv, qkv, qkv, qkv, qkv, qkv, qkv, k_ctx, v_ctx)


MLA_PAIRS = MLA_HEADS // 2


def _mla_masks():
    lane = lax.broadcasted_iota(jnp.int32, (1, 2 * LANES), 1)
    half = LANES // 2
    even = (lane < half) | ((lane >= LANES) & (lane < LANES + MLA_ROPE))
    odd = ((lane >= half) & (lane < LANES)) | ((lane >= LANES + MLA_ROPE) & (lane < LANES + 2 * MLA_ROPE))
    return [even, odd]


def _mla_ctx_kernel(q_ref, kv_ref, kpe_ref, o_ref):
    masks = _mla_masks()
    kpe = kpe_ref[...].astype(BF16)
    half = MLA_Q_COLS // 2
    for j in range(MLA_PAIRS):
        cols = slice(j * LANES, (j + 1) * LANES)
        rcols = slice(half + j * LANES, half + (j + 1) * LANES)
        q = jnp.concatenate([q_ref[:, cols], q_ref[:, rcols]], axis=1)
        k = jnp.concatenate([kv_ref[:, cols], kpe], axis=1)
        o_ref[:, cols] = _pair_attention(q, masks, [k], [kv_ref[:, rcols]], [None], 1.0, None)


def _mla_ctx_attn(q, kv, kpe):
    ow = MLA_HEADS * MLA_V
    return pl.pallas_call(
        _mla_ctx_kernel,
        grid=(BATCH,),
        in_specs=[
            pl.BlockSpec((SEQ, MLA_Q_COLS), lambda b: (b, 0)),
            pl.BlockSpec((SEQ, MLA_Q_COLS), lambda b: (b, 0)),
            pl.BlockSpec((SEQ, LANES), lambda b: (b, 0)),
        ],
        out_specs=pl.BlockSpec((SEQ, ow), lambda b: (b, 0)),
        out_shape=jax.ShapeDtypeStruct((N_PROMPT, ow), F32),
        compiler_params=_cparams(("arbitrary",)),
    )(q, kv, kpe)


MLA_Q_BLOCK = 256


def _mla_latent_kernel(qn_ref, qr_ref, ka_ref, kpe_ref, v_ref, o_ref, kcat_ref):
    @pl.when(pl.program_id(2) == 0)
    def _():
        kcat_ref[:, :LANES] = ka_ref[...]
        kcat_ref[:, LANES:] = kpe_ref[...].astype(BF16)

    q = jnp.concatenate([qn_ref[...], qr_ref[...]], axis=1)
    o_ref[...] = _pair_attention(q, _mla_masks(), [kcat_ref[...]], [v_ref[...]], [None], 1.0, None)


def _mla_latent_attn(q, kv, kpe):
    ow = MLA_HEADS * MLA_V
    n_qb = DEC_SEQ // MLA_Q_BLOCK
    q_off = N_PROMPT // MLA_Q_BLOCK
    return pl.pallas_call(
        _mla_latent_kernel,
        grid=(DEC_BATCH, MLA_PAIRS, n_qb),
        in_specs=[
            pl.BlockSpec((MLA_Q_BLOCK, LANES), lambda b, j, i: (q_off + b * n_qb + i, j)),
            pl.BlockSpec((MLA_Q_BLOCK, LANES), lambda b, j, i: (q_off + b * n_qb + i, MLA_PAIRS + j)),
            pl.BlockSpec((MLA_KEYS, LANES), lambda b, j, i: (b, j)),
            pl.BlockSpec((MLA_KEYS, LANES), lambda b, j, i: (b, 0)),
            pl.BlockSpec((MLA_KEYS, LANES), lambda b, j, i: (b, MLA_PAIRS + j)),
        ],
        out_specs=pl.BlockSpec((MLA_Q_BLOCK, LANES), lambda b, j, i: (b * n_qb + i, j)),
        out_shape=jax.ShapeDtypeStruct((N_SAMPLE, ow), F32),
        scratch_shapes=[pltpu.VMEM((MLA_KEYS, 2 * LANES), BF16)],
        compiler_params=_cparams(("arbitrary", "arbitrary", "arbitrary")),
    )(q, q, kv, kpe, kv)


def _out_proj_kernel(x_ref, o_ref, gate_ref, w_ref, y_ref):
    y_ref[...] = x_ref[...] + gate_ref[0] * _dot(o_ref[...].astype(BF16), w_ref[...])


def _out_proj(x, o_prompt, o_sample, gate, w):
    n_p = N_PROMPT // ROW_TILE
    k = w.shape[0]
    last_p = n_p - 1
    return pl.pallas_call(
        _out_proj_kernel_two,
        grid=(N_TOK // ROW_TILE,),
        in_specs=[
            pl.BlockSpec((ROW_TILE, D_MODEL), lambda i: (i, 0)),
            pl.BlockSpec((ROW_TILE, k), lambda i: (jnp.minimum(i, last_p), 0)),
            pl.BlockSpec((ROW_TILE, k), lambda i: (jnp.maximum(i - n_p, 0), 0)),
            _mod_spec(ROW_TILE),
            _resident(w.shape),
        ],
        out_specs=pl.BlockSpec((ROW_TILE, D_MODEL), lambda i: (i, 0)),
        out_shape=jax.ShapeDtypeStruct((N_TOK, D_MODEL), F32),
        compiler_params=_cparams(("arbitrary",)),
    )(x, o_prompt, o_sample, gate, w)


def _out_proj_kernel_two(x_ref, op_ref, os_ref, gate_ref, w_ref, y_ref):
    is_prompt = pl.program_id(0) < N_PROMPT // ROW_TILE

    @pl.when(is_prompt)
    def _():
        _out_proj_kernel(x_ref, op_ref, gate_ref, w_ref, y_ref)

    @pl.when(jnp.logical_not(is_prompt))
    def _():
        _out_proj_kernel(x_ref, os_ref, gate_ref, w_ref, y_ref)


def _ffn_kernel(x_ref, g_ref, sh_ref, sc_ref, gate_ref, wg_ref, wu_ref, wd_ref, y_ref, acc_ref):
    x = x_ref[...]
    hb = (_rms(x, g_ref[...]) * (1.0 + sc_ref[0]) + sh_ref[0]).astype(BF16)
    acc_ref[...] = jnp.zeros_like(acc_ref)

    def chunk(c, carry):
        a = _silu(_dot(hb, wg_ref[c])) * _dot(hb, wu_ref[c])
        acc_ref[...] += _dot(a.astype(BF16), wd_ref[c])
        return carry

    lax.fori_loop(0, N_FFN_CHUNKS, chunk, 0)
    y_ref[...] = x + gate_ref[0] * acc_ref[...]


def _ffn(x, gain, shift, scale, gate, wg, wu, wd):
    return pl.pallas_call(
        _ffn_kernel,
        grid=(N_TOK // ROW_TILE,),
        in_specs=[
            pl.BlockSpec((ROW_TILE, D_MODEL), lambda i: (i, 0)),
            pl.BlockSpec((1, D_MODEL), lambda i: (0, 0)),
            _mod_spec(ROW_TILE), _mod_spec(ROW_TILE), _mod_spec(ROW_TILE),
            _resident(wg.shape), _resident(wu.shape), _resident(wd.shape),
        ],
        out_specs=pl.BlockSpec((ROW_TILE, D_MODEL), lambda i: (i, 0)),
        out_shape=jax.ShapeDtypeStruct((N_TOK, D_MODEL), F32),
        scratch_shapes=[pltpu.VMEM((ROW_TILE, D_MODEL), F32)],
        compiler_params=_cparams(("arbitrary",)),
    )(x, gain, shift, scale, gate, wg, wu, wd)


SLAB = D_MODEL // LANES


def _store_slabs(ref, x):
    n = x.shape[0]
    for s in range(SLAB):
        ref[pl.ds(s, n, stride=SLAB), :] = x[:, s * LANES:(s + 1) * LANES]


def _load_slabs(ref, n, dtype, row0=0):
    return jnp.concatenate([ref[pl.ds(row0 + s, n, stride=SLAB), :].astype(dtype) for s in range(SLAB)], axis=1)


META_E1, META_E2, META_W1, META_W2, META_R1, META_R2 = range(6)


def _router_kernel(x_ref, g_ref, sh_ref, sc_ref, whi_ref, wlo_ref, tri_ref, h_ref, meta_ref, cnt_ref, carry_ref):
    @pl.when(pl.program_id(0) == 0)
    def _():
        carry_ref[...] = jnp.zeros_like(carry_ref)

    h = _rms(x_ref[...], g_ref[...]) * (1.0 + sc_ref[0]) + sh_ref[0]
    _store_slabs(h_ref, h)
    h_hi, h_lo = _split_bf16(h)
    logits = _dot(h_hi, whi_ref[...]) + _dot(h_lo, whi_ref[...]) + _dot(h_hi, wlo_ref[...])
    lane = lax.broadcasted_iota(jnp.int32, logits.shape, 1)
    lane_f = lane.astype(F32)
    ninf = jnp.float32(-jnp.inf)
    lg = jnp.where(lane < N_EXPERTS, logits, ninf)
    m1 = jnp.max(lg, axis=-1, keepdims=True)
    i1 = jnp.min(jnp.where(lg == m1, lane_f, float(LANES)), axis=-1, keepdims=True)
    lg2 = jnp.where(lane_f == i1, ninf, lg)
    m2 = jnp.max(lg2, axis=-1, keepdims=True)
    i2 = jnp.min(jnp.where(lg2 == m2, lane_f, float(LANES)), axis=-1, keepdims=True)
    e = jnp.exp(m2 - m1)
    w1 = 1.0 / (1.0 + e)
    w2 = e / (1.0 + e)
    sel1 = lane_f == i1
    sel2 = lane_f == i2
    onehot = jnp.where(sel1 | sel2, 1.0, 0.0)
    ranks = _dot(tri_ref[...], onehot.astype(BF16)) + carry_ref[...]
    r1 = jnp.sum(jnp.where(sel1, ranks, 0.0), axis=-1, keepdims=True)
    r2 = jnp.sum(jnp.where(sel2, ranks, 0.0), axis=-1, keepdims=True)
    carry_ref[...] += jnp.sum(onehot, axis=0, keepdims=True)
    cnt_ref[...] = carry_ref[...]
    meta = jnp.zeros(logits.shape, F32)
    for idx, val in ((META_E1, i1), (META_E2, i2), (META_W1, w1), (META_W2, w2), (META_R1, r1), (META_R2, r2)):
        meta = jnp.where(lane == idx, val, meta)
    meta_ref[...] = meta


def _router(x, gain, shift, scale, w_hi, w_lo, tri):
    return pl.pallas_call(
        _router_kernel,
        grid=(N_TOK // ROW_TILE,),
        in_specs=[
            pl.BlockSpec((ROW_TILE, D_MODEL), lambda i: (i, 0)),
            pl.BlockSpec((1, D_MODEL), lambda i: (0, 0)),
            _mod_spec(ROW_TILE), _mod_spec(ROW_TILE),
            _resident(w_hi.shape), _resident(w_lo.shape), _resident(tri.shape),
        ],
        out_specs=[
            pl.BlockSpec((ROW_TILE * SLAB, LANES), lambda i: (i, 0)),
            pl.BlockSpec((ROW_TILE, LANES), lambda i: (i, 0)),
            pl.BlockSpec((1, LANES), lambda i: (0, 0)),
        ],
        out_shape=[
            jax.ShapeDtypeStruct((N_TOK * SLAB, LANES), F32),
            jax.ShapeDtypeStruct((N_TOK, LANES), F32),
            jax.ShapeDtypeStruct((1, LANES), F32),
        ],
        scratch_shapes=[pltpu.VMEM((1, LANES), F32)],
        compiler_params=_cparams(("arbitrary",)),
    )(x, gain, shift, scale, w_hi, w_lo, tri)


def _route_plan(meta, counts):
    cnt = counts[0, :N_EXPERTS].astype(jnp.int32)
    tiles = (cnt + EXP_TILE - 1) // EXP_TILE
    tile_end = jnp.cumsum(tiles)
    tile_start = tile_end - tiles
    offs = tile_start * EXP_TILE
    e1 = meta[:, META_E1].astype(jnp.int32)
    e2 = meta[:, META_E2].astype(jnp.int32)
    pos1 = offs[e1] + meta[:, META_R1].astype(jnp.int32)
    pos2 = offs[e2] + meta[:, META_R2].astype(jnp.int32)
    t = jnp.arange(N_EXP_TILES, dtype=jnp.int32)
    tile_expert = jnp.minimum(jnp.sum(t[:, None] >= tile_end[None, :], axis=1), N_EXPERTS - 1).astype(jnp.int32)
    in_group = (t - tile_start[tile_expert]) * EXP_TILE
    tile_rows = jnp.where(t < tile_end[-1], jnp.clip(cnt[tile_expert] - in_group, 0, EXP_TILE), 0).astype(jnp.int32)
    return pos1, pos2, tile_expert, tile_rows


def _dispatch_kernel(pos1_ref, pos2_ref, h_ref, init_ref, xs_ref, sem):
    del init_ref
    base = pl.program_id(0) * DISPATCH_TILE

    def row_copy(t, dst):
        return pltpu.make_async_copy(
            h_ref.at[pl.ds(pl.multiple_of(t * SLAB, SLAB), SLAB)],
            xs_ref.at[pl.ds(pl.multiple_of(dst * SLAB, SLAB), SLAB)], sem)

    def issue(t, carry):
        row_copy(t, pos1_ref[base + t]).start()
        row_copy(t, pos2_ref[base + t]).start()
        return carry

    lax.fori_loop(0, DISPATCH_TILE, issue, 0)

    def drain(t, carry):
        row_copy(0, 0).wait()
        row_copy(0, 0).wait()
        return carry

    lax.fori_loop(0, DISPATCH_TILE, drain, 0)


def _dispatch(pos1, pos2, h_slabs, xs_init):
    return pl.pallas_call(
        _dispatch_kernel,
        grid_spec=pltpu.PrefetchScalarGridSpec(
            num_scalar_prefetch=2,
            grid=(N_TOK // DISPATCH_TILE,),
            in_specs=[pl.BlockSpec((DISPATCH_TILE * SLAB, LANES), lambda i, p1, p2: (i, 0)),
                      pl.BlockSpec(memory_space=pl.ANY)],
            out_specs=pl.BlockSpec(memory_space=pl.ANY),
            scratch_shapes=[pltpu.SemaphoreType.DMA(())],
        ),
        out_shape=jax.ShapeDtypeStruct(xs_init.shape, F32),
        input_output_aliases={3: 0},
        compiler_params=_cparams(("arbitrary",)),
    )(pos1, pos2, h_slabs, xs_init)


def _expert_kernel(te_ref, tr_ref, xs_ref, wg_ref, wu_ref, wd_ref, ys_ref, xb_ref, acc_ref):
    t = pl.program_id(0)
    c = pl.program_id(1)

    @pl.when(c == 0)
    def _():
        xb_ref[...] = _load_slabs(xs_ref, EXP_TILE, BF16)
        acc_ref[...] = jnp.zeros_like(acc_ref)

    rows = tr_ref[t]

    @pl.when(rows > 0)
    def _():
        wg = wg_ref[...].astype(BF16)
        wu = wu_ref[...].astype(BF16)
        wd = wd_ref[...].astype(BF16)
        for part in range(EXP_TILE // EXP_PART):
            @pl.when(rows > part * EXP_PART)
            def _():
                sl = slice(part * EXP_PART, (part + 1) * EXP_PART)
                xb = xb_ref[sl]
                a = _silu(_dot(xb, wg)) * _dot(xb, wu)
                acc_ref[sl] += _dot(a.astype(BF16), wd)

    @pl.when(c == N_EXP_CHUNKS - 1)
    def _():
        _store_slabs(ys_ref, acc_ref[...])


def _experts(layer, tile_expert, tile_rows, xs, wg, wu, wd):
    def chunk_of(t, c, tr):
        return jnp.where(tr[t] > 0, c, N_EXP_CHUNKS - 1)

    return pl.pallas_call(
        _expert_kernel,
        grid_spec=pltpu.PrefetchScalarGridSpec(
            num_scalar_prefetch=2,
            grid=(N_EXP_TILES, N_EXP_CHUNKS),
            in_specs=[
                pl.BlockSpec((EXP_TILE * SLAB, LANES), lambda t, c, te, tr: (t, 0)),
                pl.BlockSpec((None, None, D_MODEL, EXP_CHUNK),
                             lambda t, c, te, tr: (layer, te[t], 0, chunk_of(t, c, tr))),
                pl.BlockSpec((None, None, D_MODEL, EXP_CHUNK),
                             lambda t, c, te, tr: (layer, te[t], 0, chunk_of(t, c, tr))),
                pl.BlockSpec((None, None, EXP_CHUNK, D_MODEL),
                             lambda t, c, te, tr: (layer, te[t], chunk_of(t, c, tr), 0)),
            ],
            out_specs=pl.BlockSpec((EXP_TILE * SLAB, LANES), lambda t, c, te, tr: (t, 0)),
            scratch_shapes=[pltpu.VMEM((EXP_TILE, D_MODEL), BF16), pltpu.VMEM((EXP_TILE, D_MODEL), F32)],
        ),
        out_shape=jax.ShapeDtypeStruct(xs.shape, F32),
        compiler_params=_cparams(("arbitrary", "arbitrary")),
    )(tile_expert, tile_rows, xs, wg, wu, wd)


def _combine_kernel(pos1_ref, pos2_ref, x_ref, gate_ref, meta_ref, ys_ref, y_ref, buf_ref, sem):
    base = pl.program_id(0) * COMBINE_TILE
    n = COMBINE_TILE

    def row_copy(src, slot):
        return pltpu.make_async_copy(
            ys_ref.at[pl.ds(pl.multiple_of(src * SLAB, SLAB), SLAB)],
            buf_ref.at[pl.ds(pl.multiple_of(slot * SLAB, SLAB), SLAB)], sem)

    def issue(t, carry):
        row_copy(pos1_ref[base + t], t).start()
        row_copy(pos2_ref[base + t], n + t).start()
        return carry

    lax.fori_loop(0, n, issue, 0)

    def drain(t, carry):
        row_copy(0, 0).wait()
        row_copy(0, 0).wait()
        return carry

    lax.fori_loop(0, n, drain, 0)

    meta = meta_ref[...]
    w1 = meta[:, META_W1:META_W1 + 1]
    w2 = meta[:, META_W2:META_W2 + 1]
    y1 = _load_slabs(buf_ref, n, F32)
    y2 = _load_slabs(buf_ref, n, F32, row0=n * SLAB)
    y_ref[...] = x_ref[...] + gate_ref[0] * (w1 * y1 + w2 * y2)


def _combine(pos1, pos2, x, gate, meta, ys):
    return pl.pallas_call(
        _combine_kernel,
        grid_spec=pltpu.PrefetchScalarGridSpec(
            num_scalar_prefetch=2,
            grid=(N_TOK // COMBINE_TILE,),
            in_specs=[
                pl.BlockSpec((COMBINE_TILE, D_MODEL), lambda i, p1, p2: (i, 0)),
                pl.BlockSpec((1, 1, D_MODEL), lambda i, p1, p2: (_cond_of_row(i * COMBINE_TILE), 0, 0)),
                pl.BlockSpec((COMBINE_TILE, LANES), lambda i, p1, p2: (i, 0)),
                pl.BlockSpec(memory_space=pl.ANY),
            ],
            out_specs=pl.BlockSpec((COMBINE_TILE, D_MODEL), lambda i, p1, p2: (i, 0)),
            scratch_shapes=[pltpu.VMEM((2 * COMBINE_TILE * SLAB, LANES), F32), pltpu.SemaphoreType.DMA(())],
        ),
        out_shape=jax.ShapeDtypeStruct((N_TOK, D_MODEL), F32),
        compiler_params=_cparams(("arbitrary",)),
    )(pos1, pos2, x, gate, meta, ys)


def _moe(layer, x, gain, shift, scale, gate, router_w, wg, wu, wd, tri):
    w = jnp.pad(router_w, ((0, 0), (0, LANES - N_EXPERTS)))
    w_hi = w.astype(BF16)
    w_lo = (w - w_hi.astype(F32)).astype(BF16)
    h_slabs, meta, counts = _router(x, gain, shift, scale, w_hi, w_lo, tri)
    pos1, pos2, tile_expert, tile_rows = _route_plan(meta, counts)
    xs = _dispatch(pos1, pos2, h_slabs, jnp.zeros((N_EXP_TILES * EXP_TILE * SLAB, LANES), F32))
    ys = _experts(layer, tile_expert, tile_rows, xs, wg, wu, wd)
    return _combine(pos1, pos2, x, gate, meta, ys)


def _final_norm_kernel(x_ref, g_ref, y_ref):
    y_ref[...] = _rms(x_ref[...], g_ref[...])


def _final_norm(x, gain, row0, rows):
    off = row0 // ROW_TILE
    return pl.pallas_call(
        _final_norm_kernel,
        grid=(rows // ROW_TILE,),
        in_specs=[
            pl.BlockSpec((ROW_TILE, D_MODEL), lambda i: (off + i, 0)),
            pl.BlockSpec((1, D_MODEL), lambda i: (0, 0)),
        ],
        out_specs=pl.BlockSpec((ROW_TILE, D_MODEL), lambda i: (i, 0)),
        out_shape=jax.ShapeDtypeStruct((rows, D_MODEL), F32),
        compiler_params=_cparams(("arbitrary",)),
    )(x, gain)


def _rope_tables(rot_dim, reps):
    n_freq = rot_dim // 4
    inv_freq = ROPE_THETA ** (-np.arange(n_freq, dtype=np.float64) / n_freq)
    t = np.arange(DEC_SEQ)
    ang_r = (t // GRID_W)[:, None] * inv_freq
    ang_c = (t % GRID_W)[:, None] * inv_freq
    cos = np.concatenate([np.cos(ang_r), np.cos(ang_r), np.cos(ang_c), np.cos(ang_c)], axis=1)
    sin = np.concatenate([-np.sin(ang_r), np.sin(ang_r), -np.sin(ang_c), np.sin(ang_c)], axis=1)
    pad = LANES - reps * rot_dim
    cos = np.concatenate([cos] * reps + [np.ones((DEC_SEQ, pad))], axis=1)
    sin = np.concatenate([sin] * reps + [np.zeros((DEC_SEQ, pad))], axis=1)
    cos = np.concatenate([cos, np.ones((ROW_TILE, LANES))], axis=0)
    sin = np.concatenate([sin, np.zeros((ROW_TILE, LANES))], axis=0)
    return jnp.asarray(cos, F32), jnp.asarray(sin, F32)


def _swap_index(rot_dim):
    q = rot_dim // 4
    idx = np.arange(rot_dim)
    return np.where((idx // q) % 2 == 0, idx + q, idx - q)


def _gqa_q_perm():
    heads = []
    for p in range(GQA_KV_HEADS // 2):
        for m in range(GQA_GROUP):
            heads += [2 * p * GQA_GROUP + m, (2 * p + 1) * GQA_GROUP + m]
    return np.concatenate([np.arange(h * GQA_HEAD_DIM, (h + 1) * GQA_HEAD_DIM) for h in heads])


def _mla_layouts(wq_b, wkv_a, wkv_b):
    hd = MLA_NOPE + MLA_ROPE
    nope_cols = np.concatenate([np.arange(h * hd, h * hd + MLA_NOPE) for h in range(MLA_HEADS)])
    swap = _swap_index(MLA_ROPE)
    zeros_q = jnp.zeros((MLA_Q_LORA, LANES - 2 * MLA_ROPE), F32)
    rope_blocks, rope_swapped = [], []
    for j in range(MLA_PAIRS):
        for blocks, idx in ((rope_blocks, np.arange(MLA_ROPE)), (rope_swapped, swap)):
            blocks += [wq_b[:, (2 * j) * hd + MLA_NOPE + idx], wq_b[:, (2 * j + 1) * hd + MLA_NOPE + idx], zeros_q]
    wb = jnp.concatenate([wq_b[:, nope_cols]] + rope_blocks, axis=1)
    wbs = jnp.concatenate(rope_swapped, axis=1)
    zeros_a = jnp.zeros((D_MODEL, LANES - 2 * MLA_ROPE), F32)
    kpe_w = wkv_a[:, MLA_KV_LORA:]
    kpe_ws = kpe_w[:, swap]
    wa_tail = jnp.concatenate([wkv_a[:, :MLA_KV_LORA], kpe_w, kpe_w, zeros_a, kpe_ws, kpe_ws, zeros_a], axis=1)
    kvd = MLA_NOPE + MLA_V
    k_cols = np.concatenate([np.arange(h * kvd, h * kvd + MLA_NOPE) for h in range(MLA_HEADS)])
    v_cols = k_cols + MLA_NOPE
    wkvb = jnp.concatenate([wkv_b[:, k_cols], wkv_b[:, v_cols]], axis=1)
    return wb, wbs, wa_tail, wkvb


def _chunk_cols(w, chunk):
    k, n = w.shape
    return w.reshape(k, n // chunk, chunk).transpose(1, 0, 2).astype(BF16)


def kernel(x_prompt, x_sample, cache_l0_k, cache_l0_v, cache_l1_ckv, cache_l1_kpe, cache_l2_k, cache_l2_v, cache_l3_k, cache_l3_v, c, c_ctx, ada_w, ada_b, norm_mix, norm_ffn, norm_final, na_w_qkv, na_w_o, na_rel_bias, mla_wq_a, mla_q_norm, mla_wq_b, mla_wkv_a, mla_kv_norm, mla_wkv_b, mla_w_o, gqa_w_qkv, gqa_w_o, gqa_sink, ffn_w_gate, ffn_w_up, ffn_w_down, moe_router, moe_w_gate, moe_w_up, moe_w_down):
    x = jnp.concatenate([x_prompt.reshape(N_PROMPT, D_MODEL), x_sample.reshape(N_SAMPLE, D_MODEL)], axis=0)
    cond = jnp.concatenate([c_ctx[None, :], c, jnp.zeros((N_COND - 1 - DEC_BATCH, D_MODEL), F32)], axis=0)
    mods = _adaln(cond, ada_w, ada_b).reshape(DEPTH, N_COND, 6, 1, D_MODEL)
    tri = jnp.asarray(np.tril(np.ones((ROW_TILE, ROW_TILE), np.float32), -1), BF16)
    no_sink = jnp.zeros((GQA_HEADS,), F32)
    na_caches = {0: (cache_l0_k, cache_l0_v), 3: (cache_l3_k, cache_l3_v)}
    state = []

    for i in range(DEPTH):
        sh_m, sc_m, g_m, sh_f, sc_f, g_f = [mods[i, :, j] for j in range(6)]
        gain_m = norm_mix[i][None, :]
        gain_f = norm_ffn[i][None, :]
        kind, s = i % 3, i // 3
        if kind == 0:
            qkv = _norm_proj(x, gain_m, sh_m, sc_m, na_w_qkv[s].astype(BF16))
            qw = NA_HEADS * NA_HEAD_DIM
            o_p = _ctx_attn(qkv, no_sink, qw, False)
            ck, cv = na_caches[i]
            o_s = _na_latent_attn(qkv, ck.reshape(DEC_BATCH, PAST_LEN, qw), cv.reshape(DEC_BATCH, PAST_LEN, qw),
                                  _na_bias_table(na_rel_bias[s]))
            w_o = na_w_o[s].astype(BF16)
            kv_shape = (BATCH, SEQ, NA_HEADS, NA_HEAD_DIM)
            state += [qkv[:N_PROMPT, qw:2 * qw].reshape(kv_shape), qkv[:N_PROMPT, 2 * qw:].reshape(kv_shape)]
        elif kind == 1:
            wb, wbs, wa_tail, wkvb = _mla_layouts(mla_wq_b[s], mla_wkv_a[s], mla_wkv_b[s])
            wa = jnp.concatenate([mla_wq_a[s], wa_tail], axis=1).astype(BF16)
            cos, sin = _rope_tables(MLA_ROPE, 2)
            q, ckv, kpe = _mla_proj(x, gain_m, sh_m, sc_m, wa, mla_q_norm[s][None, :], mla_kv_norm[s][None, :],
                                    wb.astype(BF16), wbs.astype(BF16), cos, sin)
            wkvb = wkvb.astype(BF16)
            kv_p = _rows_matmul(ckv[:N_PROMPT], wkvb, BF16)
            o_p = _mla_ctx_attn(q, kv_p, kpe)
            ckv_s = ckv[N_PROMPT:].reshape(DEC_BATCH, DEC_SEQ, MLA_KV_LORA)
            ckv_all = jnp.concatenate([cache_l1_ckv, ckv_s], axis=1).reshape(DEC_BATCH * MLA_KEYS, MLA_KV_LORA)
            kv_s = _rows_matmul(ckv_all, wkvb, BF16)
            kpe_cache = jnp.concatenate(
                [cache_l1_kpe, cache_l1_kpe, jnp.zeros((DEC_BATCH, PAST_LEN, LANES - 2 * MLA_ROPE), F32)], axis=2)
            kpe_all = jnp.concatenate([kpe_cache, kpe[N_PROMPT:].reshape(DEC_BATCH, DEC_SEQ, LANES)], axis=1)
            o_s = _mla_latent_attn(q, kv_s, kpe_all.reshape(DEC_BATCH * MLA_KEYS, LANES))
            w_o = mla_w_o[s].astype(BF16)
            state += [ckv[:N_PROMPT].reshape(BATCH, SEQ, MLA_KV_LORA),
                      kpe[:N_PROMPT, :MLA_ROPE].reshape(BATCH, SEQ, MLA_ROPE)]
        else:
            qw = GQA_HEADS * GQA_HEAD_DIM
            kw = GQA_KV_HEADS * GQA_HEAD_DIM
            perm = _gqa_q_perm()
            w_qkv = jnp.concatenate([gqa_w_qkv[s][:, perm], gqa_w_qkv[s][:, qw:]], axis=1)
            swap = np.concatenate([h * GQA_HEAD_DIM + _swap_index(GQA_HEAD_DIM) for h in range((qw + kw) // GQA_HEAD_DIM)])
            cos, sin = _rope_tables(GQA_HEAD_DIM, 2)
            qkv = _norm_proj(x, gain_m, sh_m, sc_m, w_qkv.astype(BF16),
                             rope=(w_qkv[:, swap].astype(BF16), cos, sin))
            sink = gqa_sink[s]
            o_p = _ctx_attn(qkv, sink, kw, True)
            o_s = _gqa_latent_attn(qkv, sink, cache_l2_k.reshape(DEC_BATCH, PAST_LEN, kw),
                                   cache_l2_v.reshape(DEC_BATCH, PAST_LEN, kw))
            w_o = gqa_w_o[s][perm].astype(BF16)
            kv_shape = (BATCH, SEQ, GQA_KV_HEADS, GQA_HEAD_DIM)
            state += [qkv[:N_PROMPT, qw:qw + kw].reshape(kv_shape), qkv[:N_PROMPT, qw + kw:].reshape(kv_shape)]
        x = _out_proj(x, o_p, o_s, g_m, w_o)

        if i % 2 == 0:
            s = i // 2
            x = _ffn(x, gain_f, sh_f, sc_f, g_f, _chunk_cols(ffn_w_gate[s], FFN_CHUNK),
                     _chunk_cols(ffn_w_up[s], FFN_CHUNK),
                     ffn_w_down[s].reshape(N_FFN_CHUNKS, FFN_CHUNK, D_MODEL).astype(BF16))
        else:
            s = i // 2
            x = _moe(s, x, gain_f, sh_f, sc_f, g_f, moe_router[s], moe_w_gate, moe_w_up, moe_w_down, tri)

    y_prompt = _final_norm(x, norm_final[None, :], 0, N_PROMPT).reshape(BATCH, SEQ, D_MODEL)
    y_sample = _final_norm(x, norm_final[None, :], N_PROMPT, N_SAMPLE).reshape(DEC_BATCH, DEC_SEQ, D_MODEL)
    return (y_prompt, y_sample, *state)
```

```python
import functools
import math

import numpy as np
import jax
import jax.numpy as jnp
from jax import lax
from jax.experimental import pallas as pl
from jax.experimental.pallas import tpu as pltpu

F32 = jnp.float32
BF16 = jnp.bfloat16

D_MODEL = 1024
BATCH = 16
SEQ = 256
DEPTH = 4
DEC_BATCH = 4
DEC_SEQ = 2048
PAST_LEN = 512
GRID_W = 64
GRID_ROWS = DEC_SEQ // GRID_W
NORM_EPS = 1e-6
NEG_INF = -1e30
ROPE_THETA = 10000.0
NA_HEADS = 16
NA_HEAD_DIM = 64
NA_WIN_ROWS = 8
NA_WIN_COLS = 16
MLA_HEADS = 16
MLA_Q_LORA = 512
MLA_KV_LORA = 256
MLA_NOPE = 64
MLA_ROPE = 32
MLA_V = 64
GQA_HEADS = 16
GQA_KV_HEADS = 4
GQA_HEAD_DIM = 64
GQA_WINDOW = 128
FFN_DIM = 2816
N_EXPERTS = 8
EXPERT_DIM = 3584

N_PROMPT = BATCH * SEQ
N_SAMPLE = DEC_BATCH * DEC_SEQ
N_TOK = N_PROMPT + N_SAMPLE
N_COND = 8
LOG2E = math.log2(math.e)
HEAD_SCALE = NA_HEAD_DIM ** -0.5 * LOG2E
MLA_SCALE = (MLA_NOPE + MLA_ROPE) ** -0.5 * LOG2E

LANES = 128
SUBLANES = 8
VMEM_LIMIT = 56 * 1024 * 1024

ROW_TILE = 512
FFN_CHUNK = 256
N_FFN_CHUNKS = FFN_DIM // FFN_CHUNK
EXP_TILE = 1024
EXP_PART = 512
EXP_CHUNK = 512
N_EXP_CHUNKS = EXPERT_DIM // EXP_CHUNK
N_EXP_TILES = (2 * N_TOK) // EXP_TILE + N_EXPERTS
DISPATCH_TILE = 512
COMBINE_TILE = 256
NA_QROWS = 4
NA_UNION = NA_QROWS + NA_WIN_ROWS
NA_PAIRS_PER_STEP = 2
MLA_KEYS = PAST_LEN + DEC_SEQ


def _cparams(sem):
    return pltpu.CompilerParams(dimension_semantics=sem, vmem_limit_bytes=VMEM_LIMIT)


def _cond_of_row(row):
    return jnp.where(row < N_PROMPT, 0, 1 + (row - N_PROMPT) // DEC_SEQ)


def _nt_dot(a, b):
    return lax.dot_general(a, b, (((1,), (1,)), ((), ())), preferred_element_type=F32)


def _dot(a, b):
    return jnp.dot(a, b, preferred_element_type=F32)


def _rms(x, gain):
    return x * lax.rsqrt(jnp.mean(x * x, axis=-1, keepdims=True) + NORM_EPS) * gain


def _silu(x):
    return x / (1.0 + jnp.exp(-x))


def _split_bf16(x):
    hi = x.astype(BF16)
    lo = (x - hi.astype(F32)).astype(BF16)
    return hi, lo


def _resident(shape):
    nd = len(shape)
    return pl.BlockSpec(shape, lambda *_: (0,) * nd, pipeline_mode=pl.Buffered(1))


def _mod_spec(tile):
    return pl.BlockSpec((1, 1, D_MODEL), lambda i: (_cond_of_row(i * tile), 0, 0))


ADA_TILE = 1536


def _adaln_kernel(c_ref, w_ref, b_ref, o_ref):
    a_hi, a_lo = _split_bf16(_silu(c_ref[...]))
    w_hi, w_lo = _split_bf16(w_ref[0])
    o_ref[0] = _dot(a_hi, w_hi) + _dot(a_lo, w_hi) + _dot(a_hi, w_lo) + b_ref[0]


def _adaln(cond, ada_w, ada_b):
    n = 6 * D_MODEL
    return pl.pallas_call(
        _adaln_kernel,
        grid=(DEPTH, n // ADA_TILE),
        in_specs=[
            pl.BlockSpec((N_COND, D_MODEL), lambda l, j: (0, 0)),
            pl.BlockSpec((1, D_MODEL, ADA_TILE), lambda l, j: (l, 0, j)),
            pl.BlockSpec((1, 1, ADA_TILE), lambda l, j: (l, 0, j)),
        ],
        out_specs=pl.BlockSpec((1, N_COND, ADA_TILE), lambda l, j: (l, 0, j)),
        out_shape=jax.ShapeDtypeStruct((DEPTH, N_COND, n), F32),
        compiler_params=_cparams(("arbitrary", "arbitrary")),
    )(cond, ada_w, ada_b.reshape(DEPTH, 1, n))


PROJ_COLS = 256


HEAD_DIM = NA_HEAD_DIM


def _norm_proj_kernel(x_ref, g_ref, sh_ref, sc_ref, w_ref, *rest, n_out, n_rope, kv_cols):
    rest = list(rest)
    if n_rope:
        w2_ref, cos_ref, sin_ref = rest[:3]
        rest = rest[3:]
    o_ref, k_ref, v_ref = rest
    h = _rms(x_ref[...], g_ref[...]) * (1.0 + sc_ref[0]) + sh_ref[0]
    hb = h.astype(BF16)
    is_prompt = pl.program_id(0) < N_PROMPT // ROW_TILE
    kv_heads = kv_cols // HEAD_DIM
    for j in range(n_out // PROJ_COLS):
        c0 = j * PROJ_COLS
        cols = slice(c0, c0 + PROJ_COLS)
        y = _dot(hb, w_ref[:, cols])
        if c0 < n_rope:
            ys = _dot(hb, w2_ref[:, cols])
            reps = PROJ_COLS // LANES
            cos = jnp.concatenate([cos_ref[...]] * reps, axis=1)
            sin = jnp.concatenate([sin_ref[...]] * reps, axis=1)
            y = y * cos + ys * sin
        o_ref[:, cols] = y
        for kv_ref, start in ((k_ref, n_out - 2 * kv_cols), (v_ref, n_out - kv_cols)):
            if start <= c0 < start + kv_cols:
                @pl.when(is_prompt)
                def _(kv_ref=kv_ref, start=start, y=y, c0=c0):
                    for t in range(PROJ_COLS // HEAD_DIM):
                        head = (c0 - start) // HEAD_DIM + t
                        kv_ref[pl.ds(head, ROW_TILE, stride=kv_heads), :] = y[:, t * HEAD_DIM:(t + 1) * HEAD_DIM]


def _rope_block(i):
    row = i * ROW_TILE
    return jnp.where(row < N_PROMPT, DEC_SEQ // ROW_TILE, ((row - N_PROMPT) % DEC_SEQ) // ROW_TILE)


def _norm_proj(x, gain, shift, scale, w, kv_cols, rope=None):
    n_out = w.shape[1]
    kv_heads = kv_cols // HEAD_DIM
    last_prompt = N_PROMPT // ROW_TILE - 1
    kv_spec = pl.BlockSpec((ROW_TILE * kv_heads, HEAD_DIM), lambda i: (jnp.minimum(i, last_prompt), 0))
    kv_shape = jax.ShapeDtypeStruct((N_PROMPT * kv_heads, HEAD_DIM), F32)
    in_specs = [
        pl.BlockSpec((ROW_TILE, D_MODEL), lambda i: (i, 0)),
        pl.BlockSpec((1, D_MODEL), lambda i: (0, 0)),
        _mod_spec(ROW_TILE),
        _mod_spec(ROW_TILE),
        _resident(w.shape),
    ]
    args = [x, gain, shift, scale, w]
    n_rope = 0
    if rope is not None:
        w2, cos, sin = rope
        n_rope = w2.shape[1]
        in_specs += [
            _resident(w2.shape),
            pl.BlockSpec((ROW_TILE, LANES), lambda i: (_rope_block(i), 0)),
            pl.BlockSpec((ROW_TILE, LANES), lambda i: (_rope_block(i), 0)),
        ]
        args += [w2, cos, sin]
    return pl.pallas_call(
        functools.partial(_norm_proj_kernel, n_out=n_out, n_rope=n_rope, kv_cols=kv_cols),
        grid=(N_TOK // ROW_TILE,),
        in_specs=in_specs,
        out_specs=[pl.BlockSpec((ROW_TILE, n_out), lambda i: (i, 0)), kv_spec, kv_spec],
        out_shape=[jax.ShapeDtypeStruct((N_TOK, n_out), F32), kv_shape, kv_shape],
        compiler_params=_cparams(("arbitrary",)),
    )(*args)


MLA_Q_COLS = 2 * MLA_HEADS * MLA_NOPE


def _mla_proj_kernel(x_ref, g_ref, sh_ref, sc_ref, wa_ref, qn_ref, kvn_ref, wb_ref, wbs_ref,
                     cos_ref, sin_ref, q_ref, ckv_ref, kpe_ref):
    h = _rms(x_ref[...], g_ref[...]) * (1.0 + sc_ref[0]) + sh_ref[0]
    a = _dot(h.astype(BF16), wa_ref[...])
    c0 = MLA_Q_LORA
    c1 = c0 + MLA_KV_LORA
    cos = cos_ref[...]
    sin = sin_ref[...]
    ckv_ref[...] = _rms(a[:, c0:c1], kvn_ref[...])
    kpe_ref[...] = a[:, c1:c1 + LANES] * cos + a[:, c1 + LANES:c1 + 2 * LANES] * sin
    qn = _rms(a[:, :c0], qn_ref[...]).astype(BF16)
    half = MLA_Q_COLS // 2
    q_scale = MLA_SCALE
    q_ref[:, :half] = (_dot(qn, wb_ref[:, :half]) * q_scale).astype(BF16)
    q_rope = _dot(qn, wb_ref[:, half:])
    q_swap = _dot(qn, wbs_ref[...])
    reps = half // LANES
    cos_r = jnp.concatenate([cos] * reps, axis=1) * q_scale
    sin_r = jnp.concatenate([sin] * reps, axis=1) * q_scale
    q_ref[:, half:] = (q_rope * cos_r + q_swap * sin_r).astype(BF16)


def _mla_proj(x, gain, shift, scale, wa, q_norm, kv_norm, wb, wbs, cos, sin):
    return pl.pallas_call(
        _mla_proj_kernel,
        grid=(N_TOK // ROW_TILE,),
        in_specs=[
            pl.BlockSpec((ROW_TILE, D_MODEL), lambda i: (i, 0)),
            pl.BlockSpec((1, D_MODEL), lambda i: (0, 0)),
            _mod_spec(ROW_TILE),
            _mod_spec(ROW_TILE),
            _resident(wa.shape),
            pl.BlockSpec((1, MLA_Q_LORA), lambda i: (0, 0)),
            pl.BlockSpec((1, MLA_KV_LORA), lambda i: (0, 0)),
            _resident(wb.shape),
            _resident(wbs.shape),
            pl.BlockSpec((ROW_TILE, LANES), lambda i: (_rope_block(i), 0)),
            pl.BlockSpec((ROW_TILE, LANES), lambda i: (_rope_block(i), 0)),
        ],
        out_specs=[
            pl.BlockSpec((ROW_TILE, MLA_Q_COLS), lambda i: (i, 0)),
            pl.BlockSpec((ROW_TILE, MLA_KV_LORA), lambda i: (i, 0)),
            pl.BlockSpec((ROW_TILE, LANES), lambda i: (i, 0)),
        ],
        out_shape=[
            jax.ShapeDtypeStruct((N_TOK, MLA_Q_COLS), BF16),
            jax.ShapeDtypeStruct((N_TOK, MLA_KV_LORA), F32),
            jax.ShapeDtypeStruct((N_TOK, LANES), F32),
        ],
        compiler_params=_cparams(("arbitrary",)),
    )(x, gain, shift, scale, wa, q_norm, kv_norm, wb, wbs, cos, sin)


def _rows_matmul_kernel(x_ref, w_ref, o_ref):
    o_ref[...] = _dot(x_ref[...].astype(BF16), w_ref[...]).astype(o_ref.dtype)


def _rows_matmul(x, w, out_dtype):
    rows, k = x.shape
    n = w.shape[1]
    return pl.pallas_call(
        _rows_matmul_kernel,
        grid=(rows // ROW_TILE,),
        in_specs=[pl.BlockSpec((ROW_TILE, k), lambda i: (i, 0)), _resident(w.shape)],
        out_specs=pl.BlockSpec((ROW_TILE, n), lambda i: (i, 0)),
        out_shape=jax.ShapeDtypeStruct((rows, n), out_dtype),
        compiler_params=_cparams(("arbitrary",)),
    )(x, w)


def _pair_attention(q, masks, keys, values, biases, q_scale, sinks):
    half = LANES // 2
    lane = lax.broadcasted_iota(jnp.int32, (1, LANES), 1)
    outs = []
    sink_terms = []
    for h in range(2):
        qh = jnp.where(masks[h], q, jnp.zeros_like(q))
        if q_scale != 1.0:
            qh = qh * q_scale
        qh = qh.astype(BF16)
        scores = []
        for k, bias in zip(keys, biases):
            s = _nt_dot(qh, k)
            if bias is not None:
                s = s + bias(h)
            scores.append(s)
        m = functools.reduce(jnp.maximum, [jnp.max(s, axis=-1, keepdims=True) for s in scores])
        if sinks is not None:
            m = jnp.maximum(m, sinks[h])
        own = (lane < half) if h == 0 else (lane >= half)
        acc = None
        for s, v in zip(scores, values):
            part = _dot(jnp.exp2(s - m).astype(BF16), jnp.where(own, v, jnp.ones_like(v)))
            acc = part if acc is None else acc + part
        outs.append(acc)
        if sinks is not None:
            sink_terms.append(jnp.exp2(sinks[h] - m))
    first = lane < half
    numer = jnp.where(first, outs[0], outs[1])
    denom = pltpu.roll(jnp.where(first, outs[1], outs[0]), half, axis=1)
    if sinks is not None:
        denom = denom + jnp.where(first, sink_terms[0], sink_terms[1])
    return numer / denom


def _half_masks(width=LANES):
    lane = lax.broadcasted_iota(jnp.int32, (1, width), 1)
    return [lane < LANES // 2, lane >= LANES // 2]


def _ctx_attn_kernel(sink_ref, q_ref, k_ref, v_ref, o_ref, *, kv_blocks, use_sink):
    masks = _half_masks()
    n_q_blocks = o_ref.shape[1] // LANES
    for qb in range(n_q_blocks):
        kb = qb * kv_blocks // n_q_blocks
        cols = slice(kb * LANES, (kb + 1) * LANES)
        sinks = None
        if use_sink:
            base = (qb // 4) * 8 + qb % 4
            sinks = [sink_ref[base] * LOG2E, sink_ref[base + 4] * LOG2E]
        o_ref[:, qb * LANES:(qb + 1) * LANES] = _pair_attention(
            q_ref[:, qb * LANES:(qb + 1) * LANES], masks,
            [k_ref[:, cols].astype(BF16)], [v_ref[:, cols].astype(BF16)], [None],
            HEAD_SCALE, sinks)


def _ctx_attn(qkv, sink, kv_width, use_sink):
    qw = NA_HEADS * NA_HEAD_DIM
    k_blk = qw // kv_width
    return pl.pallas_call(
        functools.partial(_ctx_attn_kernel, kv_blocks=kv_width // LANES, use_sink=use_sink),
        grid_spec=pltpu.PrefetchScalarGridSpec(
            num_scalar_prefetch=1,
            grid=(BATCH,),
            in_specs=[
                pl.BlockSpec((SEQ, qw), lambda b, s: (b, 0)),
                pl.BlockSpec((SEQ, kv_width), lambda b, s: (b, k_blk)),
                pl.BlockSpec((SEQ, kv_width), lambda b, s: (b, k_blk + 1)),
            ],
            out_specs=pl.BlockSpec((SEQ, qw), lambda b, s: (b, 0)),
        ),
        out_shape=jax.ShapeDtypeStruct((N_PROMPT, qw), F32),
        compiler_params=_cparams(("arbitrary",)),
    )(sink, qkv, qkv, qkv)


def _na_union_start(rb):
    return jnp.clip(rb * NA_QROWS - NA_WIN_ROWS // 2, 0, GRID_ROWS - NA_UNION)


NA_DR_SLOTS = 2 * NA_WIN_ROWS


def _na_bias(pairs_ref, h, rb):
    u0 = _na_union_start(rb)
    left = lax.broadcasted_iota(jnp.int32, (1, LANES), 1) < GRID_W
    rows = []
    for j in range(NA_QROWS):
        r = rb * NA_QROWS + j
        start = jnp.clip(r - NA_WIN_ROWS // 2, 0, GRID_ROWS - NA_WIN_ROWS)
        blocks = []
        for ip in range(NA_UNION // 2):
            key_row = u0 + 2 * ip
            dr = key_row - r + NA_WIN_ROWS - 1
            ok_l = ((key_row >= start) & (key_row < start + NA_WIN_ROWS)).astype(jnp.int32)
            ok_r = ((key_row + 1 >= start) & (key_row + 1 < start + NA_WIN_ROWS)).astype(jnp.int32)
            blk = pairs_ref[h, jnp.clip(dr + 1, 0, NA_DR_SLOTS - 1)]
            blocks.append(jnp.where(jnp.where(left, ok_l, ok_r) > 0, blk, NEG_INF))
        rows.append(jnp.concatenate(blocks, axis=1))
    return jnp.concatenate(rows, axis=0)


def _na_latent_kernel(q_ref, k_ref, v_ref, kc_ref, vc_ref, pairs_ref, o_ref):
    rb = pl.program_id(2)
    start = pl.multiple_of(_na_union_start(rb) * GRID_W, GRID_W)
    n_loc = NA_UNION * GRID_W
    for p in range(NA_PAIRS_PER_STEP):
        cols = slice(p * LANES, (p + 1) * LANES)
        k_loc = k_ref[pl.ds(start, n_loc), cols].astype(BF16)
        v_loc = v_ref[pl.ds(start, n_loc), cols].astype(BF16)
        o_ref[:, cols] = _pair_attention(
            q_ref[:, cols], _half_masks(),
            [k_loc, kc_ref[0, :, cols].astype(BF16)], [v_loc, vc_ref[0, :, cols].astype(BF16)],
            [lambda h, p=p: _na_bias(pairs_ref, 2 * p + h, rb), None], HEAD_SCALE, None)


def _na_latent_attn(qkv, k_ctx, v_ctx, pairs):
    qw = NA_HEADS * NA_HEAD_DIM
    width = NA_PAIRS_PER_STEP * LANES
    n_groups = qw // width
    q_rows = NA_QROWS * GRID_W
    n_rb = DEC_SEQ // q_rows
    q_off = N_PROMPT // q_rows
    kv_off = N_PROMPT // DEC_SEQ
    return pl.pallas_call(
        _na_latent_kernel,
        grid=(DEC_BATCH, n_groups, n_rb),
        in_specs=[
            pl.BlockSpec((q_rows, width), lambda b, j, r: (q_off + b * n_rb + r, j)),
            pl.BlockSpec((DEC_SEQ, width), lambda b, j, r: (kv_off + b, n_groups + j)),
            pl.BlockSpec((DEC_SEQ, width), lambda b, j, r: (kv_off + b, 2 * n_groups + j)),
            pl.BlockSpec((1, PAST_LEN, width), lambda b, j, r: (b, 0, j)),
            pl.BlockSpec((1, PAST_LEN, width), lambda b, j, r: (b, 0, j)),
            pl.BlockSpec((2 * NA_PAIRS_PER_STEP, NA_DR_SLOTS, GRID_W, LANES), lambda b, j, r: (j, 0, 0, 0)),
        ],
        out_specs=pl.BlockSpec((q_rows, width), lambda b, j, r: (b * n_rb + r, j)),
        out_shape=jax.ShapeDtypeStruct((N_SAMPLE, qw), F32),
        compiler_params=_cparams(("arbitrary", "arbitrary", "arbitrary")),
    )(qkv, qkv, qkv, k_ctx, v_ctx, pairs)


def _na_pair_table(rel_bias):
    cols = np.arange(GRID_W)
    col_start = np.clip(cols - NA_WIN_COLS // 2, 0, GRID_W - NA_WIN_COLS)
    col_ok = (cols[None, :] >= col_start[:, None]) & (cols[None, :] < col_start[:, None] + NA_WIN_COLS)
    dc = np.clip(cols[None, :] - cols[:, None] + NA_WIN_COLS - 1, 0, 2 * NA_WIN_COLS - 2)
    n_dc = 2 * NA_WIN_COLS - 1
    n_dr = 2 * NA_WIN_ROWS - 1
    pick = np.zeros((n_dc, GRID_W * GRID_W), np.float32)
    pick[dc.reshape(-1), np.arange(GRID_W * GRID_W)] = 1.0
    flat = jnp.dot(rel_bias.astype(F32).reshape(NA_HEADS * n_dr, n_dc), jnp.asarray(pick),
                   precision=lax.Precision.HIGHEST)
    bias = jnp.where(col_ok.reshape(1, 1, -1), flat.reshape(NA_HEADS, n_dr, -1) * LOG2E, NEG_INF)
    bias = bias.reshape(NA_HEADS, n_dr, GRID_W, GRID_W)
    masked = jnp.full((NA_HEADS, 1, GRID_W, GRID_W), NEG_INF, F32)
    ext = jnp.concatenate([masked, bias, masked], axis=1)
    return jnp.concatenate([ext[:, :-1], ext[:, 1:]], axis=-1)


GQA_Q_BLOCK = GQA_WINDOW
GQA_GROUP = GQA_HEADS // GQA_KV_HEADS


def _gqa_latent_kernel(sink_ref, q_ref, kp_ref, kc_ref, kn_ref, vp_ref, vc_ref, vn_ref,
                       kx_ref, vx_ref, o_ref):
    pair = pl.program_id(1)
    qb = pl.program_id(2)
    n_qb = pl.num_programs(2)
    blk = GQA_Q_BLOCK
    rows = GQA_GROUP * blk
    qq = lax.broadcasted_iota(jnp.int32, (rows, blk), 0) % blk
    kk = lax.broadcasted_iota(jnp.int32, (rows, blk), 1)
    neg = jnp.float32(NEG_INF)
    bias_prev = jnp.where((kk >= qq) & (qb > 0), 0.0, neg)
    bias_next = jnp.where((kk <= qq) & (qb < n_qb - 1), 0.0, neg)
    row_grp = lax.broadcasted_iota(jnp.int32, (rows, 1), 0) // blk
    q = jnp.concatenate([q_ref[:, m * LANES:(m + 1) * LANES] for m in range(GQA_GROUP)], axis=0)
    sinks = []
    for h in range(2):
        base = pair * 2 * GQA_GROUP + h * GQA_GROUP
        col = jnp.zeros((rows, 1), F32)
        for m in range(GQA_GROUP):
            col = jnp.where(row_grp == m, sink_ref[base + m] * LOG2E, col)
        sinks.append(col)
    o = _pair_attention(
        q, _half_masks(),
        [kp_ref[...].astype(BF16), kc_ref[...].astype(BF16), kn_ref[...].astype(BF16), kx_ref[0].astype(BF16)],
        [vp_ref[...].astype(BF16), vc_ref[...].astype(BF16), vn_ref[...].astype(BF16), vx_ref[0].astype(BF16)],
        [lambda h: bias_prev, None, lambda h: bias_next, None], HEAD_SCALE, sinks)
    for m in range(GQA_GROUP):
        o_ref[:, m * LANES:(m + 1) * LANES] = o[m * blk:(m + 1) * blk]


def _gqa_latent_attn(qkv, sink, k_ctx, v_ctx):
    qw = GQA_HEADS * GQA_HEAD_DIM
    blk = GQA_Q_BLOCK
    n_qb = DEC_SEQ // blk
    q_off = N_PROMPT // blk
    q_cols = GQA_GROUP * LANES
    n_pairs = GQA_KV_HEADS // 2
    k_col = qw // LANES
    v_col = k_col + n_pairs

    def kv_spec(col0, shift):
        def imap(b, p, i, s):
            return (q_off + b * n_qb + jnp.clip(i + shift, 0, n_qb - 1), col0 + p)
        return pl.BlockSpec((blk, LANES), imap)

    ctx_spec = pl.BlockSpec((1, PAST_LEN, LANES), lambda b, p, i, s: (b, 0, p))
    return pl.pallas_call(
        _gqa_latent_kernel,
        grid_spec=pltpu.PrefetchScalarGridSpec(
            num_scalar_prefetch=1,
            grid=(DEC_BATCH, n_pairs, n_qb),
            in_specs=[
                pl.BlockSpec((blk, q_cols), lambda b, p, i, s: (q_off + b * n_qb + i, p)),
                kv_spec(k_col, -1), kv_spec(k_col, 0), kv_spec(k_col, 1),
                kv_spec(v_col, -1), kv_spec(v_col, 0), kv_spec(v_col, 1),
                ctx_spec, ctx_spec,
            ],
            out_specs=pl.BlockSpec((blk, q_cols), lambda b, p, i, s: (b * n_qb + i, p)),
        ),
        out_shape=jax.ShapeDtypeStruct((N_SAMPLE, qw), F32),
        compiler_params=_cparams(("arbitrary", "arbitrary", "arbitrary")),
    )(sink, qkv, qkv, qkv, qkv, qkv, qkv, qkv, k_ctx, v_ctx)


MLA_PAIRS = MLA_HEADS // 2


def _mla_masks():
    lane = lax.broadcasted_iota(jnp.int32, (1, 2 * LANES), 1)
    half = LANES // 2
    even = (lane < half) | ((lane >= LANES) & (lane < LANES + MLA_ROPE))
    odd = ((lane >= half) & (lane < LANES)) | ((lane >= LANES + MLA_ROPE) & (lane < LANES + 2 * MLA_ROPE))
    return [even, odd]


def _mla_ctx_kernel(q_ref, kv_ref, kpe_ref, o_ref):
    masks = _mla_masks()
    kpe = kpe_ref[...].astype(BF16)
    half = MLA_Q_COLS // 2
    for j in range(MLA_PAIRS):
        cols = slice(j * LANES, (j + 1) * LANES)
        rcols = slice(half + j * LANES, half + (j + 1) * LANES)
        q = jnp.concatenate([q_ref[:, cols], q_ref[:, rcols]], axis=1)
        k = jnp.concatenate([kv_ref[:, cols], kpe], axis=1)
        o_ref[:, cols] = _pair_attention(q, masks, [k], [kv_ref[:, rcols]], [None], 1.0, None)


def _mla_ctx_attn(q, kv, kpe):
    ow = MLA_HEADS * MLA_V
    return pl.pallas_call(
        _mla_ctx_kernel,
        grid=(BATCH,),
        in_specs=[
            pl.BlockSpec((SEQ, MLA_Q_COLS), lambda b: (b, 0)),
            pl.BlockSpec((SEQ, MLA_Q_COLS), lambda b: (b, 0)),
            pl.BlockSpec((SEQ, LANES), lambda b: (b, 0)),
        ],
        out_specs=pl.BlockSpec((SEQ, ow), lambda b: (b, 0)),
        out_shape=jax.ShapeDtypeStruct((N_PROMPT, ow), F32),
        compiler_params=_cparams(("arbitrary",)),
    )(q, kv, kpe)


MLA_Q_BLOCK = 256


def _mla_latent_kernel(qn_ref, qr_ref, ka_ref, kpe_ref, v_ref, o_ref, kcat_ref):
    @pl.when(pl.program_id(2) == 0)
    def _():
        kcat_ref[:, :LANES] = ka_ref[...]
        kcat_ref[:, LANES:] = kpe_ref[...].astype(BF16)

    q = jnp.concatenate([qn_ref[...], qr_ref[...]], axis=1)
    o_ref[...] = _pair_attention(q, _mla_masks(), [kcat_ref[...]], [v_ref[...]], [None], 1.0, None)


def _mla_latent_attn(q, kv, kpe):
    ow = MLA_HEADS * MLA_V
    n_qb = DEC_SEQ // MLA_Q_BLOCK
    q_off = N_PROMPT // MLA_Q_BLOCK
    return pl.pallas_call(
        _mla_latent_kernel,
        grid=(DEC_BATCH, MLA_PAIRS, n_qb),
        in_specs=[
            pl.BlockSpec((MLA_Q_BLOCK, LANES), lambda b, j, i: (q_off + b * n_qb + i, j)),
            pl.BlockSpec((MLA_Q_BLOCK, LANES), lambda b, j, i: (q_off + b * n_qb + i, MLA_PAIRS + j)),
            pl.BlockSpec((MLA_KEYS, LANES), lambda b, j, i: (b, j)),
            pl.BlockSpec((MLA_KEYS, LANES), lambda b, j, i: (b, 0)),
            pl.BlockSpec((MLA_KEYS, LANES), lambda b, j, i: (b, MLA_PAIRS + j)),
        ],
        out_specs=pl.BlockSpec((MLA_Q_BLOCK, LANES), lambda b, j, i: (b * n_qb + i, j)),
        out_shape=jax.ShapeDtypeStruct((N_SAMPLE, ow), F32),
        scratch_shapes=[pltpu.VMEM((MLA_KEYS, 2 * LANES), BF16)],
        compiler_params=_cparams(("arbitrary", "arbitrary", "arbitrary")),
    )(q, q, kv, kpe, kv)


def _out_proj_kernel(x_ref, o_ref, gate_ref, w_ref, y_ref):
    y_ref[...] = x_ref[...] + gate_ref[0] * _dot(o_ref[...].astype(BF16), w_ref[...])


def _out_proj(x, o_prompt, o_sample, gate, w):
    n_p = N_PROMPT // ROW_TILE
    k = w.shape[0]
    last_p = n_p - 1
    return pl.pallas_call(
        _out_proj_kernel_two,
        grid=(N_TOK // ROW_TILE,),
        in_specs=[
            pl.BlockSpec((ROW_TILE, D_MODEL), lambda i: (i, 0)),
            pl.BlockSpec((ROW_TILE, k), lambda i: (jnp.minimum(i, last_p), 0)),
            pl.BlockSpec((ROW_TILE, k), lambda i: (jnp.maximum(i - n_p, 0), 0)),
            _mod_spec(ROW_TILE),
            _resident(w.shape),
        ],
        out_specs=pl.BlockSpec((ROW_TILE, D_MODEL), lambda i: (i, 0)),
        out_shape=jax.ShapeDtypeStruct((N_TOK, D_MODEL), F32),
        compiler_params=_cparams(("arbitrary",)),
    )(x, o_prompt, o_sample, gate, w)


def _out_proj_kernel_two(x_ref, op_ref, os_ref, gate_ref, w_ref, y_ref):
    is_prompt = pl.program_id(0) < N_PROMPT // ROW_TILE

    @pl.when(is_prompt)
    def _():
        _out_proj_kernel(x_ref, op_ref, gate_ref, w_ref, y_ref)

    @pl.when(jnp.logical_not(is_prompt))
    def _():
        _out_proj_kernel(x_ref, os_ref, gate_ref, w_ref, y_ref)


def _ffn_kernel(x_ref, g_ref, sh_ref, sc_ref, gate_ref, wg_ref, wu_ref, wd_ref, y_ref, acc_ref):
    x = x_ref[...]
    hb = (_rms(x, g_ref[...]) * (1.0 + sc_ref[0]) + sh_ref[0]).astype(BF16)
    acc_ref[...] = jnp.zeros_like(acc_ref)

    def chunk(c, carry):
        a = _silu(_dot(hb, wg_ref[c])) * _dot(hb, wu_ref[c])
        acc_ref[...] += _dot(a.astype(BF16), wd_ref[c])
        return carry

    lax.fori_loop(0, N_FFN_CHUNKS, chunk, 0)
    y_ref[...] = x + gate_ref[0] * acc_ref[...]


def _ffn(x, gain, shift, scale, gate, wg, wu, wd):
    return pl.pallas_call(
        _ffn_kernel,
        grid=(N_TOK // ROW_TILE,),
        in_specs=[
            pl.BlockSpec((ROW_TILE, D_MODEL), lambda i: (i, 0)),
            pl.BlockSpec((1, D_MODEL), lambda i: (0, 0)),
            _mod_spec(ROW_TILE), _mod_spec(ROW_TILE), _mod_spec(ROW_TILE),
            _resident(wg.shape), _resident(wu.shape), _resident(wd.shape),
        ],
        out_specs=pl.BlockSpec((ROW_TILE, D_MODEL), lambda i: (i, 0)),
        out_shape=jax.ShapeDtypeStruct((N_TOK, D_MODEL), F32),
        scratch_shapes=[pltpu.VMEM((ROW_TILE, D_MODEL), F32)],
        compiler_params=_cparams(("arbitrary",)),
    )(x, gain, shift, scale, gate, wg, wu, wd)


SLAB = D_MODEL // LANES


def _store_slabs(ref, x):
    n = x.shape[0]
    for s in range(SLAB):
        ref[pl.ds(s, n, stride=SLAB), :] = x[:, s * LANES:(s + 1) * LANES]


def _load_slabs(ref, n, dtype, row0=0):
    return jnp.concatenate([ref[pl.ds(row0 + s, n, stride=SLAB), :].astype(dtype) for s in range(SLAB)], axis=1)


META_E1, META_E2, META_W1, META_W2, META_R1, META_R2 = range(6)


def _router_kernel(x_ref, g_ref, sh_ref, sc_ref, whi_ref, wlo_ref, tri_ref, h_ref, meta_ref, cnt_ref, carry_ref):
    @pl.when(pl.program_id(0) == 0)
    def _():
        carry_ref[...] = jnp.zeros_like(carry_ref)

    h = _rms(x_ref[...], g_ref[...]) * (1.0 + sc_ref[0]) + sh_ref[0]
    _store_slabs(h_ref, h)
    h_hi, h_lo = _split_bf16(h)
    logits = _dot(h_hi, whi_ref[...]) + _dot(h_lo, whi_ref[...]) + _dot(h_hi, wlo_ref[...])
    lane = lax.broadcasted_iota(jnp.int32, logits.shape, 1)
    lane_f = lane.astype(F32)
    ninf = jnp.float32(-jnp.inf)
    lg = jnp.where(lane < N_EXPERTS, logits, ninf)
    m1 = jnp.max(lg, axis=-1, keepdims=True)
    i1 = jnp.min(jnp.where(lg == m1, lane_f, float(LANES)), axis=-1, keepdims=True)
    lg2 = jnp.where(lane_f == i1, ninf, lg)
    m2 = jnp.max(lg2, axis=-1, keepdims=True)
    i2 = jnp.min(jnp.where(lg2 == m2, lane_f, float(LANES)), axis=-1, keepdims=True)
    e = jnp.exp(m2 - m1)
    w1 = 1.0 / (1.0 + e)
    w2 = e / (1.0 + e)
    sel1 = lane_f == i1
    sel2 = lane_f == i2
    onehot = jnp.where(sel1 | sel2, 1.0, 0.0)
    ranks = _dot(tri_ref[...], onehot.astype(BF16)) + carry_ref[...]
    r1 = jnp.sum(jnp.where(sel1, ranks, 0.0), axis=-1, keepdims=True)
    r2 = jnp.sum(jnp.where(sel2, ranks, 0.0), axis=-1, keepdims=True)
    carry_ref[...] += jnp.sum(onehot, axis=0, keepdims=True)
    cnt_ref[...] = carry_ref[...]
    meta = jnp.zeros(logits.shape, F32)
    for idx, val in ((META_E1, i1), (META_E2, i2), (META_W1, w1), (META_W2, w2), (META_R1, r1), (META_R2, r2)):
        meta = jnp.where(lane == idx, val, meta)
    meta_ref[...] = meta


def _router(x, gain, shift, scale, w_hi, w_lo, tri):
    return pl.pallas_call(
        _router_kernel,
        grid=(N_TOK // ROW_TILE,),
        in_specs=[
            pl.BlockSpec((ROW_TILE, D_MODEL), lambda i: (i, 0)),
            pl.BlockSpec((1, D_MODEL), lambda i: (0, 0)),
            _mod_spec(ROW_TILE), _mod_spec(ROW_TILE),
            _resident(w_hi.shape), _resident(w_lo.shape), _resident(tri.shape),
        ],
        out_specs=[
            pl.BlockSpec((ROW_TILE * SLAB, LANES), lambda i: (i, 0)),
            pl.BlockSpec((ROW_TILE, LANES), lambda i: (i, 0)),
            pl.BlockSpec((1, LANES), lambda i: (0, 0)),
        ],
        out_shape=[
            jax.ShapeDtypeStruct((N_TOK * SLAB, LANES), F32),
            jax.ShapeDtypeStruct((N_TOK, LANES), F32),
            jax.ShapeDtypeStruct((1, LANES), F32),
        ],
        scratch_shapes=[pltpu.VMEM((1, LANES), F32)],
        compiler_params=_cparams(("arbitrary",)),
    )(x, gain, shift, scale, w_hi, w_lo, tri)


def _route_plan(meta, counts):
    cnt = counts[0, :N_EXPERTS].astype(jnp.int32)
    tiles = (cnt + EXP_TILE - 1) // EXP_TILE
    tile_end = jnp.cumsum(tiles)
    tile_start = tile_end - tiles
    offs = tile_start * EXP_TILE
    experts = jnp.arange(N_EXPERTS, dtype=jnp.int32)

    def lookup(table, idx):
        return jnp.sum(jnp.where(idx[:, None] == experts[None, :], table[None, :], 0), axis=1)

    e1 = meta[:, META_E1].astype(jnp.int32)
    e2 = meta[:, META_E2].astype(jnp.int32)
    pos1 = lookup(offs, e1) + meta[:, META_R1].astype(jnp.int32)
    pos2 = lookup(offs, e2) + meta[:, META_R2].astype(jnp.int32)
    t = jnp.arange(N_EXP_TILES, dtype=jnp.int32)
    tile_expert = jnp.minimum(jnp.sum(t[:, None] >= tile_end[None, :], axis=1), N_EXPERTS - 1).astype(jnp.int32)
    in_group = (t - lookup(tile_start, tile_expert)) * EXP_TILE
    tile_rows = jnp.where(t < tile_end[-1], jnp.clip(lookup(cnt, tile_expert) - in_group, 0, EXP_TILE),
                          0).astype(jnp.int32)
    return pos1, pos2, tile_expert, tile_rows


def _dispatch_kernel(pos1_ref, pos2_ref, h_ref, init_ref, xs_ref, sem):
    del init_ref
    base = pl.program_id(0) * DISPATCH_TILE

    def row_copy(t, dst):
        return pltpu.make_async_copy(
            h_ref.at[pl.ds(pl.multiple_of(t * SLAB, SLAB), SLAB)],
            xs_ref.at[pl.ds(pl.multiple_of(dst * SLAB, SLAB), SLAB)], sem)

    def issue(t, carry):
        row_copy(t, pos1_ref[base + t]).start()
        row_copy(t, pos2_ref[base + t]).start()
        return carry

    lax.fori_loop(0, DISPATCH_TILE, issue, 0)

    def drain(t, carry):
        row_copy(0, 0).wait()
        row_copy(0, 0).wait()
        return carry

    lax.fori_loop(0, DISPATCH_TILE, drain, 0)


def _dispatch(pos1, pos2, h_slabs, xs_init):
    return pl.pallas_call(
        _dispatch_kernel,
        grid_spec=pltpu.PrefetchScalarGridSpec(
            num_scalar_prefetch=2,
            grid=(N_TOK // DISPATCH_TILE,),
            in_specs=[pl.BlockSpec((DISPATCH_TILE * SLAB, LANES), lambda i, p1, p2: (i, 0)),
                      pl.BlockSpec(memory_space=pl.ANY)],
            out_specs=pl.BlockSpec(memory_space=pl.ANY),
            scratch_shapes=[pltpu.SemaphoreType.DMA(())],
        ),
        out_shape=jax.ShapeDtypeStruct(xs_init.shape, F32),
        input_output_aliases={3: 0},
        compiler_params=_cparams(("arbitrary",)),
    )(pos1, pos2, h_slabs, xs_init)


def _expert_kernel(te_ref, tr_ref, xs_ref, wg_ref, wu_ref, wd_ref, ys_ref, xb_ref, acc_ref):
    t = pl.program_id(0)
    c = pl.program_id(1)

    @pl.when(c == 0)
    def _():
        xb_ref[...] = _load_slabs(xs_ref, EXP_TILE, BF16)
        acc_ref[...] = jnp.zeros_like(acc_ref)

    rows = tr_ref[t]

    def swiglu_rows(n):
        xb = xb_ref[:n]
        a = _silu(_dot(xb, wg_ref[...].astype(BF16))) * _dot(xb, wu_ref[...].astype(BF16))
        acc_ref[:n] += _dot(a.astype(BF16), wd_ref[...].astype(BF16))

    @pl.when(rows > EXP_PART)
    def _():
        swiglu_rows(EXP_TILE)

    @pl.when((rows > 0) & (rows <= EXP_PART))
    def _():
        swiglu_rows(EXP_PART)

    @pl.when(c == N_EXP_CHUNKS - 1)
    def _():
        _store_slabs(ys_ref, acc_ref[...])


def _experts(layer, tile_expert, tile_rows, xs, wg, wu, wd):
    def chunk_of(t, c, tr):
        return jnp.where(tr[t] > 0, c, N_EXP_CHUNKS - 1)

    return pl.pallas_call(
        _expert_kernel,
        grid_spec=pltpu.PrefetchScalarGridSpec(
            num_scalar_prefetch=2,
            grid=(N_EXP_TILES, N_EXP_CHUNKS),
            in_specs=[
                pl.BlockSpec((EXP_TILE * SLAB, LANES), lambda t, c, te, tr: (t, 0)),
                pl.BlockSpec((None, None, D_MODEL, EXP_CHUNK),
                             lambda t, c, te, tr: (layer, te[t], 0, chunk_of(t, c, tr))),
                pl.BlockSpec((None, None, D_MODEL, EXP_CHUNK),
                             lambda t, c, te, tr: (layer, te[t], 0, chunk_of(t, c, tr))),
                pl.BlockSpec((None, None, EXP_CHUNK, D_MODEL),
                             lambda t, c, te, tr: (layer, te[t], chunk_of(t, c, tr), 0)),
            ],
            out_specs=pl.BlockSpec((EXP_TILE * SLAB, LANES), lambda t, c, te, tr: (t, 0)),
            scratch_shapes=[pltpu.VMEM((EXP_TILE, D_MODEL), BF16), pltpu.VMEM((EXP_TILE, D_MODEL), F32)],
        ),
        out_shape=jax.ShapeDtypeStruct(xs.shape, F32),
        compiler_params=_cparams(("arbitrary", "arbitrary")),
    )(tile_expert, tile_rows, xs, wg, wu, wd)


def _combine_kernel(pos1_ref, pos2_ref, x_ref, gate_ref, meta_ref, ys_ref, y_ref, buf_ref, sem):
    base = pl.program_id(0) * COMBINE_TILE
    n = COMBINE_TILE

    def row_copy(src, slot):
        return pltpu.make_async_copy(
            ys_ref.at[pl.ds(pl.multiple_of(src * SLAB, SLAB), SLAB)],
            buf_ref.at[pl.ds(pl.multiple_of(slot * SLAB, SLAB), SLAB)], sem)

    def issue(t, carry):
        row_copy(pos1_ref[base + t], t).start()
        row_copy(pos2_ref[base + t], n + t).start()
        return carry

    lax.fori_loop(0, n, issue, 0)

    def drain(t, carry):
        row_copy(0, 0).wait()
        row_copy(0, 0).wait()
        return carry

    lax.fori_loop(0, n, drain, 0)

    meta = meta_ref[...]
    w1 = meta[:, META_W1:META_W1 + 1]
    w2 = meta[:, META_W2:META_W2 + 1]
    y1 = _load_slabs(buf_ref, n, F32)
    y2 = _load_slabs(buf_ref, n, F32, row0=n * SLAB)
    y_ref[...] = x_ref[...] + gate_ref[0] * (w1 * y1 + w2 * y2)


def _combine(pos1, pos2, x, gate, meta, ys):
    return pl.pallas_call(
        _combine_kernel,
        grid_spec=pltpu.PrefetchScalarGridSpec(
            num_scalar_prefetch=2,
            grid=(N_TOK // COMBINE_TILE,),
            in_specs=[
                pl.BlockSpec((COMBINE_TILE, D_MODEL), lambda i, p1, p2: (i, 0)),
                pl.BlockSpec((1, 1, D_MODEL), lambda i, p1, p2: (_cond_of_row(i * COMBINE_TILE), 0, 0)),
                pl.BlockSpec((COMBINE_TILE, LANES), lambda i, p1, p2: (i, 0)),
                pl.BlockSpec(memory_space=pl.ANY),
            ],
            out_specs=pl.BlockSpec((COMBINE_TILE, D_MODEL), lambda i, p1, p2: (i, 0)),
            scratch_shapes=[pltpu.VMEM((2 * COMBINE_TILE * SLAB, LANES), F32), pltpu.SemaphoreType.DMA(())],
        ),
        out_shape=jax.ShapeDtypeStruct((N_TOK, D_MODEL), F32),
        compiler_params=_cparams(("arbitrary",)),
    )(pos1, pos2, x, gate, meta, ys)


def _moe(layer, x, gain, shift, scale, gate, router_w, wg, wu, wd, tri):
    w = jnp.pad(router_w, ((0, 0), (0, LANES - N_EXPERTS)))
    w_hi = w.astype(BF16)
    w_lo = (w - w_hi.astype(F32)).astype(BF16)
    h_slabs, meta, counts = _router(x, gain, shift, scale, w_hi, w_lo, tri)
    pos1, pos2, tile_expert, tile_rows = _route_plan(meta, counts)
    xs = _dispatch(pos1, pos2, h_slabs, jnp.zeros((N_EXP_TILES * EXP_TILE * SLAB, LANES), F32))
    ys = _experts(layer, tile_expert, tile_rows, xs, wg, wu, wd)
    return _combine(pos1, pos2, x, gate, meta, ys)


def _final_norm_kernel(x_ref, g_ref, y_ref):
    y_ref[...] = _rms(x_ref[...], g_ref[...])


def _final_norm(x, gain, row0, rows):
    off = row0 // ROW_TILE
    return pl.pallas_call(
        _final_norm_kernel,
        grid=(rows // ROW_TILE,),
        in_specs=[
            pl.BlockSpec((ROW_TILE, D_MODEL), lambda i: (off + i, 0)),
            pl.BlockSpec((1, D_MODEL), lambda i: (0, 0)),
        ],
        out_specs=pl.BlockSpec((ROW_TILE, D_MODEL), lambda i: (i, 0)),
        out_shape=jax.ShapeDtypeStruct((rows, D_MODEL), F32),
        compiler_params=_cparams(("arbitrary",)),
    )(x, gain)


def _rope_tables(rot_dim, reps):
    n_freq = rot_dim // 4
    inv_freq = ROPE_THETA ** (-np.arange(n_freq, dtype=np.float64) / n_freq)
    t = np.arange(DEC_SEQ)
    ang_r = (t // GRID_W)[:, None] * inv_freq
    ang_c = (t % GRID_W)[:, None] * inv_freq
    cos = np.concatenate([np.cos(ang_r), np.cos(ang_r), np.cos(ang_c), np.cos(ang_c)], axis=1)
    sin = np.concatenate([-np.sin(ang_r), np.sin(ang_r), -np.sin(ang_c), np.sin(ang_c)], axis=1)
    pad = LANES - reps * rot_dim
    cos = np.concatenate([cos] * reps + [np.ones((DEC_SEQ, pad))], axis=1)
    sin = np.concatenate([sin] * reps + [np.zeros((DEC_SEQ, pad))], axis=1)
    cos = np.concatenate([cos, np.ones((ROW_TILE, LANES))], axis=0)
    sin = np.concatenate([sin, np.zeros((ROW_TILE, LANES))], axis=0)
    return jnp.asarray(cos, F32), jnp.asarray(sin, F32)


def _swap_index(rot_dim):
    q = rot_dim // 4
    idx = np.arange(rot_dim)
    return np.where((idx // q) % 2 == 0, idx + q, idx - q)


def _gqa_q_perm():
    heads = []
    for p in range(GQA_KV_HEADS // 2):
        for m in range(GQA_GROUP):
            heads += [2 * p * GQA_GROUP + m, (2 * p + 1) * GQA_GROUP + m]
    return np.concatenate([np.arange(h * GQA_HEAD_DIM, (h + 1) * GQA_HEAD_DIM) for h in heads])


def _mla_layouts(wq_b, wkv_a, wkv_b):
    hd = MLA_NOPE + MLA_ROPE
    nope_cols = np.concatenate([np.arange(h * hd, h * hd + MLA_NOPE) for h in range(MLA_HEADS)])
    swap = _swap_index(MLA_ROPE)
    zeros_q = jnp.zeros((MLA_Q_LORA, LANES - 2 * MLA_ROPE), F32)
    rope_blocks, rope_swapped = [], []
    for j in range(MLA_PAIRS):
        for blocks, idx in ((rope_blocks, np.arange(MLA_ROPE)), (rope_swapped, swap)):
            blocks += [wq_b[:, (2 * j) * hd + MLA_NOPE + idx], wq_b[:, (2 * j + 1) * hd + MLA_NOPE + idx], zeros_q]
    wb = jnp.concatenate([wq_b[:, nope_cols]] + rope_blocks, axis=1)
    wbs = jnp.concatenate(rope_swapped, axis=1)
    zeros_a = jnp.zeros((D_MODEL, LANES - 2 * MLA_ROPE), F32)
    kpe_w = wkv_a[:, MLA_KV_LORA:]
    kpe_ws = kpe_w[:, swap]
    wa_tail = jnp.concatenate([wkv_a[:, :MLA_KV_LORA], kpe_w, kpe_w, zeros_a, kpe_ws, kpe_ws, zeros_a], axis=1)
    kvd = MLA_NOPE + MLA_V
    k_cols = np.concatenate([np.arange(h * kvd, h * kvd + MLA_NOPE) for h in range(MLA_HEADS)])
    v_cols = k_cols + MLA_NOPE
    wkvb = jnp.concatenate([wkv_b[:, k_cols], wkv_b[:, v_cols]], axis=1)
    return wb, wbs, wa_tail, wkvb


def _chunk_cols(w, chunk):
    k, n = w.shape
    return w.reshape(k, n // chunk, chunk).transpose(1, 0, 2).astype(BF16)


def kernel(x_prompt, x_sample, cache_l0_k, cache_l0_v, cache_l1_ckv, cache_l1_kpe, cache_l2_k, cache_l2_v, cache_l3_k, cache_l3_v, c, c_ctx, ada_w, ada_b, norm_mix, norm_ffn, norm_final, na_w_qkv, na_w_o, na_rel_bias, mla_wq_a, mla_q_norm, mla_wq_b, mla_wkv_a, mla_kv_norm, mla_wkv_b, mla_w_o, gqa_w_qkv, gqa_w_o, gqa_sink, ffn_w_gate, ffn_w_up, ffn_w_down, moe_router, moe_w_gate, moe_w_up, moe_w_down):
    x = jnp.concatenate([x_prompt.reshape(N_PROMPT, D_MODEL), x_sample.reshape(N_SAMPLE, D_MODEL)], axis=0)
    cond = jnp.concatenate([c_ctx[None, :], c, jnp.zeros((N_COND - 1 - DEC_BATCH, D_MODEL), F32)], axis=0)
    mods = _adaln(cond, ada_w, ada_b).reshape(DEPTH, N_COND, 6, 1, D_MODEL)
    tri = jnp.asarray(np.tril(np.ones((ROW_TILE, ROW_TILE), np.float32), -1), BF16)
    no_sink = jnp.zeros((GQA_HEADS,), F32)
    na_caches = {0: (cache_l0_k, cache_l0_v), 3: (cache_l3_k, cache_l3_v)}
    state = []

    for i in range(DEPTH):
        sh_m, sc_m, g_m, sh_f, sc_f, g_f = [mods[i, :, j] for j in range(6)]
        gain_m = norm_mix[i][None, :]
        gain_f = norm_ffn[i][None, :]
        kind, s = i % 3, i // 3
        if kind == 0:
            qw = NA_HEADS * NA_HEAD_DIM
            qkv, k_rows, v_rows = _norm_proj(x, gain_m, sh_m, sc_m, na_w_qkv[s].astype(BF16), qw)
            o_p = _ctx_attn(qkv, no_sink, qw, False)
            ck, cv = na_caches[i]
            o_s = _na_latent_attn(qkv, ck.reshape(DEC_BATCH, PAST_LEN, qw), cv.reshape(DEC_BATCH, PAST_LEN, qw),
                                  _na_pair_table(na_rel_bias[s]))
            w_o = na_w_o[s].astype(BF16)
            kv_shape = (BATCH, SEQ, NA_HEADS, NA_HEAD_DIM)
            state += [k_rows.reshape(kv_shape), v_rows.reshape(kv_shape)]
        elif kind == 1:
            wb, wbs, wa_tail, wkvb = _mla_layouts(mla_wq_b[s], mla_wkv_a[s], mla_wkv_b[s])
            wa = jnp.concatenate([mla_wq_a[s], wa_tail], axis=1).astype(BF16)
            cos, sin = _rope_tables(MLA_ROPE, 2)
            q, ckv, kpe = _mla_proj(x, gain_m, sh_m, sc_m, wa, mla_q_norm[s][None, :], mla_kv_norm[s][None, :],
                                    wb.astype(BF16), wbs.astype(BF16), cos, sin)
            wkvb = wkvb.astype(BF16)
            kv_p = _rows_matmul(ckv[:N_PROMPT], wkvb, BF16)
            o_p = _mla_ctx_attn(q, kv_p, kpe)
            ckv_s = ckv[N_PROMPT:].reshape(DEC_BATCH, DEC_SEQ, MLA_KV_LORA)
            ckv_all = jnp.concatenate([cache_l1_ckv, ckv_s], axis=1).reshape(DEC_BATCH * MLA_KEYS, MLA_KV_LORA)
            kv_s = _rows_matmul(ckv_all, wkvb, BF16)
            kpe_cache = jnp.concatenate(
                [cache_l1_kpe, cache_l1_kpe, jnp.zeros((DEC_BATCH, PAST_LEN, LANES - 2 * MLA_ROPE), F32)], axis=2)
            kpe_all = jnp.concatenate([kpe_cache, kpe[N_PROMPT:].reshape(DEC_BATCH, DEC_SEQ, LANES)], axis=1)
            o_s = _mla_latent_attn(q, kv_s, kpe_all.reshape(DEC_BATCH * MLA_KEYS, LANES))
            w_o = mla_w_o[s].astype(BF16)
            state += [ckv[:N_PROMPT].reshape(BATCH, SEQ, MLA_KV_LORA),
                      kpe[:N_PROMPT, :MLA_ROPE].reshape(BATCH, SEQ, MLA_ROPE)]
        else:
            qw = GQA_HEADS * GQA_HEAD_DIM
            kw = GQA_KV_HEADS * GQA_HEAD_DIM
            perm = _gqa_q_perm()
            w_qkv = jnp.concatenate([gqa_w_qkv[s][:, perm], gqa_w_qkv[s][:, qw:]], axis=1)
            swap = np.concatenate([h * GQA_HEAD_DIM + _swap_index(GQA_HEAD_DIM) for h in range((qw + kw) // GQA_HEAD_DIM)])
            cos, sin = _rope_tables(GQA_HEAD_DIM, 2)
            qkv, k_rows, v_rows = _norm_proj(x, gain_m, sh_m, sc_m, w_qkv.astype(BF16), kw,
                                             rope=(w_qkv[:, swap].astype(BF16), cos, sin))
            sink = gqa_sink[s]
            o_p = _ctx_attn(qkv, sink, kw, True)
            o_s = _gqa_latent_attn(qkv, sink, cache_l2_k.reshape(DEC_BATCH, PAST_LEN, kw),
                                   cache_l2_v.reshape(DEC_BATCH, PAST_LEN, kw))
            w_o = gqa_w_o[s][perm].astype(BF16)
            kv_shape = (BATCH, SEQ, GQA_KV_HEADS, GQA_HEAD_DIM)
            state += [k_rows.reshape(kv_shape), v_rows.reshape(kv_shape)]
        x = _out_proj(x, o_p, o_s, g_m, w_o)

        if i % 2 == 0:
            s = i // 2
            x = _ffn(x, gain_f, sh_f, sc_f, g_f, _chunk_cols(ffn_w_gate[s], FFN_CHUNK),
                     _chunk_cols(ffn_w_up[s], FFN_CHUNK),
                     ffn_w_down[s].reshape(N_FFN_CHUNKS, FFN_CHUNK, D_MODEL).astype(BF16))
        else:
            s = i // 2
            x = _moe(s, x, gain_f, sh_f, sc_f, g_f, moe_router[s], moe_w_gate, moe_w_up, moe_w_down, tri)

    y_prompt = _final_norm(x, norm_final[None, :], 0, N_PROMPT).reshape(BATCH, SEQ, D_MODEL)
    y_sample = _final_norm(x, norm_final[None, :], N_PROMPT, N_SAMPLE).reshape(DEC_BATCH, DEC_SEQ, D_MODEL)
    return (y_prompt, y_sample, *state)
```

```python
import functools
import math

import numpy as np
import jax
import jax.numpy as jnp
from jax import lax
from jax.experimental import pallas as pl
from jax.experimental.pallas import tpu as pltpu

F32 = jnp.float32
BF16 = jnp.bfloat16

D_MODEL = 1024
BATCH = 16
SEQ = 256
DEPTH = 4
DEC_BATCH = 4
DEC_SEQ = 2048
PAST_LEN = 512
GRID_W = 64
GRID_ROWS = DEC_SEQ // GRID_W
NORM_EPS = 1e-6
NEG_INF = -1e30
ROPE_THETA = 10000.0
NA_HEADS = 16
NA_HEAD_DIM = 64
NA_WIN_ROWS = 8
NA_WIN_COLS = 16
MLA_HEADS = 16
MLA_Q_LORA = 512
MLA_KV_LORA = 256
MLA_NOPE = 64
MLA_ROPE = 32
MLA_V = 64
GQA_HEADS = 16
GQA_KV_HEADS = 4
GQA_HEAD_DIM = 64
GQA_WINDOW = 128
FFN_DIM = 2816
N_EXPERTS = 8
EXPERT_DIM = 3584

N_PROMPT = BATCH * SEQ
N_SAMPLE = DEC_BATCH * DEC_SEQ
N_TOK = N_PROMPT + N_SAMPLE
N_COND = 8
LOG2E = math.log2(math.e)
HEAD_SCALE = NA_HEAD_DIM ** -0.5 * LOG2E
MLA_SCALE = (MLA_NOPE + MLA_ROPE) ** -0.5 * LOG2E

LANES = 128
SUBLANES = 8
VMEM_LIMIT = 56 * 1024 * 1024

ROW_TILE = 512
FFN_CHUNK = 256
N_FFN_CHUNKS = FFN_DIM // FFN_CHUNK
EXP_TILE = 1024
EXP_PART = 512
EXP_CHUNK = 512
N_EXP_CHUNKS = EXPERT_DIM // EXP_CHUNK
N_EXP_TILES = (2 * N_TOK) // EXP_TILE + N_EXPERTS
DISPATCH_TILE = 512
COMBINE_TILE = 256
NA_QROWS = 4
NA_UNION = NA_QROWS + NA_WIN_ROWS
NA_PAIRS_PER_STEP = 4
MLA_KEYS = PAST_LEN + DEC_SEQ


def _cparams(sem):
    return pltpu.CompilerParams(dimension_semantics=sem, vmem_limit_bytes=VMEM_LIMIT)


def _cond_of_row(row):
    return jnp.where(row < N_PROMPT, 0, 1 + (row - N_PROMPT) // DEC_SEQ)


def _nt_dot(a, b):
    return lax.dot_general(a, b, (((1,), (1,)), ((), ())), preferred_element_type=F32)


def _dot(a, b):
    return jnp.dot(a, b, preferred_element_type=F32)


def _rms(x, gain):
    return x * lax.rsqrt(jnp.mean(x * x, axis=-1, keepdims=True) + NORM_EPS) * gain


def _silu(x):
    return x / (1.0 + jnp.exp(-x))


def _split_bf16(x):
    hi = x.astype(BF16)
    lo = (x - hi.astype(F32)).astype(BF16)
    return hi, lo


def _resident(shape):
    nd = len(shape)
    return pl.BlockSpec(shape, lambda *_: (0,) * nd, pipeline_mode=pl.Buffered(1))


def _mod_spec(tile):
    return pl.BlockSpec((1, 1, D_MODEL), lambda i: (_cond_of_row(i * tile), 0, 0))


ADA_TILE = 1536


def _adaln_kernel(c_ref, w_ref, b_ref, o_ref):
    a_hi, a_lo = _split_bf16(_silu(c_ref[...]))
    w_hi, w_lo = _split_bf16(w_ref[0])
    o_ref[0] = _dot(a_hi, w_hi) + _dot(a_lo, w_hi) + _dot(a_hi, w_lo) + b_ref[0]


def _adaln(cond, ada_w, ada_b):
    n = 6 * D_MODEL
    return pl.pallas_call(
        _adaln_kernel,
        grid=(DEPTH, n // ADA_TILE),
        in_specs=[
            pl.BlockSpec((N_COND, D_MODEL), lambda l, j: (0, 0)),
            pl.BlockSpec((1, D_MODEL, ADA_TILE), lambda l, j: (l, 0, j)),
            pl.BlockSpec((1, 1, ADA_TILE), lambda l, j: (l, 0, j)),
        ],
        out_specs=pl.BlockSpec((1, N_COND, ADA_TILE), lambda l, j: (l, 0, j)),
        out_shape=jax.ShapeDtypeStruct((DEPTH, N_COND, n), F32),
        compiler_params=_cparams(("arbitrary", "arbitrary")),
    )(cond, ada_w, ada_b.reshape(DEPTH, 1, n))


PROJ_COLS = 512


def _norm_proj_kernel(x_ref, g_ref, sh_ref, sc_ref, w_ref, *rest, n_out, n_rope, kv_cols):
    rest = list(rest)
    if n_rope:
        w2_ref, cos_ref, sin_ref = rest[:3]
        rest = rest[3:]
    o_ref, k_ref, v_ref = rest
    h = _rms(x_ref[...], g_ref[...]) * (1.0 + sc_ref[0]) + sh_ref[0]
    hb = h.astype(BF16)
    is_prompt = pl.program_id(0) < N_PROMPT // ROW_TILE
    cuts = sorted({0, n_out, n_rope, n_out - 2 * kv_cols, n_out - kv_cols}
                  | set(range(0, n_out, PROJ_COLS)))
    for c0, c1 in zip(cuts[:-1], cuts[1:]):
        cols = slice(c0, c1)
        y = _dot(hb, w_ref[:, cols])
        if c0 < n_rope:
            ys = _dot(hb, w2_ref[:, cols])
            reps = (c1 - c0) // LANES
            cos = jnp.concatenate([cos_ref[...]] * reps, axis=1)
            sin = jnp.concatenate([sin_ref[...]] * reps, axis=1)
            y = y * cos + ys * sin
        o_ref[:, cols] = y
        for kv_ref, start in ((k_ref, n_out - 2 * kv_cols), (v_ref, n_out - kv_cols)):
            if start <= c0 < start + kv_cols:
                @pl.when(is_prompt)
                def _(kv_ref=kv_ref, start=start, y=y, c0=c0, c1=c1):
                    kv_ref[:, c0 - start:c1 - start] = y


def _rope_block(i):
    row = i * ROW_TILE
    return jnp.where(row < N_PROMPT, DEC_SEQ // ROW_TILE, ((row - N_PROMPT) % DEC_SEQ) // ROW_TILE)


def _norm_proj(x, gain, shift, scale, w, kv_cols, rope=None):
    n_out = w.shape[1]
    last_prompt = N_PROMPT // ROW_TILE - 1
    kv_spec = pl.BlockSpec((ROW_TILE, kv_cols), lambda i: (jnp.minimum(i, last_prompt), 0))
    kv_shape = jax.ShapeDtypeStruct((N_PROMPT, kv_cols), F32)
    in_specs = [
        pl.BlockSpec((ROW_TILE, D_MODEL), lambda i: (i, 0)),
        pl.BlockSpec((1, D_MODEL), lambda i: (0, 0)),
        _mod_spec(ROW_TILE),
        _mod_spec(ROW_TILE),
        _resident(w.shape),
    ]
    args = [x, gain, shift, scale, w]
    n_rope = 0
    if rope is not None:
        w2, cos, sin = rope
        n_rope = w2.shape[1]
        in_specs += [
            _resident(w2.shape),
            pl.BlockSpec((ROW_TILE, LANES), lambda i: (_rope_block(i), 0)),
            pl.BlockSpec((ROW_TILE, LANES), lambda i: (_rope_block(i), 0)),
        ]
        args += [w2, cos, sin]
    return pl.pallas_call(
        functools.partial(_norm_proj_kernel, n_out=n_out, n_rope=n_rope, kv_cols=kv_cols),
        grid=(N_TOK // ROW_TILE,),
        in_specs=in_specs,
        out_specs=[pl.BlockSpec((ROW_TILE, n_out), lambda i: (i, 0)), kv_spec, kv_spec],
        out_shape=[jax.ShapeDtypeStruct((N_TOK, n_out), F32), kv_shape, kv_shape],
        compiler_params=_cparams(("arbitrary",)),
    )(*args)


MLA_Q_COLS = 2 * MLA_HEADS * MLA_NOPE


def _mla_proj_kernel(x_ref, g_ref, sh_ref, sc_ref, wa_ref, qn_ref, kvn_ref, wb_ref, wbs_ref,
                     cos_ref, sin_ref, q_ref, ckv_ref, kpe_ref):
    h = _rms(x_ref[...], g_ref[...]) * (1.0 + sc_ref[0]) + sh_ref[0]
    a = _dot(h.astype(BF16), wa_ref[...])
    c0 = MLA_Q_LORA
    c1 = c0 + MLA_KV_LORA
    cos = cos_ref[...]
    sin = sin_ref[...]
    ckv_ref[...] = _rms(a[:, c0:c1], kvn_ref[...])
    kpe_ref[...] = a[:, c1:c1 + LANES] * cos + a[:, c1 + LANES:c1 + 2 * LANES] * sin
    qn = _rms(a[:, :c0], qn_ref[...]).astype(BF16)
    half = MLA_Q_COLS // 2
    q_scale = MLA_SCALE
    q_ref[:, :half] = (_dot(qn, wb_ref[:, :half]) * q_scale).astype(BF16)
    q_rope = _dot(qn, wb_ref[:, half:])
    q_swap = _dot(qn, wbs_ref[...])
    reps = half // LANES
    cos_r = jnp.concatenate([cos] * reps, axis=1) * q_scale
    sin_r = jnp.concatenate([sin] * reps, axis=1) * q_scale
    q_ref[:, half:] = (q_rope * cos_r + q_swap * sin_r).astype(BF16)


def _mla_proj(x, gain, shift, scale, wa, q_norm, kv_norm, wb, wbs, cos, sin):
    return pl.pallas_call(
        _mla_proj_kernel,
        grid=(N_TOK // ROW_TILE,),
        in_specs=[
            pl.BlockSpec((ROW_TILE, D_MODEL), lambda i: (i, 0)),
            pl.BlockSpec((1, D_MODEL), lambda i: (0, 0)),
            _mod_spec(ROW_TILE),
            _mod_spec(ROW_TILE),
            _resident(wa.shape),
            pl.BlockSpec((1, MLA_Q_LORA), lambda i: (0, 0)),
            pl.BlockSpec((1, MLA_KV_LORA), lambda i: (0, 0)),
            _resident(wb.shape),
            _resident(wbs.shape),
            pl.BlockSpec((ROW_TILE, LANES), lambda i: (_rope_block(i), 0)),
            pl.BlockSpec((ROW_TILE, LANES), lambda i: (_rope_block(i), 0)),
        ],
        out_specs=[
            pl.BlockSpec((ROW_TILE, MLA_Q_COLS), lambda i: (i, 0)),
            pl.BlockSpec((ROW_TILE, MLA_KV_LORA), lambda i: (i, 0)),
            pl.BlockSpec((ROW_TILE, LANES), lambda i: (i, 0)),
        ],
        out_shape=[
            jax.ShapeDtypeStruct((N_TOK, MLA_Q_COLS), BF16),
            jax.ShapeDtypeStruct((N_TOK, MLA_KV_LORA), F32),
            jax.ShapeDtypeStruct((N_TOK, LANES), F32),
        ],
        compiler_params=_cparams(("arbitrary",)),
    )(x, gain, shift, scale, wa, q_norm, kv_norm, wb, wbs, cos, sin)


def _rows_matmul_kernel(x_ref, w_ref, o_ref):
    o_ref[...] = _dot(x_ref[...].astype(BF16), w_ref[...]).astype(o_ref.dtype)


def _rows_matmul(x, w, out_dtype):
    rows, k = x.shape
    n = w.shape[1]
    return pl.pallas_call(
        _rows_matmul_kernel,
        grid=(rows // ROW_TILE,),
        in_specs=[pl.BlockSpec((ROW_TILE, k), lambda i: (i, 0)), _resident(w.shape)],
        out_specs=pl.BlockSpec((ROW_TILE, n), lambda i: (i, 0)),
        out_shape=jax.ShapeDtypeStruct((rows, n), out_dtype),
        compiler_params=_cparams(("arbitrary",)),
    )(x, w)


def _pair_attention(q, masks, keys, values, biases, q_scale, sinks):
    half = LANES // 2
    lane = lax.broadcasted_iota(jnp.int32, (1, LANES), 1)
    outs = []
    sink_terms = []
    for h in range(2):
        qh = jnp.where(masks[h], q, jnp.zeros_like(q))
        if q_scale != 1.0:
            qh = qh * q_scale
        qh = qh.astype(BF16)
        scores = []
        for k, bias in zip(keys, biases):
            s = _nt_dot(qh, k)
            if bias is not None:
                s = s + bias(h)
            scores.append(s)
        m = functools.reduce(jnp.maximum, [jnp.max(s, axis=-1, keepdims=True) for s in scores])
        if sinks is not None:
            m = jnp.maximum(m, sinks[h])
        own = (lane < half) if h == 0 else (lane >= half)
        acc = None
        for s, v in zip(scores, values):
            part = _dot(jnp.exp2(s - m).astype(BF16), jnp.where(own, v, jnp.ones_like(v)))
            acc = part if acc is None else acc + part
        outs.append(acc)
        if sinks is not None:
            sink_terms.append(jnp.exp2(sinks[h] - m))
    first = lane < half
    numer = jnp.where(first, outs[0], outs[1])
    denom = pltpu.roll(jnp.where(first, outs[1], outs[0]), half, axis=1)
    if sinks is not None:
        denom = denom + jnp.where(first, sink_terms[0], sink_terms[1])
    return (numer / denom).astype(BF16)


def _half_masks(width=LANES):
    lane = lax.broadcasted_iota(jnp.int32, (1, width), 1)
    return [lane < LANES // 2, lane >= LANES // 2]


def _ctx_attn_kernel(sink_ref, q_ref, k_ref, v_ref, o_ref, *, kv_blocks, use_sink):
    masks = _half_masks()
    n_q_blocks = o_ref.shape[1] // LANES
    for qb in range(n_q_blocks):
        kb = qb * kv_blocks // n_q_blocks
        cols = slice(kb * LANES, (kb + 1) * LANES)
        sinks = None
        if use_sink:
            base = (qb // 4) * 8 + qb % 4
            sinks = [sink_ref[base] * LOG2E, sink_ref[base + 4] * LOG2E]
        o_ref[:, qb * LANES:(qb + 1) * LANES] = _pair_attention(
            q_ref[:, qb * LANES:(qb + 1) * LANES], masks,
            [k_ref[:, cols].astype(BF16)], [v_ref[:, cols].astype(BF16)], [None],
            HEAD_SCALE, sinks)


def _ctx_attn(qkv, sink, kv_width, use_sink):
    qw = NA_HEADS * NA_HEAD_DIM
    k_blk = qw // kv_width
    return pl.pallas_call(
        functools.partial(_ctx_attn_kernel, kv_blocks=kv_width // LANES, use_sink=use_sink),
        grid_spec=pltpu.PrefetchScalarGridSpec(
            num_scalar_prefetch=1,
            grid=(BATCH,),
            in_specs=[
                pl.BlockSpec((SEQ, qw), lambda b, s: (b, 0)),
                pl.BlockSpec((SEQ, kv_width), lambda b, s: (b, k_blk)),
                pl.BlockSpec((SEQ, kv_width), lambda b, s: (b, k_blk + 1)),
            ],
            out_specs=pl.BlockSpec((SEQ, qw), lambda b, s: (b, 0)),
        ),
        out_shape=jax.ShapeDtypeStruct((N_PROMPT, qw), BF16),
        compiler_params=_cparams(("arbitrary",)),
    )(sink, qkv, qkv, qkv)


def _na_union_start(rb):
    return jnp.clip(rb * NA_QROWS - NA_WIN_ROWS // 2, 0, GRID_ROWS - NA_UNION)


NA_DR_SLOTS = 2 * NA_WIN_ROWS


def _na_bias(pairs_ref, h, rb):
    u0 = _na_union_start(rb)
    left = lax.broadcasted_iota(jnp.int32, (1, LANES), 1) < GRID_W
    rows = []
    for j in range(NA_QROWS):
        r = rb * NA_QROWS + j
        start = jnp.clip(r - NA_WIN_ROWS // 2, 0, GRID_ROWS - NA_WIN_ROWS)
        blocks = []
        for ip in range(NA_UNION // 2):
            key_row = u0 + 2 * ip
            dr = key_row - r + NA_WIN_ROWS - 1
            ok_l = ((key_row >= start) & (key_row < start + NA_WIN_ROWS)).astype(jnp.int32)
            ok_r = ((key_row + 1 >= start) & (key_row + 1 < start + NA_WIN_ROWS)).astype(jnp.int32)
            blk = pairs_ref[h, jnp.clip(dr + 1, 0, NA_DR_SLOTS - 1)]
            blocks.append(jnp.where(jnp.where(left, ok_l, ok_r) > 0, blk, NEG_INF))
        rows.append(jnp.concatenate(blocks, axis=1))
    return jnp.concatenate(rows, axis=0)


def _na_latent_kernel(q_ref, k_ref, v_ref, kc_ref, vc_ref, pairs_ref, o_ref):
    rb = pl.program_id(2)
    start = pl.multiple_of(_na_union_start(rb) * GRID_W, GRID_W)
    n_loc = NA_UNION * GRID_W
    for p in range(NA_PAIRS_PER_STEP):
        cols = slice(p * LANES, (p + 1) * LANES)
        k_loc = k_ref[pl.ds(start, n_loc), cols].astype(BF16)
        v_loc = v_ref[pl.ds(start, n_loc), cols].astype(BF16)
        o_ref[:, cols] = _pair_attention(
            q_ref[:, cols], _half_masks(),
            [k_loc, kc_ref[0, :, cols].astype(BF16)], [v_loc, vc_ref[0, :, cols].astype(BF16)],
            [lambda h, p=p: _na_bias(pairs_ref, 2 * p + h, rb), None], HEAD_SCALE, None)


def _na_latent_attn(qkv, k_ctx, v_ctx, pairs):
    qw = NA_HEADS * NA_HEAD_DIM
    width = NA_PAIRS_PER_STEP * LANES
    n_groups = qw // width
    q_rows = NA_QROWS * GRID_W
    n_rb = DEC_SEQ // q_rows
    q_off = N_PROMPT // q_rows
    kv_off = N_PROMPT // DEC_SEQ
    return pl.pallas_call(
        _na_latent_kernel,
        grid=(DEC_BATCH, n_groups, n_rb),
        in_specs=[
            pl.BlockSpec((q_rows, width), lambda b, j, r: (q_off + b * n_rb + r, j)),
            pl.BlockSpec((DEC_SEQ, width), lambda b, j, r: (kv_off + b, n_groups + j)),
            pl.BlockSpec((DEC_SEQ, width), lambda b, j, r: (kv_off + b, 2 * n_groups + j)),
            pl.BlockSpec((1, PAST_LEN, width), lambda b, j, r: (b, 0, j)),
            pl.BlockSpec((1, PAST_LEN, width), lambda b, j, r: (b, 0, j)),
            pl.BlockSpec((2 * NA_PAIRS_PER_STEP, NA_DR_SLOTS, GRID_W, LANES), lambda b, j, r: (j, 0, 0, 0)),
        ],
        out_specs=pl.BlockSpec((q_rows, width), lambda b, j, r: (b * n_rb + r, j)),
        out_shape=jax.ShapeDtypeStruct((N_SAMPLE, qw), BF16),
        compiler_params=_cparams(("arbitrary", "arbitrary", "arbitrary")),
    )(qkv, qkv, qkv, k_ctx, v_ctx, pairs)


def _na_pair_table(rel_bias):
    cols = np.arange(GRID_W)
    col_start = np.clip(cols - NA_WIN_COLS // 2, 0, GRID_W - NA_WIN_COLS)
    col_ok = (cols[None, :] >= col_start[:, None]) & (cols[None, :] < col_start[:, None] + NA_WIN_COLS)
    dc = np.clip(cols[None, :] - cols[:, None] + NA_WIN_COLS - 1, 0, 2 * NA_WIN_COLS - 2)
    n_dc = 2 * NA_WIN_COLS - 1
    n_dr = 2 * NA_WIN_ROWS - 1
    pick = np.zeros((n_dc, GRID_W * GRID_W), np.float32)
    pick[dc.reshape(-1), np.arange(GRID_W * GRID_W)] = 1.0
    flat = jnp.dot(rel_bias.astype(F32).reshape(NA_HEADS * n_dr, n_dc), jnp.asarray(pick),
                   precision=lax.Precision.HIGHEST)
    bias = jnp.where(col_ok.reshape(1, 1, -1), flat.reshape(NA_HEADS, n_dr, -1) * LOG2E, NEG_INF)
    bias = bias.reshape(NA_HEADS, n_dr, GRID_W, GRID_W)
    masked = jnp.full((NA_HEADS, 1, GRID_W, GRID_W), NEG_INF, F32)
    ext = jnp.concatenate([masked, bias, masked], axis=1)
    return jnp.concatenate([ext[:, :-1], ext[:, 1:]], axis=-1)


GQA_Q_BLOCK = GQA_WINDOW
GQA_GROUP = GQA_HEADS // GQA_KV_HEADS


def _gqa_latent_kernel(sink_ref, q_ref, kp_ref, kc_ref, kn_ref, vp_ref, vc_ref, vn_ref,
                       kx_ref, vx_ref, o_ref):
    pair = pl.program_id(1)
    qb = pl.program_id(2)
    n_qb = pl.num_programs(2)
    blk = GQA_Q_BLOCK
    rows = GQA_GROUP * blk
    qq = lax.broadcasted_iota(jnp.int32, (rows, blk), 0) % blk
    kk = lax.broadcasted_iota(jnp.int32, (rows, blk), 1)
    neg = jnp.float32(NEG_INF)
    bias_prev = jnp.where((kk >= qq) & (qb > 0), 0.0, neg)
    bias_next = jnp.where((kk <= qq) & (qb < n_qb - 1), 0.0, neg)
    row_grp = lax.broadcasted_iota(jnp.int32, (rows, 1), 0) // blk
    q = jnp.concatenate([q_ref[:, m * LANES:(m + 1) * LANES] for m in range(GQA_GROUP)], axis=0)
    sinks = []
    for h in range(2):
        base = pair * 2 * GQA_GROUP + h * GQA_GROUP
        col = jnp.zeros((rows, 1), F32)
        for m in range(GQA_GROUP):
            col = jnp.where(row_grp == m, sink_ref[base + m] * LOG2E, col)
        sinks.append(col)
    o = _pair_attention(
        q, _half_masks(),
        [kp_ref[...].astype(BF16), kc_ref[...].astype(BF16), kn_ref[...].astype(BF16), kx_ref[0].astype(BF16)],
        [vp_ref[...].astype(BF16), vc_ref[...].astype(BF16), vn_ref[...].astype(BF16), vx_ref[0].astype(BF16)],
        [lambda h: bias_prev, None, lambda h: bias_next, None], HEAD_SCALE, sinks)
    for m in range(GQA_GROUP):
        o_ref[:, m * LANES:(m + 1) * LANES] = o[m * blk:(m + 1) * blk]


def _gqa_latent_attn(qkv, sink, k_ctx, v_ctx):
    qw = GQA_HEADS * GQA_HEAD_DIM
    blk = GQA_Q_BLOCK
    n_qb = DEC_SEQ // blk
    q_off = N_PROMPT // blk
    q_cols = GQA_GROUP * LANES
    n_pairs = GQA_KV_HEADS // 2
    k_col = qw // LANES
    v_col = k_col + n_pairs

    def kv_spec(col0, shift):
        def imap(b, p, i, s):
            return (q_off + b * n_qb + jnp.clip(i + shift, 0, n_qb - 1), col0 + p)
        return pl.BlockSpec((blk, LANES), imap)

    ctx_spec = pl.BlockSpec((1, PAST_LEN, LANES), lambda b, p, i, s: (b, 0, p))
    return pl.pallas_call(
        _gqa_latent_kernel,
        grid_spec=pltpu.PrefetchScalarGridSpec(
            num_scalar_prefetch=1,
            grid=(DEC_BATCH, n_pairs, n_qb),
            in_specs=[
                pl.BlockSpec((blk, q_cols), lambda b, p, i, s: (q_off + b * n_qb + i, p)),
                kv_spec(k_col, -1), kv_spec(k_col, 0), kv_spec(k_col, 1),
                kv_spec(v_col, -1), kv_spec(v_col, 0), kv_spec(v_col, 1),
                ctx_spec, ctx_spec,
            ],
            out_specs=pl.BlockSpec((blk, q_cols), lambda b, p, i, s: (b * n_qb + i, p)),
        ),
        out_shape=jax.ShapeDtypeStruct((N_SAMPLE, qw), BF16),
        compiler_params=_cparams(("arbitrary", "arbitrary", "arbitrary")),
    )(sink, qkv, qkv, qkv, qkv, qkv, qkv, qkv, k_ctx, v_ctx)


MLA_PAIRS = MLA_HEADS // 2


def _mla_masks():
    lane = lax.broadcasted_iota(jnp.int32, (1, 2 * LANES), 1)
    half = LANES // 2
    even = (lane < half) | ((lane >= LANES) & (lane < LANES + MLA_ROPE))
    odd = ((lane >= half) & (lane < LANES)) | ((lane >= LANES + MLA_ROPE) & (lane < LANES + 2 * MLA_ROPE))
    return [even, odd]


def _mla_ctx_kernel(q_ref, kv_ref, kpe_ref, o_ref):
    masks = _mla_masks()
    kpe = kpe_ref[...].astype(BF16)
    half = MLA_Q_COLS // 2
    for j in range(MLA_PAIRS):
        cols = slice(j * LANES, (j + 1) * LANES)
        rcols = slice(half + j * LANES, half + (j + 1) * LANES)
        q = jnp.concatenate([q_ref[:, cols], q_ref[:, rcols]], axis=1)
        k = jnp.concatenate([kv_ref[:, cols], kpe], axis=1)
        o_ref[:, cols] = _pair_attention(q, masks, [k], [kv_ref[:, rcols]], [None], 1.0, None)


def _mla_ctx_attn(q, kv, kpe):
    ow = MLA_HEADS * MLA_V
    return pl.pallas_call(
        _mla_ctx_kernel,
        grid=(BATCH,),
        in_specs=[
            pl.BlockSpec((SEQ, MLA_Q_COLS), lambda b: (b, 0)),
            pl.BlockSpec((SEQ, MLA_Q_COLS), lambda b: (b, 0)),
            pl.BlockSpec((SEQ, LANES), lambda b: (b, 0)),
        ],
        out_specs=pl.BlockSpec((SEQ, ow), lambda b: (b, 0)),
        out_shape=jax.ShapeDtypeStruct((N_PROMPT, ow), BF16),
        compiler_params=_cparams(("arbitrary",)),
    )(q, kv, kpe)


MLA_Q_BLOCK = 256


MLA_PAIRS_PER_STEP = 2


def _mla_latent_kernel(qn_ref, qr_ref, ka_ref, kpe_ref, v_ref, o_ref, kcat_ref):
    @pl.when(pl.program_id(2) == 0)
    def _():
        kpe = kpe_ref[...].astype(BF16)
        for p in range(MLA_PAIRS_PER_STEP):
            kcat_ref[p, :, :LANES] = ka_ref[:, p * LANES:(p + 1) * LANES]
            kcat_ref[p, :, LANES:] = kpe

    for p in range(MLA_PAIRS_PER_STEP):
        cols = slice(p * LANES, (p + 1) * LANES)
        q = jnp.concatenate([qn_ref[:, cols], qr_ref[:, cols]], axis=1)
        o_ref[:, cols] = _pair_attention(q, _mla_masks(), [kcat_ref[p]], [v_ref[:, cols]], [None], 1.0, None)


def _mla_latent_attn(q, kv, kpe):
    ow = MLA_HEADS * MLA_V
    n_qb = DEC_SEQ // MLA_Q_BLOCK
    q_off = N_PROMPT // MLA_Q_BLOCK
    width = MLA_PAIRS_PER_STEP * LANES
    n_groups = MLA_PAIRS // MLA_PAIRS_PER_STEP
    return pl.pallas_call(
        _mla_latent_kernel,
        grid=(DEC_BATCH, n_groups, n_qb),
        in_specs=[
            pl.BlockSpec((MLA_Q_BLOCK, width), lambda b, j, i: (q_off + b * n_qb + i, j)),
            pl.BlockSpec((MLA_Q_BLOCK, width), lambda b, j, i: (q_off + b * n_qb + i, n_groups + j)),
            pl.BlockSpec((MLA_KEYS, width), lambda b, j, i: (b, j)),
            pl.BlockSpec((MLA_KEYS, LANES), lambda b, j, i: (b, 0)),
            pl.BlockSpec((MLA_KEYS, width), lambda b, j, i: (b, n_groups + j)),
        ],
        out_specs=pl.BlockSpec((MLA_Q_BLOCK, width), lambda b, j, i: (b * n_qb + i, j)),
        out_shape=jax.ShapeDtypeStruct((N_SAMPLE, ow), BF16),
        scratch_shapes=[pltpu.VMEM((MLA_PAIRS_PER_STEP, MLA_KEYS, 2 * LANES), BF16)],
        compiler_params=_cparams(("arbitrary", "arbitrary", "arbitrary")),
    )(q, q, kv, kpe, kv)


def _out_proj_kernel(x_ref, o_ref, gate_ref, w_ref, y_ref):
    y_ref[...] = x_ref[...] + gate_ref[0] * _dot(o_ref[...].astype(BF16), w_ref[...])


def _out_proj(x, o_prompt, o_sample, gate, w):
    n_p = N_PROMPT // ROW_TILE
    k = w.shape[0]
    last_p = n_p - 1
    return pl.pallas_call(
        _out_proj_kernel_two,
        grid=(N_TOK // ROW_TILE,),
        in_specs=[
            pl.BlockSpec((ROW_TILE, D_MODEL), lambda i: (i, 0)),
            pl.BlockSpec((ROW_TILE, k), lambda i: (jnp.minimum(i, last_p), 0)),
            pl.BlockSpec((ROW_TILE, k), lambda i: (jnp.maximum(i - n_p, 0), 0)),
            _mod_spec(ROW_TILE),
            _resident(w.shape),
        ],
        out_specs=pl.BlockSpec((ROW_TILE, D_MODEL), lambda i: (i, 0)),
        out_shape=jax.ShapeDtypeStruct((N_TOK, D_MODEL), F32),
        compiler_params=_cparams(("arbitrary",)),
    )(x, o_prompt, o_sample, gate, w)


def _out_proj_kernel_two(x_ref, op_ref, os_ref, gate_ref, w_ref, y_ref):
    is_prompt = pl.program_id(0) < N_PROMPT // ROW_TILE

    @pl.when(is_prompt)
    def _():
        _out_proj_kernel(x_ref, op_ref, gate_ref, w_ref, y_ref)

    @pl.when(jnp.logical_not(is_prompt))
    def _():
        _out_proj_kernel(x_ref, os_ref, gate_ref, w_ref, y_ref)


def _ffn_kernel(x_ref, g_ref, sh_ref, sc_ref, gate_ref, wg_ref, wu_ref, wd_ref, y_ref, acc_ref):
    x = x_ref[...]
    hb = (_rms(x, g_ref[...]) * (1.0 + sc_ref[0]) + sh_ref[0]).astype(BF16)
    acc_ref[...] = jnp.zeros_like(acc_ref)

    def chunk(c, carry):
        a = _silu(_dot(hb, wg_ref[c])) * _dot(hb, wu_ref[c])
        acc_ref[...] += _dot(a.astype(BF16), wd_ref[c])
        return carry

    lax.fori_loop(0, N_FFN_CHUNKS, chunk, 0)
    y_ref[...] = x + gate_ref[0] * acc_ref[...]


def _ffn(x, gain, shift, scale, gate, wg, wu, wd):
    return pl.pallas_call(
        _ffn_kernel,
        grid=(N_TOK // ROW_TILE,),
        in_specs=[
            pl.BlockSpec((ROW_TILE, D_MODEL), lambda i: (i, 0)),
            pl.BlockSpec((1, D_MODEL), lambda i: (0, 0)),
            _mod_spec(ROW_TILE), _mod_spec(ROW_TILE), _mod_spec(ROW_TILE),
            _resident(wg.shape), _resident(wu.shape), _resident(wd.shape),
        ],
        out_specs=pl.BlockSpec((ROW_TILE, D_MODEL), lambda i: (i, 0)),
        out_shape=jax.ShapeDtypeStruct((N_TOK, D_MODEL), F32),
        scratch_shapes=[pltpu.VMEM((ROW_TILE, D_MODEL), F32)],
        compiler_params=_cparams(("arbitrary",)),
    )(x, gain, shift, scale, gate, wg, wu, wd)


SLAB = D_MODEL // LANES


def _store_slabs(ref, x):
    n = x.shape[0]
    for s in range(SLAB):
        ref[pl.ds(s, n, stride=SLAB), :] = x[:, s * LANES:(s + 1) * LANES]


def _load_slabs(ref, n, dtype, row0=0):
    return jnp.concatenate([ref[pl.ds(row0 + s, n, stride=SLAB), :].astype(dtype) for s in range(SLAB)], axis=1)


META_E1, META_E2, META_W1, META_W2, META_R1, META_R2 = range(6)


def _router_kernel(x_ref, g_ref, sh_ref, sc_ref, whi_ref, wlo_ref, tri_ref, h_ref, meta_ref, cnt_ref, carry_ref):
    @pl.when(pl.program_id(0) == 0)
    def _():
        carry_ref[...] = jnp.zeros_like(carry_ref)

    h = _rms(x_ref[...], g_ref[...]) * (1.0 + sc_ref[0]) + sh_ref[0]
    _store_slabs(h_ref, h)
    h_hi, h_lo = _split_bf16(h)
    logits = _dot(h_hi, whi_ref[...]) + _dot(h_lo, whi_ref[...]) + _dot(h_hi, wlo_ref[...])
    lane = lax.broadcasted_iota(jnp.int32, logits.shape, 1)
    lane_f = lane.astype(F32)
    ninf = jnp.float32(-jnp.inf)
    lg = jnp.where(lane < N_EXPERTS, logits, ninf)
    m1 = jnp.max(lg, axis=-1, keepdims=True)
    i1 = jnp.min(jnp.where(lg == m1, lane_f, float(LANES)), axis=-1, keepdims=True)
    lg2 = jnp.where(lane_f == i1, ninf, lg)
    m2 = jnp.max(lg2, axis=-1, keepdims=True)
    i2 = jnp.min(jnp.where(lg2 == m2, lane_f, float(LANES)), axis=-1, keepdims=True)
    e = jnp.exp(m2 - m1)
    w1 = 1.0 / (1.0 + e)
    w2 = e / (1.0 + e)
    sel1 = lane_f == i1
    sel2 = lane_f == i2
    onehot = jnp.where(sel1 | sel2, 1.0, 0.0)
    ranks = _dot(tri_ref[...], onehot.astype(BF16)) + carry_ref[...]
    r1 = jnp.sum(jnp.where(sel1, ranks, 0.0), axis=-1, keepdims=True)
    r2 = jnp.sum(jnp.where(sel2, ranks, 0.0), axis=-1, keepdims=True)
    carry_ref[...] += jnp.sum(onehot, axis=0, keepdims=True)
    cnt_ref[...] = carry_ref[...]
    meta = jnp.zeros(logits.shape, F32)
    for idx, val in ((META_E1, i1), (META_E2, i2), (META_W1, w1), (META_W2, w2), (META_R1, r1), (META_R2, r2)):
        meta = jnp.where(lane == idx, val, meta)
    meta_ref[...] = meta


def _router(x, gain, shift, scale, w_hi, w_lo, tri):
    return pl.pallas_call(
        _router_kernel,
        grid=(N_TOK // ROW_TILE,),
        in_specs=[
            pl.BlockSpec((ROW_TILE, D_MODEL), lambda i: (i, 0)),
            pl.BlockSpec((1, D_MODEL), lambda i: (0, 0)),
            _mod_spec(ROW_TILE), _mod_spec(ROW_TILE),
            _resident(w_hi.shape), _resident(w_lo.shape), _resident(tri.shape),
        ],
        out_specs=[
            pl.BlockSpec((ROW_TILE * SLAB, LANES), lambda i: (i, 0)),
            pl.BlockSpec((ROW_TILE, LANES), lambda i: (i, 0)),
            pl.BlockSpec((1, LANES), lambda i: (0, 0)),
        ],
        out_shape=[
            jax.ShapeDtypeStruct((N_TOK * SLAB, LANES), F32),
            jax.ShapeDtypeStruct((N_TOK, LANES), F32),
            jax.ShapeDtypeStruct((1, LANES), F32),
        ],
        scratch_shapes=[pltpu.VMEM((1, LANES), F32)],
        compiler_params=_cparams(("arbitrary",)),
    )(x, gain, shift, scale, w_hi, w_lo, tri)


def _route_plan(meta, counts):
    cnt = counts[0, :N_EXPERTS].astype(jnp.int32)
    tiles = (cnt + EXP_TILE - 1) // EXP_TILE
    tile_end = jnp.cumsum(tiles)
    tile_start = tile_end - tiles
    offs = tile_start * EXP_TILE
    experts = jnp.arange(N_EXPERTS, dtype=jnp.int32)

    def lookup(table, idx):
        return jnp.sum(jnp.where(idx[:, None] == experts[None, :], table[None, :], 0), axis=1)

    e1 = meta[:, META_E1].astype(jnp.int32)
    e2 = meta[:, META_E2].astype(jnp.int32)
    pos1 = lookup(offs, e1) + meta[:, META_R1].astype(jnp.int32)
    pos2 = lookup(offs, e2) + meta[:, META_R2].astype(jnp.int32)
    t = jnp.arange(N_EXP_TILES, dtype=jnp.int32)
    tile_expert = jnp.minimum(jnp.sum(t[:, None] >= tile_end[None, :], axis=1), N_EXPERTS - 1).astype(jnp.int32)
    in_group = (t - lookup(tile_start, tile_expert)) * EXP_TILE
    tile_rows = jnp.where(t < tile_end[-1], jnp.clip(lookup(cnt, tile_expert) - in_group, 0, EXP_TILE),
                          0).astype(jnp.int32)
    return pos1, pos2, tile_expert, tile_rows


def _dispatch_kernel(pos1_ref, pos2_ref, h_ref, init_ref, xs_ref, sem):
    del init_ref
    base = pl.program_id(0) * DISPATCH_TILE

    def row_copy(t, dst):
        return pltpu.make_async_copy(
            h_ref.at[pl.ds(pl.multiple_of(t * SLAB, SLAB), SLAB)],
            xs_ref.at[pl.ds(pl.multiple_of(dst * SLAB, SLAB), SLAB)], sem)

    def issue(t, carry):
        row_copy(t, pos1_ref[base + t]).start()
        row_copy(t, pos2_ref[base + t]).start()
        return carry

    lax.fori_loop(0, DISPATCH_TILE, issue, 0)

    def drain(t, carry):
        row_copy(0, 0).wait()
        row_copy(0, 0).wait()
        return carry

    lax.fori_loop(0, DISPATCH_TILE, drain, 0)


def _dispatch(pos1, pos2, h_slabs, xs_init):
    return pl.pallas_call(
        _dispatch_kernel,
        grid_spec=pltpu.PrefetchScalarGridSpec(
            num_scalar_prefetch=2,
            grid=(N_TOK // DISPATCH_TILE,),
            in_specs=[pl.BlockSpec((DISPATCH_TILE * SLAB, LANES), lambda i, p1, p2: (i, 0)),
                      pl.BlockSpec(memory_space=pl.ANY)],
            out_specs=pl.BlockSpec(memory_space=pl.ANY),
            scratch_shapes=[pltpu.SemaphoreType.DMA(())],
        ),
        out_shape=jax.ShapeDtypeStruct(xs_init.shape, F32),
        input_output_aliases={3: 0},
        compiler_params=_cparams(("arbitrary",)),
    )(pos1, pos2, h_slabs, xs_init)


def _expert_kernel(te_ref, tr_ref, xs_ref, wg_ref, wu_ref, wd_ref, ys_ref, xb_ref, acc_ref):
    t = pl.program_id(0)
    c = pl.program_id(1)

    @pl.when(c == 0)
    def _():
        xb_ref[...] = _load_slabs(xs_ref, EXP_TILE, BF16)
        acc_ref[...] = jnp.zeros_like(acc_ref)

    rows = tr_ref[t]

    def swiglu_rows(n):
        xb = xb_ref[:n]
        a = _silu(_dot(xb, wg_ref[...].astype(BF16))) * _dot(xb, wu_ref[...].astype(BF16))
        acc_ref[:n] += _dot(a.astype(BF16), wd_ref[...].astype(BF16))

    @pl.when(rows > EXP_PART)
    def _():
        swiglu_rows(EXP_TILE)

    @pl.when((rows > 0) & (rows <= EXP_PART))
    def _():
        swiglu_rows(EXP_PART)

    @pl.when(c == N_EXP_CHUNKS - 1)
    def _():
        _store_slabs(ys_ref, acc_ref[...])


def _experts(layer, tile_expert, tile_rows, xs, wg, wu, wd):
    def chunk_of(t, c, tr):
        return jnp.where(tr[t] > 0, c, N_EXP_CHUNKS - 1)

    return pl.pallas_call(
        _expert_kernel,
        grid_spec=pltpu.PrefetchScalarGridSpec(
            num_scalar_prefetch=2,
            grid=(N_EXP_TILES, N_EXP_CHUNKS),
            in_specs=[
                pl.BlockSpec((EXP_TILE * SLAB, LANES), lambda t, c, te, tr: (t, 0)),
                pl.BlockSpec((None, None, D_MODEL, EXP_CHUNK),
                             lambda t, c, te, tr: (layer, te[t], 0, chunk_of(t, c, tr))),
                pl.BlockSpec((None, None, D_MODEL, EXP_CHUNK),
                             lambda t, c, te, tr: (layer, te[t], 0, chunk_of(t, c, tr))),
                pl.BlockSpec((None, None, EXP_CHUNK, D_MODEL),
                             lambda t, c, te, tr: (layer, te[t], chunk_of(t, c, tr), 0)),
            ],
            out_specs=pl.BlockSpec((EXP_TILE * SLAB, LANES), lambda t, c, te, tr: (t, 0)),
            scratch_shapes=[pltpu.VMEM((EXP_TILE, D_MODEL), BF16), pltpu.VMEM((EXP_TILE, D_MODEL), F32)],
        ),
        out_shape=jax.ShapeDtypeStruct(xs.shape, F32),
        compiler_params=_cparams(("arbitrary", "arbitrary")),
    )(tile_expert, tile_rows, xs, wg, wu, wd)


def _combine_kernel(pos1_ref, pos2_ref, x_ref, gate_ref, meta_ref, ys_ref, y_ref, buf_ref, sem):
    base = pl.program_id(0) * COMBINE_TILE
    n = COMBINE_TILE

    def row_copy(src, slot):
        return pltpu.make_async_copy(
            ys_ref.at[pl.ds(pl.multiple_of(src * SLAB, SLAB), SLAB)],
            buf_ref.at[pl.ds(pl.multiple_of(slot * SLAB, SLAB), SLAB)], sem)

    def issue(t, carry):
        row_copy(pos1_ref[base + t], t).start()
        row_copy(pos2_ref[base + t], n + t).start()
        return carry

    lax.fori_loop(0, n, issue, 0)

    def drain(t, carry):
        row_copy(0, 0).wait()
        row_copy(0, 0).wait()
        return carry

    lax.fori_loop(0, n, drain, 0)

    meta = meta_ref[...]
    w1 = meta[:, META_W1:META_W1 + 1]
    w2 = meta[:, META_W2:META_W2 + 1]
    y1 = _load_slabs(buf_ref, n, F32)
    y2 = _load_slabs(buf_ref, n, F32, row0=n * SLAB)
    y_ref[...] = x_ref[...] + gate_ref[0] * (w1 * y1 + w2 * y2)


def _combine(pos1, pos2, x, gate, meta, ys):
    return pl.pallas_call(
        _combine_kernel,
        grid_spec=pltpu.PrefetchScalarGridSpec(
            num_scalar_prefetch=2,
            grid=(N_TOK // COMBINE_TILE,),
            in_specs=[
                pl.BlockSpec((COMBINE_TILE, D_MODEL), lambda i, p1, p2: (i, 0)),
                pl.BlockSpec((1, 1, D_MODEL), lambda i, p1, p2: (_cond_of_row(i * COMBINE_TILE), 0, 0)),
                pl.BlockSpec((COMBINE_TILE, LANES), lambda i, p1, p2: (i, 0)),
                pl.BlockSpec(memory_space=pl.ANY),
            ],
            out_specs=pl.BlockSpec((COMBINE_TILE, D_MODEL), lambda i, p1, p2: (i, 0)),
            scratch_shapes=[pltpu.VMEM((2 * COMBINE_TILE * SLAB, LANES), F32), pltpu.SemaphoreType.DMA(())],
        ),
        out_shape=jax.ShapeDtypeStruct((N_TOK, D_MODEL), F32),
        compiler_params=_cparams(("arbitrary",)),
    )(pos1, pos2, x, gate, meta, ys)


def _moe(layer, x, gain, shift, scale, gate, router_w, wg, wu, wd, tri):
    w = jnp.pad(router_w, ((0, 0), (0, LANES - N_EXPERTS)))
    w_hi = w.astype(BF16)
    w_lo = (w - w_hi.astype(F32)).astype(BF16)
    h_slabs, meta, counts = _router(x, gain, shift, scale, w_hi, w_lo, tri)
    pos1, pos2, tile_expert, tile_rows = _route_plan(meta, counts)
    xs = _dispatch(pos1, pos2, h_slabs, jnp.zeros((N_EXP_TILES * EXP_TILE * SLAB, LANES), F32))
    ys = _experts(layer, tile_expert, tile_rows, xs, wg, wu, wd)
    return _combine(pos1, pos2, x, gate, meta, ys)


def _final_norm_kernel(x_ref, g_ref, y_ref):
    y_ref[...] = _rms(x_ref[...], g_ref[...])


def _final_norm(x, gain, row0, rows):
    off = row0 // ROW_TILE
    return pl.pallas_call(
        _final_norm_kernel,
        grid=(rows // ROW_TILE,),
        in_specs=[
            pl.BlockSpec((ROW_TILE, D_MODEL), lambda i: (off + i, 0)),
            pl.BlockSpec((1, D_MODEL), lambda i: (0, 0)),
        ],
        out_specs=pl.BlockSpec((ROW_TILE, D_MODEL), lambda i: (i, 0)),
        out_shape=jax.ShapeDtypeStruct((rows, D_MODEL), F32),
        compiler_params=_cparams(("arbitrary",)),
    )(x, gain)


def _rope_tables(rot_dim, reps):
    n_freq = rot_dim // 4
    inv_freq = ROPE_THETA ** (-np.arange(n_freq, dtype=np.float64) / n_freq)
    t = np.arange(DEC_SEQ)
    ang_r = (t // GRID_W)[:, None] * inv_freq
    ang_c = (t % GRID_W)[:, None] * inv_freq
    cos = np.concatenate([np.cos(ang_r), np.cos(ang_r), np.cos(ang_c), np.cos(ang_c)], axis=1)
    sin = np.concatenate([-np.sin(ang_r), np.sin(ang_r), -np.sin(ang_c), np.sin(ang_c)], axis=1)
    pad = LANES - reps * rot_dim
    cos = np.concatenate([cos] * reps + [np.ones((DEC_SEQ, pad))], axis=1)
    sin = np.concatenate([sin] * reps + [np.zeros((DEC_SEQ, pad))], axis=1)
    cos = np.concatenate([cos, np.ones((ROW_TILE, LANES))], axis=0)
    sin = np.concatenate([sin, np.zeros((ROW_TILE, LANES))], axis=0)
    return jnp.asarray(cos, F32), jnp.asarray(sin, F32)


def _swap_index(rot_dim):
    q = rot_dim // 4
    idx = np.arange(rot_dim)
    return np.where((idx // q) % 2 == 0, idx + q, idx - q)


def _gqa_q_perm():
    heads = []
    for p in range(GQA_KV_HEADS // 2):
        for m in range(GQA_GROUP):
            heads += [2 * p * GQA_GROUP + m, (2 * p + 1) * GQA_GROUP + m]
    return np.concatenate([np.arange(h * GQA_HEAD_DIM, (h + 1) * GQA_HEAD_DIM) for h in heads])


def _mla_layouts(wq_b, wkv_a, wkv_b):
    hd = MLA_NOPE + MLA_ROPE
    nope_cols = np.concatenate([np.arange(h * hd, h * hd + MLA_NOPE) for h in range(MLA_HEADS)])
    swap = _swap_index(MLA_ROPE)
    zeros_q = jnp.zeros((MLA_Q_LORA, LANES - 2 * MLA_ROPE), F32)
    rope_blocks, rope_swapped = [], []
    for j in range(MLA_PAIRS):
        for blocks, idx in ((rope_blocks, np.arange(MLA_ROPE)), (rope_swapped, swap)):
            blocks += [wq_b[:, (2 * j) * hd + MLA_NOPE + idx], wq_b[:, (2 * j + 1) * hd + MLA_NOPE + idx], zeros_q]
    wb = jnp.concatenate([wq_b[:, nope_cols]] + rope_blocks, axis=1)
    wbs = jnp.concatenate(rope_swapped, axis=1)
    zeros_a = jnp.zeros((D_MODEL, LANES - 2 * MLA_ROPE), F32)
    kpe_w = wkv_a[:, MLA_KV_LORA:]
    kpe_ws = kpe_w[:, swap]
    wa_tail = jnp.concatenate([wkv_a[:, :MLA_KV_LORA], kpe_w, kpe_w, zeros_a, kpe_ws, kpe_ws, zeros_a], axis=1)
    kvd = MLA_NOPE + MLA_V
    k_cols = np.concatenate([np.arange(h * kvd, h * kvd + MLA_NOPE) for h in range(MLA_HEADS)])
    v_cols = k_cols + MLA_NOPE
    wkvb = jnp.concatenate([wkv_b[:, k_cols], wkv_b[:, v_cols]], axis=1)
    return wb, wbs, wa_tail, wkvb


def _chunk_cols(w, chunk):
    k, n = w.shape
    return w.reshape(k, n // chunk, chunk).transpose(1, 0, 2).astype(BF16)


def kernel(x_prompt, x_sample, cache_l0_k, cache_l0_v, cache_l1_ckv, cache_l1_kpe, cache_l2_k, cache_l2_v, cache_l3_k, cache_l3_v, c, c_ctx, ada_w, ada_b, norm_mix, norm_ffn, norm_final, na_w_qkv, na_w_o, na_rel_bias, mla_wq_a, mla_q_norm, mla_wq_b, mla_wkv_a, mla_kv_norm, mla_wkv_b, mla_w_o, gqa_w_qkv, gqa_w_o, gqa_sink, ffn_w_gate, ffn_w_up, ffn_w_down, moe_router, moe_w_gate, moe_w_up, moe_w_down):
    x = jnp.concatenate([x_prompt.reshape(N_PROMPT, D_MODEL), x_sample.reshape(N_SAMPLE, D_MODEL)], axis=0)
    cond = jnp.concatenate([c_ctx[None, :], c, jnp.zeros((N_COND - 1 - DEC_BATCH, D_MODEL), F32)], axis=0)
    mods = _adaln(cond, ada_w, ada_b).reshape(DEPTH, N_COND, 6, 1, D_MODEL)
    tri = jnp.asarray(np.tril(np.ones((ROW_TILE, ROW_TILE), np.float32), -1), BF16)
    no_sink = jnp.zeros((GQA_HEADS,), F32)
    na_caches = {0: (cache_l0_k, cache_l0_v), 3: (cache_l3_k, cache_l3_v)}
    state = []

    for i in range(DEPTH):
        sh_m, sc_m, g_m, sh_f, sc_f, g_f = [mods[i, :, j] for j in range(6)]
        gain_m = norm_mix[i][None, :]
        gain_f = norm_ffn[i][None, :]
        kind, s = i % 3, i // 3
        if kind == 0:
            qw = NA_HEADS * NA_HEAD_DIM
            qkv, k_rows, v_rows = _norm_proj(x, gain_m, sh_m, sc_m, na_w_qkv[s].astype(BF16), qw)
            o_p = _ctx_attn(qkv, no_sink, qw, False)
            ck, cv = na_caches[i]
            o_s = _na_latent_attn(qkv, ck.reshape(DEC_BATCH, PAST_LEN, qw), cv.reshape(DEC_BATCH, PAST_LEN, qw),
                                  _na_pair_table(na_rel_bias[s]))
            w_o = na_w_o[s].astype(BF16)
            kv_shape = (BATCH, SEQ, NA_HEADS, NA_HEAD_DIM)
            state += [k_rows.reshape(kv_shape), v_rows.reshape(kv_shape)]
        elif kind == 1:
            wb, wbs, wa_tail, wkvb = _mla_layouts(mla_wq_b[s], mla_wkv_a[s], mla_wkv_b[s])
            wa = jnp.concatenate([mla_wq_a[s], wa_tail], axis=1).astype(BF16)
            cos, sin = _rope_tables(MLA_ROPE, 2)
            q, ckv, kpe = _mla_proj(x, gain_m, sh_m, sc_m, wa, mla_q_norm[s][None, :], mla_kv_norm[s][None, :],
                                    wb.astype(BF16), wbs.astype(BF16), cos, sin)
            wkvb = wkvb.astype(BF16)
            kv_p = _rows_matmul(ckv[:N_PROMPT], wkvb, BF16)
            o_p = _mla_ctx_attn(q, kv_p, kpe)
            ckv_s = ckv[N_PROMPT:].reshape(DEC_BATCH, DEC_SEQ, MLA_KV_LORA)
            ckv_all = jnp.concatenate([cache_l1_ckv, ckv_s], axis=1).reshape(DEC_BATCH * MLA_KEYS, MLA_KV_LORA)
            kv_s = _rows_matmul(ckv_all, wkvb, BF16)
            kpe_cache = jnp.concatenate(
                [cache_l1_kpe, cache_l1_kpe, jnp.zeros((DEC_BATCH, PAST_LEN, LANES - 2 * MLA_ROPE), F32)], axis=2)
            kpe_all = jnp.concatenate([kpe_cache, kpe[N_PROMPT:].reshape(DEC_BATCH, DEC_SEQ, LANES)], axis=1)
            o_s = _mla_latent_attn(q, kv_s, kpe_all.reshape(DEC_BATCH * MLA_KEYS, LANES))
            w_o = mla_w_o[s].astype(BF16)
            state += [ckv[:N_PROMPT].reshape(BATCH, SEQ, MLA_KV_LORA),
                      kpe[:N_PROMPT, :MLA_ROPE].reshape(BATCH, SEQ, MLA_ROPE)]
        else:
            qw = GQA_HEADS * GQA_HEAD_DIM
            kw = GQA_KV_HEADS * GQA_HEAD_DIM
            perm = _gqa_q_perm()
            w_qkv = jnp.concatenate([gqa_w_qkv[s][:, perm], gqa_w_qkv[s][:, qw:]], axis=1)
            swap = np.concatenate([h * GQA_HEAD_DIM + _swap_index(GQA_HEAD_DIM) for h in range((qw + kw) // GQA_HEAD_DIM)])
            cos, sin = _rope_tables(GQA_HEAD_DIM, 2)
            qkv, k_rows, v_rows = _norm_proj(x, gain_m, sh_m, sc_m, w_qkv.astype(BF16), kw,
                                             rope=(w_qkv[:, swap].astype(BF16), cos, sin))
            sink = gqa_sink[s]
            o_p = _ctx_attn(qkv, sink, kw, True)
            o_s = _gqa_latent_attn(qkv, sink, cache_l2_k.reshape(DEC_BATCH, PAST_LEN, kw),
                                   cache_l2_v.reshape(DEC_BATCH, PAST_LEN, kw))
            w_o = gqa_w_o[s][perm].astype(BF16)
            kv_shape = (BATCH, SEQ, GQA_KV_HEADS, GQA_HEAD_DIM)
            state += [k_rows.reshape(kv_shape), v_rows.reshape(kv_shape)]
        x = _out_proj(x, o_p, o_s, g_m, w_o)

        if i % 2 == 0:
            s = i // 2
            x = _ffn(x, gain_f, sh_f, sc_f, g_f, _chunk_cols(ffn_w_gate[s], FFN_CHUNK),
                     _chunk_cols(ffn_w_up[s], FFN_CHUNK),
                     ffn_w_down[s].reshape(N_FFN_CHUNKS, FFN_CHUNK, D_MODEL).astype(BF16))
        else:
            s = i // 2
            x = _moe(s, x, gain_f, sh_f, sc_f, g_f, moe_router[s], moe_w_gate, moe_w_up, moe_w_down, tri)

    y_prompt = _final_norm(x, norm_final[None, :], 0, N_PROMPT).reshape(BATCH, SEQ, D_MODEL)
    y_sample = _final_norm(x, norm_final[None, :], N_PROMPT, N_SAMPLE).reshape(DEC_BATCH, DEC_SEQ, D_MODEL)
    return (y_prompt, y_sample, *state)
```

```python
import functools
import math

import numpy as np
import jax
import jax.numpy as jnp
from jax import lax
from jax.experimental import pallas as pl
from jax.experimental.pallas import tpu as pltpu

F32 = jnp.float32
BF16 = jnp.bfloat16

D_MODEL = 1024
BATCH = 16
SEQ = 256
DEPTH = 4
DEC_BATCH = 4
DEC_SEQ = 2048
PAST_LEN = 512
GRID_W = 64
GRID_ROWS = DEC_SEQ // GRID_W
NORM_EPS = 1e-6
NEG_INF = -1e30
ROPE_THETA = 10000.0
NA_HEADS = 16
NA_HEAD_DIM = 64
NA_WIN_ROWS = 8
NA_WIN_COLS = 16
MLA_HEADS = 16
MLA_Q_LORA = 512
MLA_KV_LORA = 256
MLA_NOPE = 64
MLA_ROPE = 32
MLA_V = 64
GQA_HEADS = 16
GQA_KV_HEADS = 4
GQA_HEAD_DIM = 64
GQA_WINDOW = 128
FFN_DIM = 2816
N_EXPERTS = 8
EXPERT_DIM = 3584

N_PROMPT = BATCH * SEQ
N_SAMPLE = DEC_BATCH * DEC_SEQ
N_TOK = N_PROMPT + N_SAMPLE
N_COND = 8
LOG2E = math.log2(math.e)
HEAD_SCALE = NA_HEAD_DIM ** -0.5 * LOG2E
MLA_SCALE = (MLA_NOPE + MLA_ROPE) ** -0.5 * LOG2E

LANES = 128
SUBLANES = 8
VMEM_LIMIT = 56 * 1024 * 1024

ROW_TILE = 512
FFN_CHUNK = 256
N_FFN_CHUNKS = FFN_DIM // FFN_CHUNK
EXP_TILE = 1024
EXP_PART = 512
EXP_CHUNK = 512
N_EXP_CHUNKS = EXPERT_DIM // EXP_CHUNK
N_EXP_TILES = (2 * N_TOK) // EXP_TILE + N_EXPERTS
COMBINE_TILE = 256
NA_QROWS = 4
NA_UNION = NA_QROWS + NA_WIN_ROWS
NA_PAIRS_PER_STEP = 4
MLA_KEYS = PAST_LEN + DEC_SEQ


def _cparams(sem):
    return pltpu.CompilerParams(dimension_semantics=sem, vmem_limit_bytes=VMEM_LIMIT)


def _cond_of_row(row):
    return jnp.where(row < N_PROMPT, 0, 1 + (row - N_PROMPT) // DEC_SEQ)


def _nt_dot(a, b):
    return lax.dot_general(a, b, (((1,), (1,)), ((), ())), preferred_element_type=F32)


def _dot(a, b):
    return jnp.dot(a, b, preferred_element_type=F32)


def _rms(x, gain):
    return x * lax.rsqrt(jnp.mean(x * x, axis=-1, keepdims=True) + NORM_EPS) * gain


def _silu(x):
    return x / (1.0 + jnp.exp(-x))


def _split_bf16(x):
    hi = x.astype(BF16)
    lo = (x - hi.astype(F32)).astype(BF16)
    return hi, lo


def _resident(shape):
    nd = len(shape)
    return pl.BlockSpec(shape, lambda *_: (0,) * nd, pipeline_mode=pl.Buffered(1))


def _mod_spec(tile):
    return pl.BlockSpec((1, 1, D_MODEL), lambda i: (_cond_of_row(i * tile), 0, 0))


ADA_TILE = 1536


def _adaln_kernel(c_ref, w_ref, b_ref, o_ref):
    a_hi, a_lo = _split_bf16(_silu(c_ref[...]))
    w_hi, w_lo = _split_bf16(w_ref[0])
    o_ref[0] = _dot(a_hi, w_hi) + _dot(a_lo, w_hi) + _dot(a_hi, w_lo) + b_ref[0]


def _adaln(cond, ada_w, ada_b):
    n = 6 * D_MODEL
    return pl.pallas_call(
        _adaln_kernel,
        grid=(DEPTH, n // ADA_TILE),
        in_specs=[
            pl.BlockSpec((N_COND, D_MODEL), lambda l, j: (0, 0)),
            pl.BlockSpec((1, D_MODEL, ADA_TILE), lambda l, j: (l, 0, j)),
            pl.BlockSpec((1, 1, ADA_TILE), lambda l, j: (l, 0, j)),
        ],
        out_specs=pl.BlockSpec((1, N_COND, ADA_TILE), lambda l, j: (l, 0, j)),
        out_shape=jax.ShapeDtypeStruct((DEPTH, N_COND, n), F32),
        compiler_params=_cparams(("arbitrary", "arbitrary")),
    )(cond, ada_w, ada_b.reshape(DEPTH, 1, n))


PROJ_COLS = 512


def _norm_proj_kernel(x_ref, g_ref, sh_ref, sc_ref, w_ref, *rest, n_out, n_rope, kv_cols):
    rest = list(rest)
    if n_rope:
        w2_ref, cos_ref, sin_ref = rest[:3]
        rest = rest[3:]
    o_ref, k_ref, v_ref = rest
    h = _rms(x_ref[...], g_ref[...]) * (1.0 + sc_ref[0]) + sh_ref[0]
    hb = h.astype(BF16)
    is_prompt = pl.program_id(0) < N_PROMPT // ROW_TILE
    cuts = sorted({0, n_out, n_rope, n_out - 2 * kv_cols, n_out - kv_cols}
                  | set(range(0, n_out, PROJ_COLS)))
    for c0, c1 in zip(cuts[:-1], cuts[1:]):
        cols = slice(c0, c1)
        y = _dot(hb, w_ref[:, cols])
        if c0 < n_rope:
            ys = _dot(hb, w2_ref[:, cols])
            reps = (c1 - c0) // LANES
            cos = jnp.concatenate([cos_ref[...]] * reps, axis=1)
            sin = jnp.concatenate([sin_ref[...]] * reps, axis=1)
            y = y * cos + ys * sin
        o_ref[:, cols] = y
        for kv_ref, start in ((k_ref, n_out - 2 * kv_cols), (v_ref, n_out - kv_cols)):
            if start <= c0 < start + kv_cols:
                @pl.when(is_prompt)
                def _(kv_ref=kv_ref, start=start, y=y, c0=c0, c1=c1):
                    kv_ref[:, c0 - start:c1 - start] = y


def _rope_block(i):
    row = i * ROW_TILE
    return jnp.where(row < N_PROMPT, DEC_SEQ // ROW_TILE, ((row - N_PROMPT) % DEC_SEQ) // ROW_TILE)


def _norm_proj(x, gain, shift, scale, w, kv_cols, rope=None):
    n_out = w.shape[1]
    last_prompt = N_PROMPT // ROW_TILE - 1
    kv_spec = pl.BlockSpec((ROW_TILE, kv_cols), lambda i: (jnp.minimum(i, last_prompt), 0))
    kv_shape = jax.ShapeDtypeStruct((N_PROMPT, kv_cols), F32)
    in_specs = [
        pl.BlockSpec((ROW_TILE, D_MODEL), lambda i: (i, 0)),
        pl.BlockSpec((1, D_MODEL), lambda i: (0, 0)),
        _mod_spec(ROW_TILE),
        _mod_spec(ROW_TILE),
        _resident(w.shape),
    ]
    args = [x, gain, shift, scale, w]
    n_rope = 0
    if rope is not None:
        w2, cos, sin = rope
        n_rope = w2.shape[1]
        in_specs += [
            _resident(w2.shape),
            pl.BlockSpec((ROW_TILE, LANES), lambda i: (_rope_block(i), 0)),
            pl.BlockSpec((ROW_TILE, LANES), lambda i: (_rope_block(i), 0)),
        ]
        args += [w2, cos, sin]
    return pl.pallas_call(
        functools.partial(_norm_proj_kernel, n_out=n_out, n_rope=n_rope, kv_cols=kv_cols),
        grid=(N_TOK // ROW_TILE,),
        in_specs=in_specs,
        out_specs=[pl.BlockSpec((ROW_TILE, n_out), lambda i: (i, 0)), kv_spec, kv_spec],
        out_shape=[jax.ShapeDtypeStruct((N_TOK, n_out), F32), kv_shape, kv_shape],
        compiler_params=_cparams(("arbitrary",)),
    )(*args)


MLA_Q_COLS = 2 * MLA_HEADS * MLA_NOPE


def _mla_proj_kernel(x_ref, g_ref, sh_ref, sc_ref, wa_ref, qn_ref, kvn_ref, wb_ref, wbs_ref,
                     cos_ref, sin_ref, q_ref, ckv_ref, kpe_ref):
    h = _rms(x_ref[...], g_ref[...]) * (1.0 + sc_ref[0]) + sh_ref[0]
    a = _dot(h.astype(BF16), wa_ref[...])
    c0 = MLA_Q_LORA
    c1 = c0 + MLA_KV_LORA
    cos = cos_ref[...]
    sin = sin_ref[...]
    ckv_ref[...] = _rms(a[:, c0:c1], kvn_ref[...])
    kpe_ref[...] = a[:, c1:c1 + LANES] * cos + a[:, c1 + LANES:c1 + 2 * LANES] * sin
    qn = _rms(a[:, :c0], qn_ref[...]).astype(BF16)
    half = MLA_Q_COLS // 2
    q_scale = MLA_SCALE
    q_ref[:, :half] = (_dot(qn, wb_ref[:, :half]) * q_scale).astype(BF16)
    q_rope = _dot(qn, wb_ref[:, half:])
    q_swap = _dot(qn, wbs_ref[...])
    reps = half // LANES
    cos_r = jnp.concatenate([cos] * reps, axis=1) * q_scale
    sin_r = jnp.concatenate([sin] * reps, axis=1) * q_scale
    q_ref[:, half:] = (q_rope * cos_r + q_swap * sin_r).astype(BF16)


def _mla_proj(x, gain, shift, scale, wa, q_norm, kv_norm, wb, wbs, cos, sin):
    return pl.pallas_call(
        _mla_proj_kernel,
        grid=(N_TOK // ROW_TILE,),
        in_specs=[
            pl.BlockSpec((ROW_TILE, D_MODEL), lambda i: (i, 0)),
            pl.BlockSpec((1, D_MODEL), lambda i: (0, 0)),
            _mod_spec(ROW_TILE),
            _mod_spec(ROW_TILE),
            _resident(wa.shape),
            pl.BlockSpec((1, MLA_Q_LORA), lambda i: (0, 0)),
            pl.BlockSpec((1, MLA_KV_LORA), lambda i: (0, 0)),
            _resident(wb.shape),
            _resident(wbs.shape),
            pl.BlockSpec((ROW_TILE, LANES), lambda i: (_rope_block(i), 0)),
            pl.BlockSpec((ROW_TILE, LANES), lambda i: (_rope_block(i), 0)),
        ],
        out_specs=[
            pl.BlockSpec((ROW_TILE, MLA_Q_COLS), lambda i: (i, 0)),
            pl.BlockSpec((ROW_TILE, MLA_KV_LORA), lambda i: (i, 0)),
            pl.BlockSpec((ROW_TILE, LANES), lambda i: (i, 0)),
        ],
        out_shape=[
            jax.ShapeDtypeStruct((N_TOK, MLA_Q_COLS), BF16),
            jax.ShapeDtypeStruct((N_TOK, MLA_KV_LORA), F32),
            jax.ShapeDtypeStruct((N_TOK, LANES), F32),
        ],
        compiler_params=_cparams(("arbitrary",)),
    )(x, gain, shift, scale, wa, q_norm, kv_norm, wb, wbs, cos, sin)


def _rows_matmul_kernel(x_ref, w_ref, o_ref):
    o_ref[...] = _dot(x_ref[...].astype(BF16), w_ref[...]).astype(o_ref.dtype)


def _rows_matmul(x, w, out_dtype):
    rows, k = x.shape
    n = w.shape[1]
    return pl.pallas_call(
        _rows_matmul_kernel,
        grid=(rows // ROW_TILE,),
        in_specs=[pl.BlockSpec((ROW_TILE, k), lambda i: (i, 0)), _resident(w.shape)],
        out_specs=pl.BlockSpec((ROW_TILE, n), lambda i: (i, 0)),
        out_shape=jax.ShapeDtypeStruct((rows, n), out_dtype),
        compiler_params=_cparams(("arbitrary",)),
    )(x, w)


def _pair_attention(q, masks, keys, values, biases, q_scale, sinks):
    half = LANES // 2
    lane = lax.broadcasted_iota(jnp.int32, (1, LANES), 1)
    outs = []
    sink_terms = []
    for h in range(2):
        qh = jnp.where(masks[h], q, jnp.zeros_like(q))
        if q_scale != 1.0:
            qh = qh * q_scale
        qh = qh.astype(BF16)
        scores = []
        for k, bias in zip(keys, biases):
            s = _nt_dot(qh, k)
            if bias is not None:
                s = s + bias(h)
            scores.append(s)
        m = functools.reduce(jnp.maximum, [jnp.max(s, axis=-1, keepdims=True) for s in scores])
        if sinks is not None:
            m = jnp.maximum(m, sinks[h])
        own = (lane < half) if h == 0 else (lane >= half)
        acc = None
        for s, v in zip(scores, values):
            part = _dot(jnp.exp2(s - m).astype(BF16), jnp.where(own, v, jnp.ones_like(v)))
            acc = part if acc is None else acc + part
        outs.append(acc)
        if sinks is not None:
            sink_terms.append(jnp.exp2(sinks[h] - m))
    first = lane < half
    numer = jnp.where(first, outs[0], outs[1])
    denom = pltpu.roll(jnp.where(first, outs[1], outs[0]), half, axis=1)
    if sinks is not None:
        denom = denom + jnp.where(first, sink_terms[0], sink_terms[1])
    return (numer / denom).astype(BF16)


def _half_masks(width=LANES):
    lane = lax.broadcasted_iota(jnp.int32, (1, width), 1)
    return [lane < LANES // 2, lane >= LANES // 2]


def _ctx_attn_kernel(sink_ref, q_ref, k_ref, v_ref, o_ref, *, kv_blocks, use_sink):
    masks = _half_masks()
    n_q_blocks = o_ref.shape[1] // LANES
    for qb in range(n_q_blocks):
        kb = qb * kv_blocks // n_q_blocks
        cols = slice(kb * LANES, (kb + 1) * LANES)
        sinks = None
        if use_sink:
            base = (qb // 4) * 8 + qb % 4
            sinks = [sink_ref[base] * LOG2E, sink_ref[base + 4] * LOG2E]
        o_ref[:, qb * LANES:(qb + 1) * LANES] = _pair_attention(
            q_ref[:, qb * LANES:(qb + 1) * LANES], masks,
            [k_ref[:, cols].astype(BF16)], [v_ref[:, cols].astype(BF16)], [None],
            HEAD_SCALE, sinks)


def _ctx_attn(qkv, sink, kv_width, use_sink):
    qw = NA_HEADS * NA_HEAD_DIM
    k_blk = qw // kv_width
    return pl.pallas_call(
        functools.partial(_ctx_attn_kernel, kv_blocks=kv_width // LANES, use_sink=use_sink),
        grid_spec=pltpu.PrefetchScalarGridSpec(
            num_scalar_prefetch=1,
            grid=(BATCH,),
            in_specs=[
                pl.BlockSpec((SEQ, qw), lambda b, s: (b, 0)),
                pl.BlockSpec((SEQ, kv_width), lambda b, s: (b, k_blk)),
                pl.BlockSpec((SEQ, kv_width), lambda b, s: (b, k_blk + 1)),
            ],
            out_specs=pl.BlockSpec((SEQ, qw), lambda b, s: (b, 0)),
        ),
        out_shape=jax.ShapeDtypeStruct((N_PROMPT, qw), BF16),
        compiler_params=_cparams(("arbitrary",)),
    )(sink, qkv, qkv, qkv)


def _na_union_start(rb):
    return jnp.clip(rb * NA_QROWS - NA_WIN_ROWS // 2, 0, GRID_ROWS - NA_UNION)


NA_DR_SLOTS = 2 * NA_WIN_ROWS


def _na_bias(pairs_ref, h, rb):
    u0 = _na_union_start(rb)
    left = lax.broadcasted_iota(jnp.int32, (1, LANES), 1) < GRID_W
    rows = []
    for j in range(NA_QROWS):
        r = rb * NA_QROWS + j
        start = jnp.clip(r - NA_WIN_ROWS // 2, 0, GRID_ROWS - NA_WIN_ROWS)
        blocks = []
        for ip in range(NA_UNION // 2):
            key_row = u0 + 2 * ip
            dr = key_row - r + NA_WIN_ROWS - 1
            ok_l = ((key_row >= start) & (key_row < start + NA_WIN_ROWS)).astype(jnp.int32)
            ok_r = ((key_row + 1 >= start) & (key_row + 1 < start + NA_WIN_ROWS)).astype(jnp.int32)
            blk = pairs_ref[h, jnp.clip(dr + 1, 0, NA_DR_SLOTS - 1)]
            blocks.append(jnp.where(jnp.where(left, ok_l, ok_r) > 0, blk, NEG_INF))
        rows.append(jnp.concatenate(blocks, axis=1))
    return jnp.concatenate(rows, axis=0)


def _na_latent_kernel(q_ref, k_ref, v_ref, kc_ref, vc_ref, pairs_ref, o_ref):
    rb = pl.program_id(2)
    start = pl.multiple_of(_na_union_start(rb) * GRID_W, GRID_W)
    n_loc = NA_UNION * GRID_W
    for p in range(NA_PAIRS_PER_STEP):
        cols = slice(p * LANES, (p + 1) * LANES)
        k_loc = k_ref[pl.ds(start, n_loc), cols].astype(BF16)
        v_loc = v_ref[pl.ds(start, n_loc), cols].astype(BF16)
        o_ref[:, cols] = _pair_attention(
            q_ref[:, cols], _half_masks(),
            [k_loc, kc_ref[0, :, cols].astype(BF16)], [v_loc, vc_ref[0, :, cols].astype(BF16)],
            [lambda h, p=p: _na_bias(pairs_ref, 2 * p + h, rb), None], HEAD_SCALE, None)


def _na_latent_attn(qkv, k_ctx, v_ctx, pairs):
    qw = NA_HEADS * NA_HEAD_DIM
    width = NA_PAIRS_PER_STEP * LANES
    n_groups = qw // width
    q_rows = NA_QROWS * GRID_W
    n_rb = DEC_SEQ // q_rows
    q_off = N_PROMPT // q_rows
    kv_off = N_PROMPT // DEC_SEQ
    return pl.pallas_call(
        _na_latent_kernel,
        grid=(DEC_BATCH, n_groups, n_rb),
        in_specs=[
            pl.BlockSpec((q_rows, width), lambda b, j, r: (q_off + b * n_rb + r, j)),
            pl.BlockSpec((DEC_SEQ, width), lambda b, j, r: (kv_off + b, n_groups + j)),
            pl.BlockSpec((DEC_SEQ, width), lambda b, j, r: (kv_off + b, 2 * n_groups + j)),
            pl.BlockSpec((1, PAST_LEN, width), lambda b, j, r: (b, 0, j)),
            pl.BlockSpec((1, PAST_LEN, width), lambda b, j, r: (b, 0, j)),
            pl.BlockSpec((2 * NA_PAIRS_PER_STEP, NA_DR_SLOTS, GRID_W, LANES), lambda b, j, r: (j, 0, 0, 0)),
        ],
        out_specs=pl.BlockSpec((q_rows, width), lambda b, j, r: (b * n_rb + r, j)),
        out_shape=jax.ShapeDtypeStruct((N_SAMPLE, qw), BF16),
        compiler_params=_cparams(("arbitrary", "arbitrary", "arbitrary")),
    )(qkv, qkv, qkv, k_ctx, v_ctx, pairs)


def _na_pair_table(rel_bias):
    cols = np.arange(GRID_W)
    col_start = np.clip(cols - NA_WIN_COLS // 2, 0, GRID_W - NA_WIN_COLS)
    col_ok = (cols[None, :] >= col_start[:, None]) & (cols[None, :] < col_start[:, None] + NA_WIN_COLS)
    dc = np.clip(cols[None, :] - cols[:, None] + NA_WIN_COLS - 1, 0, 2 * NA_WIN_COLS - 2)
    n_dc = 2 * NA_WIN_COLS - 1
    n_dr = 2 * NA_WIN_ROWS - 1
    pick = np.zeros((n_dc, GRID_W * GRID_W), np.float32)
    pick[dc.reshape(-1), np.arange(GRID_W * GRID_W)] = 1.0
    flat = jnp.dot(rel_bias.astype(F32).reshape(NA_HEADS * n_dr, n_dc), jnp.asarray(pick),
                   precision=lax.Precision.HIGHEST)
    bias = jnp.where(col_ok.reshape(1, 1, -1), flat.reshape(NA_HEADS, n_dr, -1) * LOG2E, NEG_INF)
    bias = bias.reshape(NA_HEADS, n_dr, GRID_W, GRID_W)
    masked = jnp.full((NA_HEADS, 1, GRID_W, GRID_W), NEG_INF, F32)
    ext = jnp.concatenate([masked, bias, masked], axis=1)
    return jnp.concatenate([ext[:, :-1], ext[:, 1:]], axis=-1)


GQA_Q_BLOCK = GQA_WINDOW
GQA_GROUP = GQA_HEADS // GQA_KV_HEADS


def _gqa_latent_kernel(sink_ref, q_ref, kp_ref, kc_ref, kn_ref, vp_ref, vc_ref, vn_ref,
                       kx_ref, vx_ref, o_ref):
    pair = pl.program_id(1)
    qb = pl.program_id(2)
    n_qb = pl.num_programs(2)
    blk = GQA_Q_BLOCK
    rows = GQA_GROUP * blk
    qq = lax.broadcasted_iota(jnp.int32, (rows, blk), 0) % blk
    kk = lax.broadcasted_iota(jnp.int32, (rows, blk), 1)
    neg = jnp.float32(NEG_INF)
    bias_prev = jnp.where((kk >= qq) & (qb > 0), 0.0, neg)
    bias_next = jnp.where((kk <= qq) & (qb < n_qb - 1), 0.0, neg)
    row_grp = lax.broadcasted_iota(jnp.int32, (rows, 1), 0) // blk
    q = jnp.concatenate([q_ref[:, m * LANES:(m + 1) * LANES] for m in range(GQA_GROUP)], axis=0)
    sinks = []
    for h in range(2):
        base = pair * 2 * GQA_GROUP + h * GQA_GROUP
        col = jnp.zeros((rows, 1), F32)
        for m in range(GQA_GROUP):
            col = jnp.where(row_grp == m, sink_ref[base + m] * LOG2E, col)
        sinks.append(col)
    o = _pair_attention(
        q, _half_masks(),
        [kp_ref[...].astype(BF16), kc_ref[...].astype(BF16), kn_ref[...].astype(BF16), kx_ref[0].astype(BF16)],
        [vp_ref[...].astype(BF16), vc_ref[...].astype(BF16), vn_ref[...].astype(BF16), vx_ref[0].astype(BF16)],
        [lambda h: bias_prev, None, lambda h: bias_next, None], HEAD_SCALE, sinks)
    for m in range(GQA_GROUP):
        o_ref[:, m * LANES:(m + 1) * LANES] = o[m * blk:(m + 1) * blk]


def _gqa_latent_attn(qkv, sink, k_ctx, v_ctx):
    qw = GQA_HEADS * GQA_HEAD_DIM
    blk = GQA_Q_BLOCK
    n_qb = DEC_SEQ // blk
    q_off = N_PROMPT // blk
    q_cols = GQA_GROUP * LANES
    n_pairs = GQA_KV_HEADS // 2
    k_col = qw // LANES
    v_col = k_col + n_pairs

    def kv_spec(col0, shift):
        def imap(b, p, i, s):
            return (q_off + b * n_qb + jnp.clip(i + shift, 0, n_qb - 1), col0 + p)
        return pl.BlockSpec((blk, LANES), imap)

    ctx_spec = pl.BlockSpec((1, PAST_LEN, LANES), lambda b, p, i, s: (b, 0, p))
    return pl.pallas_call(
        _gqa_latent_kernel,
        grid_spec=pltpu.PrefetchScalarGridSpec(
            num_scalar_prefetch=1,
            grid=(DEC_BATCH, n_pairs, n_qb),
            in_specs=[
                pl.BlockSpec((blk, q_cols), lambda b, p, i, s: (q_off + b * n_qb + i, p)),
                kv_spec(k_col, -1), kv_spec(k_col, 0), kv_spec(k_col, 1),
                kv_spec(v_col, -1), kv_spec(v_col, 0), kv_spec(v_col, 1),
                ctx_spec, ctx_spec,
            ],
            out_specs=pl.BlockSpec((blk, q_cols), lambda b, p, i, s: (b * n_qb + i, p)),
        ),
        out_shape=jax.ShapeDtypeStruct((N_SAMPLE, qw), BF16),
        compiler_params=_cparams(("arbitrary", "arbitrary", "arbitrary")),
    )(sink, qkv, qkv, qkv, qkv, qkv, qkv, qkv, k_ctx, v_ctx)


MLA_PAIRS = MLA_HEADS // 2


def _mla_masks():
    lane = lax.broadcasted_iota(jnp.int32, (1, 2 * LANES), 1)
    half = LANES // 2
    even = (lane < half) | ((lane >= LANES) & (lane < LANES + MLA_ROPE))
    odd = ((lane >= half) & (lane < LANES)) | ((lane >= LANES + MLA_ROPE) & (lane < LANES + 2 * MLA_ROPE))
    return [even, odd]


def _mla_ctx_kernel(q_ref, kv_ref, kpe_ref, o_ref):
    masks = _mla_masks()
    kpe = kpe_ref[...].astype(BF16)
    half = MLA_Q_COLS // 2
    for j in range(MLA_PAIRS):
        cols = slice(j * LANES, (j + 1) * LANES)
        rcols = slice(half + j * LANES, half + (j + 1) * LANES)
        q = jnp.concatenate([q_ref[:, cols], q_ref[:, rcols]], axis=1)
        k = jnp.concatenate([kv_ref[:, cols], kpe], axis=1)
        o_ref[:, cols] = _pair_attention(q, masks, [k], [kv_ref[:, rcols]], [None], 1.0, None)


def _mla_ctx_attn(q, kv, kpe):
    ow = MLA_HEADS * MLA_V
    return pl.pallas_call(
        _mla_ctx_kernel,
        grid=(BATCH,),
        in_specs=[
            pl.BlockSpec((SEQ, MLA_Q_COLS), lambda b: (b, 0)),
            pl.BlockSpec((SEQ, MLA_Q_COLS), lambda b: (b, 0)),
            pl.BlockSpec((SEQ, LANES), lambda b: (b, 0)),
        ],
        out_specs=pl.BlockSpec((SEQ, ow), lambda b: (b, 0)),
        out_shape=jax.ShapeDtypeStruct((N_PROMPT, ow), BF16),
        compiler_params=_cparams(("arbitrary",)),
    )(q, kv, kpe)


MLA_Q_BLOCK = 256


MLA_PAIRS_PER_STEP = 2


def _mla_latent_kernel(qn_ref, qr_ref, ka_ref, kpe_ref, v_ref, o_ref, kcat_ref):
    @pl.when(pl.program_id(2) == 0)
    def _():
        kpe = kpe_ref[...].astype(BF16)
        for p in range(MLA_PAIRS_PER_STEP):
            kcat_ref[p, :, :LANES] = ka_ref[:, p * LANES:(p + 1) * LANES]
            kcat_ref[p, :, LANES:] = kpe

    for p in range(MLA_PAIRS_PER_STEP):
        cols = slice(p * LANES, (p + 1) * LANES)
        q = jnp.concatenate([qn_ref[:, cols], qr_ref[:, cols]], axis=1)
        o_ref[:, cols] = _pair_attention(q, _mla_masks(), [kcat_ref[p]], [v_ref[:, cols]], [None], 1.0, None)


def _mla_latent_attn(q, kv, kpe):
    ow = MLA_HEADS * MLA_V
    n_qb = DEC_SEQ // MLA_Q_BLOCK
    q_off = N_PROMPT // MLA_Q_BLOCK
    width = MLA_PAIRS_PER_STEP * LANES
    n_groups = MLA_PAIRS // MLA_PAIRS_PER_STEP
    return pl.pallas_call(
        _mla_latent_kernel,
        grid=(DEC_BATCH, n_groups, n_qb),
        in_specs=[
            pl.BlockSpec((MLA_Q_BLOCK, width), lambda b, j, i: (q_off + b * n_qb + i, j)),
            pl.BlockSpec((MLA_Q_BLOCK, width), lambda b, j, i: (q_off + b * n_qb + i, n_groups + j)),
            pl.BlockSpec((MLA_KEYS, width), lambda b, j, i: (b, j)),
            pl.BlockSpec((MLA_KEYS, LANES), lambda b, j, i: (b, 0)),
            pl.BlockSpec((MLA_KEYS, width), lambda b, j, i: (b, n_groups + j)),
        ],
        out_specs=pl.BlockSpec((MLA_Q_BLOCK, width), lambda b, j, i: (b * n_qb + i, j)),
        out_shape=jax.ShapeDtypeStruct((N_SAMPLE, ow), BF16),
        scratch_shapes=[pltpu.VMEM((MLA_PAIRS_PER_STEP, MLA_KEYS, 2 * LANES), BF16)],
        compiler_params=_cparams(("arbitrary", "arbitrary", "arbitrary")),
    )(q, q, kv, kpe, kv)


def _out_proj_kernel(x_ref, o_ref, gate_ref, w_ref, y_ref):
    y_ref[...] = x_ref[...] + gate_ref[0] * _dot(o_ref[...].astype(BF16), w_ref[...])


def _out_proj(x, o_prompt, o_sample, gate, w):
    n_p = N_PROMPT // ROW_TILE
    k = w.shape[0]
    last_p = n_p - 1
    return pl.pallas_call(
        _out_proj_kernel_two,
        grid=(N_TOK // ROW_TILE,),
        in_specs=[
            pl.BlockSpec((ROW_TILE, D_MODEL), lambda i: (i, 0)),
            pl.BlockSpec((ROW_TILE, k), lambda i: (jnp.minimum(i, last_p), 0)),
            pl.BlockSpec((ROW_TILE, k), lambda i: (jnp.maximum(i - n_p, 0), 0)),
            _mod_spec(ROW_TILE),
            _resident(w.shape),
        ],
        out_specs=pl.BlockSpec((ROW_TILE, D_MODEL), lambda i: (i, 0)),
        out_shape=jax.ShapeDtypeStruct((N_TOK, D_MODEL), F32),
        compiler_params=_cparams(("arbitrary",)),
    )(x, o_prompt, o_sample, gate, w)


def _out_proj_kernel_two(x_ref, op_ref, os_ref, gate_ref, w_ref, y_ref):
    is_prompt = pl.program_id(0) < N_PROMPT // ROW_TILE

    @pl.when(is_prompt)
    def _():
        _out_proj_kernel(x_ref, op_ref, gate_ref, w_ref, y_ref)

    @pl.when(jnp.logical_not(is_prompt))
    def _():
        _out_proj_kernel(x_ref, os_ref, gate_ref, w_ref, y_ref)


def _ffn_kernel(x_ref, g_ref, sh_ref, sc_ref, gate_ref, wg_ref, wu_ref, wd_ref, y_ref, acc_ref):
    x = x_ref[...]
    hb = (_rms(x, g_ref[...]) * (1.0 + sc_ref[0]) + sh_ref[0]).astype(BF16)
    acc_ref[...] = jnp.zeros_like(acc_ref)

    def chunk(c, carry):
        a = _silu(_dot(hb, wg_ref[c])) * _dot(hb, wu_ref[c])
        acc_ref[...] += _dot(a.astype(BF16), wd_ref[c])
        return carry

    lax.fori_loop(0, N_FFN_CHUNKS, chunk, 0)
    y_ref[...] = x + gate_ref[0] * acc_ref[...]


def _ffn(x, gain, shift, scale, gate, wg, wu, wd):
    return pl.pallas_call(
        _ffn_kernel,
        grid=(N_TOK // ROW_TILE,),
        in_specs=[
            pl.BlockSpec((ROW_TILE, D_MODEL), lambda i: (i, 0)),
            pl.BlockSpec((1, D_MODEL), lambda i: (0, 0)),
            _mod_spec(ROW_TILE), _mod_spec(ROW_TILE), _mod_spec(ROW_TILE),
            _resident(wg.shape), _resident(wu.shape), _resident(wd.shape),
        ],
        out_specs=pl.BlockSpec((ROW_TILE, D_MODEL), lambda i: (i, 0)),
        out_shape=jax.ShapeDtypeStruct((N_TOK, D_MODEL), F32),
        scratch_shapes=[pltpu.VMEM((ROW_TILE, D_MODEL), F32)],
        compiler_params=_cparams(("arbitrary",)),
    )(x, gain, shift, scale, gate, wg, wu, wd)


SLAB = D_MODEL // LANES


def _store_slabs(ref, x):
    n = x.shape[0]
    for s in range(SLAB):
        ref[pl.ds(s, n, stride=SLAB), :] = x[:, s * LANES:(s + 1) * LANES]


def _load_slabs(ref, n, dtype, row0=0):
    return jnp.concatenate([ref[pl.ds(row0 + s, n, stride=SLAB), :].astype(dtype) for s in range(SLAB)], axis=1)


META_E1, META_E2, META_W1, META_W2, META_R1, META_R2 = range(6)


def _router_kernel(x_ref, g_ref, sh_ref, sc_ref, whi_ref, wlo_ref, tri_ref, h_ref, meta_ref, cnt_ref, carry_ref):
    @pl.when(pl.program_id(0) == 0)
    def _():
        carry_ref[...] = jnp.zeros_like(carry_ref)

    h = _rms(x_ref[...], g_ref[...]) * (1.0 + sc_ref[0]) + sh_ref[0]
    _store_slabs(h_ref, h)
    h_hi, h_lo = _split_bf16(h)
    logits = _dot(h_hi, whi_ref[...]) + _dot(h_lo, whi_ref[...]) + _dot(h_hi, wlo_ref[...])
    lane = lax.broadcasted_iota(jnp.int32, logits.shape, 1)
    lane_f = lane.astype(F32)
    ninf = jnp.float32(-jnp.inf)
    lg = jnp.where(lane < N_EXPERTS, logits, ninf)
    m1 = jnp.max(lg, axis=-1, keepdims=True)
    i1 = jnp.min(jnp.where(lg == m1, lane_f, float(LANES)), axis=-1, keepdims=True)
    lg2 = jnp.where(lane_f == i1, ninf, lg)
    m2 = jnp.max(lg2, axis=-1, keepdims=True)
    i2 = jnp.min(jnp.where(lg2 == m2, lane_f, float(LANES)), axis=-1, keepdims=True)
    e = jnp.exp(m2 - m1)
    w1 = 1.0 / (1.0 + e)
    w2 = e / (1.0 + e)
    sel1 = lane_f == i1
    sel2 = lane_f == i2
    onehot = jnp.where(sel1 | sel2, 1.0, 0.0)
    ranks = _dot(tri_ref[...], onehot.astype(BF16)) + carry_ref[...]
    r1 = jnp.sum(jnp.where(sel1, ranks, 0.0), axis=-1, keepdims=True)
    r2 = jnp.sum(jnp.where(sel2, ranks, 0.0), axis=-1, keepdims=True)
    carry_ref[...] += jnp.sum(onehot, axis=0, keepdims=True)
    cnt_ref[...] = carry_ref[...]
    meta = jnp.zeros(logits.shape, F32)
    for idx, val in ((META_E1, i1), (META_E2, i2), (META_W1, w1), (META_W2, w2), (META_R1, r1), (META_R2, r2)):
        meta = jnp.where(lane == idx, val, meta)
    meta_ref[...] = meta


def _router(x, gain, shift, scale, w_hi, w_lo, tri):
    return pl.pallas_call(
        _router_kernel,
        grid=(N_TOK // ROW_TILE,),
        in_specs=[
            pl.BlockSpec((ROW_TILE, D_MODEL), lambda i: (i, 0)),
            pl.BlockSpec((1, D_MODEL), lambda i: (0, 0)),
            _mod_spec(ROW_TILE), _mod_spec(ROW_TILE),
            _resident(w_hi.shape), _resident(w_lo.shape), _resident(tri.shape),
        ],
        out_specs=[
            pl.BlockSpec((ROW_TILE * SLAB, LANES), lambda i: (i, 0)),
            pl.BlockSpec((ROW_TILE, LANES), lambda i: (i, 0)),
            pl.BlockSpec((1, LANES), lambda i: (0, 0)),
        ],
        out_shape=[
            jax.ShapeDtypeStruct((N_TOK * SLAB, LANES), F32),
            jax.ShapeDtypeStruct((N_TOK, LANES), F32),
            jax.ShapeDtypeStruct((1, LANES), F32),
        ],
        scratch_shapes=[pltpu.VMEM((1, LANES), F32)],
        compiler_params=_cparams(("arbitrary",)),
    )(x, gain, shift, scale, w_hi, w_lo, tri)


def _route_plan(meta, counts):
    cnt = counts[0, :N_EXPERTS].astype(jnp.int32)
    tiles = (cnt + EXP_TILE - 1) // EXP_TILE
    tile_end = jnp.cumsum(tiles)
    tile_start = tile_end - tiles
    offs = tile_start * EXP_TILE
    experts = jnp.arange(N_EXPERTS, dtype=jnp.int32)

    def lookup(table, idx):
        return jnp.sum(jnp.where(idx[:, None] == experts[None, :], table[None, :], 0), axis=1)

    e1 = meta[:, META_E1].astype(jnp.int32)
    e2 = meta[:, META_E2].astype(jnp.int32)
    pos1 = lookup(offs, e1) + meta[:, META_R1].astype(jnp.int32)
    pos2 = lookup(offs, e2) + meta[:, META_R2].astype(jnp.int32)
    t = jnp.arange(N_EXP_TILES, dtype=jnp.int32)
    tile_expert = jnp.minimum(jnp.sum(t[:, None] >= tile_end[None, :], axis=1), N_EXPERTS - 1).astype(jnp.int32)
    in_group = (t - lookup(tile_start, tile_expert)) * EXP_TILE
    tile_rows = jnp.where(t < tile_end[-1], jnp.clip(lookup(cnt, tile_expert) - in_group, 0, EXP_TILE),
                          0).astype(jnp.int32)
    tok = jnp.arange(N_TOK, dtype=jnp.int32)
    src = jnp.zeros((N_EXP_TILES * EXP_TILE,), jnp.int32).at[pos1].set(tok).at[pos2].set(tok)
    return pos1, pos2, src, tile_expert, tile_rows


GATHER_PART = -(-EXP_TILE // N_EXP_CHUNKS)


def _expert_kernel(te_ref, tr_ref, src_ref, h_ref, wg_ref, wu_ref, wd_ref, ys_ref,
                   rows_ref, xb_ref, acc_ref, sem):
    t = pl.program_id(0)
    c = pl.program_id(1)
    n_tiles = pl.num_programs(0)
    rows = tr_ref[t]
    slot = t % 2

    def gather(tile, dst_slot, lo, hi):
        def body(r, carry):
            tok = src_ref[tile * EXP_TILE + r]
            pltpu.make_async_copy(
                h_ref.at[pl.ds(pl.multiple_of(tok * SLAB, SLAB), SLAB)],
                rows_ref.at[dst_slot, pl.ds(pl.multiple_of(r * SLAB, SLAB), SLAB)],
                sem.at[dst_slot]).start()
            return carry
        lax.fori_loop(lo, hi, body, 0)

    @pl.when((t == 0) & (c == 0) & (rows > 0))
    def _():
        gather(0, 0, 0, EXP_TILE)

    @pl.when(c == 0)
    def _():
        acc_ref[...] = jnp.zeros_like(acc_ref)

        @pl.when(rows > 0)
        def _():
            pltpu.make_async_copy(h_ref.at[pl.ds(0, EXP_TILE * SLAB)], rows_ref.at[slot], sem.at[slot]).wait()
            xb_ref[...] = _load_slabs(rows_ref.at[slot], EXP_TILE, BF16)

    nxt = jnp.minimum(t + 1, n_tiles - 1)

    @pl.when((t + 1 < n_tiles) & (tr_ref[nxt] > 0))
    def _():
        lo = c * GATHER_PART
        gather(nxt, 1 - slot, lo, jnp.minimum(lo + GATHER_PART, EXP_TILE))

    def swiglu_rows(n):
        xb = xb_ref[:n]
        a = _silu(_dot(xb, wg_ref[...].astype(BF16))) * _dot(xb, wu_ref[...].astype(BF16))
        acc_ref[:n] += _dot(a.astype(BF16), wd_ref[...].astype(BF16))

    @pl.when(rows > EXP_PART)
    def _():
        swiglu_rows(EXP_TILE)

    @pl.when((rows > 0) & (rows <= EXP_PART))
    def _():
        swiglu_rows(EXP_PART)

    @pl.when(c == N_EXP_CHUNKS - 1)
    def _():
        _store_slabs(ys_ref, acc_ref[...])


def _experts(layer, tile_expert, tile_rows, src, h_slabs, wg, wu, wd):
    def chunk_of(t, c, tr):
        return jnp.where(tr[t] > 0, c, N_EXP_CHUNKS - 1)

    return pl.pallas_call(
        _expert_kernel,
        grid_spec=pltpu.PrefetchScalarGridSpec(
            num_scalar_prefetch=3,
            grid=(N_EXP_TILES, N_EXP_CHUNKS),
            in_specs=[
                pl.BlockSpec(memory_space=pl.ANY),
                pl.BlockSpec((None, None, D_MODEL, EXP_CHUNK),
                             lambda t, c, te, tr, src: (layer, te[t], 0, chunk_of(t, c, tr))),
                pl.BlockSpec((None, None, D_MODEL, EXP_CHUNK),
                             lambda t, c, te, tr, src: (layer, te[t], 0, chunk_of(t, c, tr))),
                pl.BlockSpec((None, None, EXP_CHUNK, D_MODEL),
                             lambda t, c, te, tr, src: (layer, te[t], chunk_of(t, c, tr), 0)),
            ],
            out_specs=pl.BlockSpec((EXP_TILE * SLAB, LANES), lambda t, c, te, tr, src: (t, 0)),
            scratch_shapes=[
                pltpu.VMEM((2, EXP_TILE * SLAB, LANES), F32),
                pltpu.VMEM((EXP_TILE, D_MODEL), BF16),
                pltpu.VMEM((EXP_TILE, D_MODEL), F32),
                pltpu.SemaphoreType.DMA((2,)),
            ],
        ),
        out_shape=jax.ShapeDtypeStruct((N_EXP_TILES * EXP_TILE * SLAB, LANES), F32),
        compiler_params=_cparams(("arbitrary", "arbitrary")),
    )(tile_expert, tile_rows, src, h_slabs, wg, wu, wd)


def _combine_kernel(pos1_ref, pos2_ref, x_ref, gate_ref, meta_ref, ys_ref, y_ref, buf_ref, sem):
    base = pl.program_id(0) * COMBINE_TILE
    n = COMBINE_TILE

    def row_copy(src, slot):
        return pltpu.make_async_copy(
            ys_ref.at[pl.ds(pl.multiple_of(src * SLAB, SLAB), SLAB)],
            buf_ref.at[pl.ds(pl.multiple_of(slot * SLAB, SLAB), SLAB)], sem)

    def issue(t, carry):
        row_copy(pos1_ref[base + t], t).start()
        row_copy(pos2_ref[base + t], n + t).start()
        return carry

    lax.fori_loop(0, n, issue, 0)

    def drain(t, carry):
        row_copy(0, 0).wait()
        row_copy(0, 0).wait()
        return carry

    lax.fori_loop(0, n, drain, 0)

    meta = meta_ref[...]
    w1 = meta[:, META_W1:META_W1 + 1]
    w2 = meta[:, META_W2:META_W2 + 1]
    y1 = _load_slabs(buf_ref, n, F32)
    y2 = _load_slabs(buf_ref, n, F32, row0=n * SLAB)
    y_ref[...] = x_ref[...] + gate_ref[0] * (w1 * y1 + w2 * y2)


def _combine(pos1, pos2, x, gate, meta, ys):
    return pl.pallas_call(
        _combine_kernel,
        grid_spec=pltpu.PrefetchScalarGridSpec(
            num_scalar_prefetch=2,
            grid=(N_TOK // COMBINE_TILE,),
            in_specs=[
                pl.BlockSpec((COMBINE_TILE, D_MODEL), lambda i, p1, p2: (i, 0)),
                pl.BlockSpec((1, 1, D_MODEL), lambda i, p1, p2: (_cond_of_row(i * COMBINE_TILE), 0, 0)),
                pl.BlockSpec((COMBINE_TILE, LANES), lambda i, p1, p2: (i, 0)),
                pl.BlockSpec(memory_space=pl.ANY),
            ],
            out_specs=pl.BlockSpec((COMBINE_TILE, D_MODEL), lambda i, p1, p2: (i, 0)),
            scratch_shapes=[pltpu.VMEM((2 * COMBINE_TILE * SLAB, LANES), F32), pltpu.SemaphoreType.DMA(())],
        ),
        out_shape=jax.ShapeDtypeStruct((N_TOK, D_MODEL), F32),
        compiler_params=_cparams(("arbitrary",)),
    )(pos1, pos2, x, gate, meta, ys)


def _moe(layer, x, gain, shift, scale, gate, router_w, wg, wu, wd, tri):
    w = jnp.pad(router_w, ((0, 0), (0, LANES - N_EXPERTS)))
    w_hi = w.astype(BF16)
    w_lo = (w - w_hi.astype(F32)).astype(BF16)
    h_slabs, meta, counts = _router(x, gain, shift, scale, w_hi, w_lo, tri)
    pos1, pos2, src, tile_expert, tile_rows = _route_plan(meta, counts)
    ys = _experts(layer, tile_expert, tile_rows, src, h_slabs, wg, wu, wd)
    return _combine(pos1, pos2, x, gate, meta, ys)


def _final_norm_kernel(x_ref, g_ref, y_ref):
    y_ref[...] = _rms(x_ref[...], g_ref[...])


def _final_norm(x, gain, row0, rows):
    off = row0 // ROW_TILE
    return pl.pallas_call(
        _final_norm_kernel,
        grid=(rows // ROW_TILE,),
        in_specs=[
            pl.BlockSpec((ROW_TILE, D_MODEL), lambda i: (off + i, 0)),
            pl.BlockSpec((1, D_MODEL), lambda i: (0, 0)),
        ],
        out_specs=pl.BlockSpec((ROW_TILE, D_MODEL), lambda i: (i, 0)),
        out_shape=jax.ShapeDtypeStruct((rows, D_MODEL), F32),
        compiler_params=_cparams(("arbitrary",)),
    )(x, gain)


def _rope_tables(rot_dim, reps):
    n_freq = rot_dim // 4
    inv_freq = ROPE_THETA ** (-np.arange(n_freq, dtype=np.float64) / n_freq)
    t = np.arange(DEC_SEQ)
    ang_r = (t // GRID_W)[:, None] * inv_freq
    ang_c = (t % GRID_W)[:, None] * inv_freq
    cos = np.concatenate([np.cos(ang_r), np.cos(ang_r), np.cos(ang_c), np.cos(ang_c)], axis=1)
    sin = np.concatenate([-np.sin(ang_r), np.sin(ang_r), -np.sin(ang_c), np.sin(ang_c)], axis=1)
    pad = LANES - reps * rot_dim
    cos = np.concatenate([cos] * reps + [np.ones((DEC_SEQ, pad))], axis=1)
    sin = np.concatenate([sin] * reps + [np.zeros((DEC_SEQ, pad))], axis=1)
    cos = np.concatenate([cos, np.ones((ROW_TILE, LANES))], axis=0)
    sin = np.concatenate([sin, np.zeros((ROW_TILE, LANES))], axis=0)
    return jnp.asarray(cos, F32), jnp.asarray(sin, F32)


def _swap_halves(w, rot_dim):
    k, n = w.shape
    q = rot_dim // 4
    return w.reshape(k, n // (2 * q), 2, q)[:, :, ::-1, :].reshape(k, n)


def _gqa_arrange(w_q, axis):
    pairs = GQA_KV_HEADS // 2
    shape = w_q.shape
    split = shape[:axis] + (pairs, 2, GQA_GROUP, GQA_HEAD_DIM) + shape[axis + 1:]
    order = list(range(len(split)))
    order[axis + 1], order[axis + 2] = order[axis + 2], order[axis + 1]
    return w_q.reshape(split).transpose(order).reshape(shape)


def _mla_layouts(wq_b, wkv_a, wkv_b):
    hd = MLA_NOPE + MLA_ROPE
    w3 = wq_b.reshape(MLA_Q_LORA, MLA_HEADS, hd)
    nope = w3[:, :, :MLA_NOPE].reshape(MLA_Q_LORA, MLA_HEADS * MLA_NOPE)
    rope = w3[:, :, MLA_NOPE:].reshape(MLA_Q_LORA, MLA_HEADS * MLA_ROPE)

    def rope_blocks(r):
        r = r.reshape(-1, MLA_PAIRS, 2 * MLA_ROPE)
        return jnp.pad(r, ((0, 0), (0, 0), (0, LANES - 2 * MLA_ROPE))).reshape(-1, MLA_PAIRS * LANES)

    wb = jnp.concatenate([nope, rope_blocks(rope)], axis=1)
    wbs = rope_blocks(_swap_halves(rope, MLA_ROPE))
    kpe_w = wkv_a[:, MLA_KV_LORA:]
    kpe_ws = _swap_halves(kpe_w, MLA_ROPE)
    zeros_a = jnp.zeros((D_MODEL, LANES - 2 * MLA_ROPE), F32)
    wa_tail = jnp.concatenate([wkv_a[:, :MLA_KV_LORA], kpe_w, kpe_w, zeros_a, kpe_ws, kpe_ws, zeros_a], axis=1)
    kv3 = wkv_b.reshape(MLA_KV_LORA, MLA_HEADS, MLA_NOPE + MLA_V)
    wkvb = jnp.concatenate([kv3[:, :, :MLA_NOPE].reshape(MLA_KV_LORA, -1),
                            kv3[:, :, MLA_NOPE:].reshape(MLA_KV_LORA, -1)], axis=1)
    return wb, wbs, wa_tail, wkvb


def _chunk_cols(w, chunk):
    k, n = w.shape
    return w.reshape(k, n // chunk, chunk).transpose(1, 0, 2).astype(BF16)


def kernel(x_prompt, x_sample, cache_l0_k, cache_l0_v, cache_l1_ckv, cache_l1_kpe, cache_l2_k, cache_l2_v, cache_l3_k, cache_l3_v, c, c_ctx, ada_w, ada_b, norm_mix, norm_ffn, norm_final, na_w_qkv, na_w_o, na_rel_bias, mla_wq_a, mla_q_norm, mla_wq_b, mla_wkv_a, mla_kv_norm, mla_wkv_b, mla_w_o, gqa_w_qkv, gqa_w_o, gqa_sink, ffn_w_gate, ffn_w_up, ffn_w_down, moe_router, moe_w_gate, moe_w_up, moe_w_down):
    x = jnp.concatenate([x_prompt.reshape(N_PROMPT, D_MODEL), x_sample.reshape(N_SAMPLE, D_MODEL)], axis=0)
    cond = jnp.concatenate([c_ctx[None, :], c, jnp.zeros((N_COND - 1 - DEC_BATCH, D_MODEL), F32)], axis=0)
    mods = _adaln(cond, ada_w, ada_b).reshape(DEPTH, N_COND, 6, 1, D_MODEL)
    tri = jnp.asarray(np.tril(np.ones((ROW_TILE, ROW_TILE), np.float32), -1), BF16)
    no_sink = jnp.zeros((GQA_HEADS,), F32)
    na_caches = {0: (cache_l0_k, cache_l0_v), 3: (cache_l3_k, cache_l3_v)}
    state = []

    for i in range(DEPTH):
        sh_m, sc_m, g_m, sh_f, sc_f, g_f = [mods[i, :, j] for j in range(6)]
        gain_m = norm_mix[i][None, :]
        gain_f = norm_ffn[i][None, :]
        kind, s = i % 3, i // 3
        if kind == 0:
            qw = NA_HEADS * NA_HEAD_DIM
            qkv, k_rows, v_rows = _norm_proj(x, gain_m, sh_m, sc_m, na_w_qkv[s].astype(BF16), qw)
            o_p = _ctx_attn(qkv, no_sink, qw, False)
            ck, cv = na_caches[i]
            o_s = _na_latent_attn(qkv, ck.reshape(DEC_BATCH, PAST_LEN, qw), cv.reshape(DEC_BATCH, PAST_LEN, qw),
                                  _na_pair_table(na_rel_bias[s]))
            w_o = na_w_o[s].astype(BF16)
            kv_shape = (BATCH, SEQ, NA_HEADS, NA_HEAD_DIM)
            state += [k_rows.reshape(kv_shape), v_rows.reshape(kv_shape)]
        elif kind == 1:
            wb, wbs, wa_tail, wkvb = _mla_layouts(mla_wq_b[s], mla_wkv_a[s], mla_wkv_b[s])
            wa = jnp.concatenate([mla_wq_a[s], wa_tail], axis=1).astype(BF16)
            cos, sin = _rope_tables(MLA_ROPE, 2)
            q, ckv, kpe = _mla_proj(x, gain_m, sh_m, sc_m, wa, mla_q_norm[s][None, :], mla_kv_norm[s][None, :],
                                    wb.astype(BF16), wbs.astype(BF16), cos, sin)
            wkvb = wkvb.astype(BF16)
            kv_p = _rows_matmul(ckv[:N_PROMPT], wkvb, BF16)
            o_p = _mla_ctx_attn(q, kv_p, kpe)
            ckv_s = ckv[N_PROMPT:].reshape(DEC_BATCH, DEC_SEQ, MLA_KV_LORA)
            ckv_all = jnp.concatenate([cache_l1_ckv, ckv_s], axis=1).reshape(DEC_BATCH * MLA_KEYS, MLA_KV_LORA)
            kv_s = _rows_matmul(ckv_all, wkvb, BF16)
            kpe_cache = jnp.concatenate(
                [cache_l1_kpe, cache_l1_kpe, jnp.zeros((DEC_BATCH, PAST_LEN, LANES - 2 * MLA_ROPE), F32)], axis=2)
            kpe_all = jnp.concatenate([kpe_cache, kpe[N_PROMPT:].reshape(DEC_BATCH, DEC_SEQ, LANES)], axis=1)
            o_s = _mla_latent_attn(q, kv_s, kpe_all.reshape(DEC_BATCH * MLA_KEYS, LANES))
            w_o = mla_w_o[s].astype(BF16)
            state += [ckv[:N_PROMPT].reshape(BATCH, SEQ, MLA_KV_LORA),
                      kpe[:N_PROMPT, :MLA_ROPE].reshape(BATCH, SEQ, MLA_ROPE)]
        else:
            qw = GQA_HEADS * GQA_HEAD_DIM
            kw = GQA_KV_HEADS * GQA_HEAD_DIM
            w_qkv = jnp.concatenate([_gqa_arrange(gqa_w_qkv[s][:, :qw], 1), gqa_w_qkv[s][:, qw:]], axis=1)
            w_swap = _swap_halves(w_qkv[:, :qw + kw], GQA_HEAD_DIM)
            cos, sin = _rope_tables(GQA_HEAD_DIM, 2)
            qkv, k_rows, v_rows = _norm_proj(x, gain_m, sh_m, sc_m, w_qkv.astype(BF16), kw,
                                             rope=(w_swap.astype(BF16), cos, sin))
            sink = gqa_sink[s]
            o_p = _ctx_attn(qkv, sink, kw, True)
            o_s = _gqa_latent_attn(qkv, sink, cache_l2_k.reshape(DEC_BATCH, PAST_LEN, kw),
                                   cache_l2_v.reshape(DEC_BATCH, PAST_LEN, kw))
            w_o = _gqa_arrange(gqa_w_o[s], 0).astype(BF16)
            kv_shape = (BATCH, SEQ, GQA_KV_HEADS, GQA_HEAD_DIM)
            state += [k_rows.reshape(kv_shape), v_rows.reshape(kv_shape)]
        x = _out_proj(x, o_p, o_s, g_m, w_o)

        if i % 2 == 0:
            s = i // 2
            x = _ffn(x, gain_f, sh_f, sc_f, g_f, _chunk_cols(ffn_w_gate[s], FFN_CHUNK),
                     _chunk_cols(ffn_w_up[s], FFN_CHUNK),
                     ffn_w_down[s].reshape(N_FFN_CHUNKS, FFN_CHUNK, D_MODEL).astype(BF16))
        else:
            s = i // 2
            x = _moe(s, x, gain_f, sh_f, sc_f, g_f, moe_router[s], moe_w_gate, moe_w_up, moe_w_down, tri)

    y_prompt = _final_norm(x, norm_final[None, :], 0, N_PROMPT).reshape(BATCH, SEQ, D_MODEL)
    y_sample = _final_norm(x, norm_final[None, :], N_PROMPT, N_SAMPLE).reshape(DEC_BATCH, DEC_SEQ, D_MODEL)
    return (y_prompt, y_sample, *state)
```

```python
import functools
import math

import numpy as np
import jax
import jax.numpy as jnp
from jax import lax
from jax.experimental import pallas as pl
from jax.experimental.pallas import tpu as pltpu

F32 = jnp.float32
BF16 = jnp.bfloat16

D_MODEL = 1024
BATCH = 16
SEQ = 256
DEPTH = 4
DEC_BATCH = 4
DEC_SEQ = 2048
PAST_LEN = 512
GRID_W = 64
GRID_ROWS = DEC_SEQ // GRID_W
NORM_EPS = 1e-6
NEG_INF = -1e30
ROPE_THETA = 10000.0
NA_HEADS = 16
NA_HEAD_DIM = 64
NA_WIN_ROWS = 8
NA_WIN_COLS = 16
MLA_HEADS = 16
MLA_Q_LORA = 512
MLA_KV_LORA = 256
MLA_NOPE = 64
MLA_ROPE = 32
MLA_V = 64
GQA_HEADS = 16
GQA_KV_HEADS = 4
GQA_HEAD_DIM = 64
GQA_WINDOW = 128
FFN_DIM = 2816
N_EXPERTS = 8
EXPERT_DIM = 3584

N_PROMPT = BATCH * SEQ
N_SAMPLE = DEC_BATCH * DEC_SEQ
N_TOK = N_PROMPT + N_SAMPLE
N_COND = 8
LOG2E = math.log2(math.e)
HEAD_SCALE = NA_HEAD_DIM ** -0.5 * LOG2E
MLA_SCALE = (MLA_NOPE + MLA_ROPE) ** -0.5 * LOG2E

LANES = 128
SUBLANES = 8
VMEM_LIMIT = 56 * 1024 * 1024

ROW_TILE = 512
FFN_CHUNK = 256
N_FFN_CHUNKS = FFN_DIM // FFN_CHUNK
EXP_TILE = 1024
EXP_PART = 512
EXP_CHUNK = 512
N_EXP_CHUNKS = EXPERT_DIM // EXP_CHUNK
N_EXP_TILES = (2 * N_TOK) // EXP_TILE + N_EXPERTS
COMBINE_TILE = 256
NA_QROWS = 4
NA_UNION = NA_QROWS + NA_WIN_ROWS
NA_PAIRS_PER_STEP = 4
MLA_KEYS = PAST_LEN + DEC_SEQ


def _cparams(sem):
    return pltpu.CompilerParams(dimension_semantics=sem, vmem_limit_bytes=VMEM_LIMIT)


def _cond_of_row(row):
    return jnp.where(row < N_PROMPT, 0, 1 + (row - N_PROMPT) // DEC_SEQ)


def _nt_dot(a, b):
    return lax.dot_general(a, b, (((1,), (1,)), ((), ())), preferred_element_type=F32)


def _dot(a, b):
    return jnp.dot(a, b, preferred_element_type=F32)


def _rms(x, gain):
    return x * lax.rsqrt(jnp.mean(x * x, axis=-1, keepdims=True) + NORM_EPS) * gain


def _silu(x):
    return x / (1.0 + jnp.exp(-x))


def _split_bf16(x):
    hi = x.astype(BF16)
    lo = (x - hi.astype(F32)).astype(BF16)
    return hi, lo


def _resident(shape):
    nd = len(shape)
    return pl.BlockSpec(shape, lambda *_: (0,) * nd, pipeline_mode=pl.Buffered(1))


def _mod_spec(tile):
    return pl.BlockSpec((1, 1, D_MODEL), lambda i: (_cond_of_row(i * tile), 0, 0))


ADA_TILE = 1536


def _adaln_kernel(c_ref, w_ref, b_ref, o_ref):
    a_hi, a_lo = _split_bf16(_silu(c_ref[...]))
    w_hi, w_lo = _split_bf16(w_ref[0])
    o_ref[0] = _dot(a_hi, w_hi) + _dot(a_lo, w_hi) + _dot(a_hi, w_lo) + b_ref[0]


def _adaln(cond, ada_w, ada_b):
    n = 6 * D_MODEL
    return pl.pallas_call(
        _adaln_kernel,
        grid=(DEPTH, n // ADA_TILE),
        in_specs=[
            pl.BlockSpec((N_COND, D_MODEL), lambda l, j: (0, 0)),
            pl.BlockSpec((1, D_MODEL, ADA_TILE), lambda l, j: (l, 0, j)),
            pl.BlockSpec((1, 1, ADA_TILE), lambda l, j: (l, 0, j)),
        ],
        out_specs=pl.BlockSpec((1, N_COND, ADA_TILE), lambda l, j: (l, 0, j)),
        out_shape=jax.ShapeDtypeStruct((DEPTH, N_COND, n), F32),
        compiler_params=_cparams(("arbitrary", "arbitrary")),
    )(cond, ada_w, ada_b.reshape(DEPTH, 1, n))


PROJ_COLS = 512


def _norm_proj_kernel(x_ref, g_ref, sh_ref, sc_ref, w_ref, *rest, n_out, n_rope, kv_cols):
    rest = list(rest)
    if n_rope:
        w2_ref, cos_ref, sin_ref = rest[:3]
        rest = rest[3:]
    o_ref, k_ref, v_ref = rest
    h = _rms(x_ref[...], g_ref[...]) * (1.0 + sc_ref[0]) + sh_ref[0]
    hb = h.astype(BF16)
    is_prompt = pl.program_id(0) < N_PROMPT // ROW_TILE
    cuts = sorted({0, n_out, n_rope, n_out - 2 * kv_cols, n_out - kv_cols}
                  | set(range(0, n_out, PROJ_COLS)))
    for c0, c1 in zip(cuts[:-1], cuts[1:]):
        cols = slice(c0, c1)
        y = _dot(hb, w_ref[:, cols])
        if c0 < n_rope:
            ys = _dot(hb, w2_ref[:, cols])
            reps = (c1 - c0) // LANES
            cos = jnp.concatenate([cos_ref[...]] * reps, axis=1)
            sin = jnp.concatenate([sin_ref[...]] * reps, axis=1)
            y = y * cos + ys * sin
        o_ref[:, cols] = y
        for kv_ref, start in ((k_ref, n_out - 2 * kv_cols), (v_ref, n_out - kv_cols)):
            if start <= c0 < start + kv_cols:
                @pl.when(is_prompt)
                def _(kv_ref=kv_ref, start=start, y=y, c0=c0, c1=c1):
                    kv_ref[:, c0 - start:c1 - start] = y


def _rope_block(i):
    row = i * ROW_TILE
    return jnp.where(row < N_PROMPT, DEC_SEQ // ROW_TILE, ((row - N_PROMPT) % DEC_SEQ) // ROW_TILE)


def _norm_proj(x, gain, shift, scale, w, kv_cols, rope=None):
    n_out = w.shape[1]
    last_prompt = N_PROMPT // ROW_TILE - 1
    kv_spec = pl.BlockSpec((ROW_TILE, kv_cols), lambda i: (jnp.minimum(i, last_prompt), 0))
    kv_shape = jax.ShapeDtypeStruct((N_PROMPT, kv_cols), F32)
    in_specs = [
        pl.BlockSpec((ROW_TILE, D_MODEL), lambda i: (i, 0)),
        pl.BlockSpec((1, D_MODEL), lambda i: (0, 0)),
        _mod_spec(ROW_TILE),
        _mod_spec(ROW_TILE),
        _resident(w.shape),
    ]
    args = [x, gain, shift, scale, w]
    n_rope = 0
    if rope is not None:
        w2, cos, sin = rope
        n_rope = w2.shape[1]
        in_specs += [
            _resident(w2.shape),
            pl.BlockSpec((ROW_TILE, LANES), lambda i: (_rope_block(i), 0)),
            pl.BlockSpec((ROW_TILE, LANES), lambda i: (_rope_block(i), 0)),
        ]
        args += [w2, cos, sin]
    return pl.pallas_call(
        functools.partial(_norm_proj_kernel, n_out=n_out, n_rope=n_rope, kv_cols=kv_cols),
        grid=(N_TOK // ROW_TILE,),
        in_specs=in_specs,
        out_specs=[pl.BlockSpec((ROW_TILE, n_out), lambda i: (i, 0)), kv_spec, kv_spec],
        out_shape=[jax.ShapeDtypeStruct((N_TOK, n_out), F32), kv_shape, kv_shape],
        compiler_params=_cparams(("arbitrary",)),
    )(*args)


MLA_Q_COLS = 2 * MLA_HEADS * MLA_NOPE


def _mla_proj_kernel(x_ref, g_ref, sh_ref, sc_ref, wa_ref, qn_ref, kvn_ref, wb_ref, wbs_ref,
                     cos_ref, sin_ref, q_ref, ckv_ref, kpe_ref):
    h = _rms(x_ref[...], g_ref[...]) * (1.0 + sc_ref[0]) + sh_ref[0]
    a = _dot(h.astype(BF16), wa_ref[...])
    c0 = MLA_Q_LORA
    c1 = c0 + MLA_KV_LORA
    cos = cos_ref[...]
    sin = sin_ref[...]
    ckv_ref[...] = _rms(a[:, c0:c1], kvn_ref[...])
    kpe_ref[...] = a[:, c1:c1 + LANES] * cos + a[:, c1 + LANES:c1 + 2 * LANES] * sin
    qn = _rms(a[:, :c0], qn_ref[...]).astype(BF16)
    half = MLA_Q_COLS // 2
    q_scale = MLA_SCALE
    q_ref[:, :half] = (_dot(qn, wb_ref[:, :half]) * q_scale).astype(BF16)
    q_rope = _dot(qn, wb_ref[:, half:])
    q_swap = _dot(qn, wbs_ref[...])
    reps = half // LANES
    cos_r = jnp.concatenate([cos] * reps, axis=1) * q_scale
    sin_r = jnp.concatenate([sin] * reps, axis=1) * q_scale
    q_ref[:, half:] = (q_rope * cos_r + q_swap * sin_r).astype(BF16)


def _mla_proj(x, gain, shift, scale, wa, q_norm, kv_norm, wb, wbs, cos, sin):
    return pl.pallas_call(
        _mla_proj_kernel,
        grid=(N_TOK // ROW_TILE,),
        in_specs=[
            pl.BlockSpec((ROW_TILE, D_MODEL), lambda i: (i, 0)),
            pl.BlockSpec((1, D_MODEL), lambda i: (0, 0)),
            _mod_spec(ROW_TILE),
            _mod_spec(ROW_TILE),
            _resident(wa.shape),
            pl.BlockSpec((1, MLA_Q_LORA), lambda i: (0, 0)),
            pl.BlockSpec((1, MLA_KV_LORA), lambda i: (0, 0)),
            _resident(wb.shape),
            _resident(wbs.shape),
            pl.BlockSpec((ROW_TILE, LANES), lambda i: (_rope_block(i), 0)),
            pl.BlockSpec((ROW_TILE, LANES), lambda i: (_rope_block(i), 0)),
        ],
        out_specs=[
            pl.BlockSpec((ROW_TILE, MLA_Q_COLS), lambda i: (i, 0)),
            pl.BlockSpec((ROW_TILE, MLA_KV_LORA), lambda i: (i, 0)),
            pl.BlockSpec((ROW_TILE, LANES), lambda i: (i, 0)),
        ],
        out_shape=[
            jax.ShapeDtypeStruct((N_TOK, MLA_Q_COLS), BF16),
            jax.ShapeDtypeStruct((N_TOK, MLA_KV_LORA), F32),
            jax.ShapeDtypeStruct((N_TOK, LANES), F32),
        ],
        compiler_params=_cparams(("arbitrary",)),
    )(x, gain, shift, scale, wa, q_norm, kv_norm, wb, wbs, cos, sin)


def _rows_matmul_kernel(x_ref, w_ref, o_ref):
    o_ref[...] = _dot(x_ref[...].astype(BF16), w_ref[...]).astype(o_ref.dtype)


def _rows_matmul(x, w, out_dtype):
    rows, k = x.shape
    n = w.shape[1]
    return pl.pallas_call(
        _rows_matmul_kernel,
        grid=(rows // ROW_TILE,),
        in_specs=[pl.BlockSpec((ROW_TILE, k), lambda i: (i, 0)), _resident(w.shape)],
        out_specs=pl.BlockSpec((ROW_TILE, n), lambda i: (i, 0)),
        out_shape=jax.ShapeDtypeStruct((rows, n), out_dtype),
        compiler_params=_cparams(("arbitrary",)),
    )(x, w)


def _pair_attention(q, masks, keys, values, biases, q_scale, sinks):
    half = LANES // 2
    lane = lax.broadcasted_iota(jnp.int32, (1, LANES), 1)
    outs = []
    sink_terms = []
    for h in range(2):
        qh = jnp.where(masks[h], q, jnp.zeros_like(q))
        if q_scale != 1.0:
            qh = qh * q_scale
        qh = qh.astype(BF16)
        scores = []
        for k, bias in zip(keys, biases):
            s = _nt_dot(qh, k)
            if bias is not None:
                s = s + bias(h)
            scores.append(s)
        m = functools.reduce(jnp.maximum, [jnp.max(s, axis=-1, keepdims=True) for s in scores])
        if sinks is not None:
            m = jnp.maximum(m, sinks[h])
        own = (lane < half) if h == 0 else (lane >= half)
        acc = None
        for s, v in zip(scores, values):
            part = _dot(jnp.exp2(s - m).astype(BF16), jnp.where(own, v, jnp.ones_like(v)))
            acc = part if acc is None else acc + part
        outs.append(acc)
        if sinks is not None:
            sink_terms.append(jnp.exp2(sinks[h] - m))
    first = lane < half
    numer = jnp.where(first, outs[0], outs[1])
    denom = pltpu.roll(jnp.where(first, outs[1], outs[0]), half, axis=1)
    if sinks is not None:
        denom = denom + jnp.where(first, sink_terms[0], sink_terms[1])
    return (numer / denom).astype(BF16)


def _half_masks(width=LANES):
    lane = lax.broadcasted_iota(jnp.int32, (1, width), 1)
    return [lane < LANES // 2, lane >= LANES // 2]


def _ctx_attn_kernel(sink_ref, q_ref, k_ref, v_ref, o_ref, *, kv_blocks, use_sink):
    masks = _half_masks()
    n_q_blocks = o_ref.shape[1] // LANES
    for qb in range(n_q_blocks):
        kb = qb * kv_blocks // n_q_blocks
        cols = slice(kb * LANES, (kb + 1) * LANES)
        sinks = None
        if use_sink:
            base = (qb // 4) * 8 + qb % 4
            sinks = [sink_ref[base] * LOG2E, sink_ref[base + 4] * LOG2E]
        o_ref[:, qb * LANES:(qb + 1) * LANES] = _pair_attention(
            q_ref[:, qb * LANES:(qb + 1) * LANES], masks,
            [k_ref[:, cols].astype(BF16)], [v_ref[:, cols].astype(BF16)], [None],
            HEAD_SCALE, sinks)


def _ctx_attn(qkv, sink, kv_width, use_sink):
    qw = NA_HEADS * NA_HEAD_DIM
    k_blk = qw // kv_width
    return pl.pallas_call(
        functools.partial(_ctx_attn_kernel, kv_blocks=kv_width // LANES, use_sink=use_sink),
        grid_spec=pltpu.PrefetchScalarGridSpec(
            num_scalar_prefetch=1,
            grid=(BATCH,),
            in_specs=[
                pl.BlockSpec((SEQ, qw), lambda b, s: (b, 0)),
                pl.BlockSpec((SEQ, kv_width), lambda b, s: (b, k_blk)),
                pl.BlockSpec((SEQ, kv_width), lambda b, s: (b, k_blk + 1)),
            ],
            out_specs=pl.BlockSpec((SEQ, qw), lambda b, s: (b, 0)),
        ),
        out_shape=jax.ShapeDtypeStruct((N_PROMPT, qw), BF16),
        compiler_params=_cparams(("arbitrary",)),
    )(sink, qkv, qkv, qkv)


def _na_union_start(rb):
    return jnp.clip(rb * NA_QROWS - NA_WIN_ROWS // 2, 0, GRID_ROWS - NA_UNION)


NA_DR_SLOTS = 2 * NA_WIN_ROWS


def _na_bias(pairs_ref, h, rb):
    u0 = _na_union_start(rb)
    left = lax.broadcasted_iota(jnp.int32, (1, LANES), 1) < GRID_W
    rows = []
    for j in range(NA_QROWS):
        r = rb * NA_QROWS + j
        start = jnp.clip(r - NA_WIN_ROWS // 2, 0, GRID_ROWS - NA_WIN_ROWS)
        blocks = []
        for ip in range(NA_UNION // 2):
            key_row = u0 + 2 * ip
            dr = key_row - r + NA_WIN_ROWS - 1
            ok_l = ((key_row >= start) & (key_row < start + NA_WIN_ROWS)).astype(jnp.int32)
            ok_r = ((key_row + 1 >= start) & (key_row + 1 < start + NA_WIN_ROWS)).astype(jnp.int32)
            blk = pairs_ref[h, jnp.clip(dr + 1, 0, NA_DR_SLOTS - 1)]
            blocks.append(jnp.where(jnp.where(left, ok_l, ok_r) > 0, blk, NEG_INF))
        rows.append(jnp.concatenate(blocks, axis=1))
    return jnp.concatenate(rows, axis=0)


def _na_latent_kernel(q_ref, k_ref, v_ref, kc_ref, vc_ref, pairs_ref, o_ref):
    rb = pl.program_id(2)
    start = pl.multiple_of(_na_union_start(rb) * GRID_W, GRID_W)
    n_loc = NA_UNION * GRID_W
    for p in range(NA_PAIRS_PER_STEP):
        cols = slice(p * LANES, (p + 1) * LANES)
        k_loc = k_ref[pl.ds(start, n_loc), cols].astype(BF16)
        v_loc = v_ref[pl.ds(start, n_loc), cols].astype(BF16)
        o_ref[:, cols] = _pair_attention(
            q_ref[:, cols], _half_masks(),
            [k_loc, kc_ref[0, :, cols].astype(BF16)], [v_loc, vc_ref[0, :, cols].astype(BF16)],
            [lambda h, p=p: _na_bias(pairs_ref, 2 * p + h, rb), None], HEAD_SCALE, None)


def _na_latent_attn(qkv, k_ctx, v_ctx, pairs):
    qw = NA_HEADS * NA_HEAD_DIM
    width = NA_PAIRS_PER_STEP * LANES
    n_groups = qw // width
    q_rows = NA_QROWS * GRID_W
    n_rb = DEC_SEQ // q_rows
    q_off = N_PROMPT // q_rows
    kv_off = N_PROMPT // DEC_SEQ
    return pl.pallas_call(
        _na_latent_kernel,
        grid=(DEC_BATCH, n_groups, n_rb),
        in_specs=[
            pl.BlockSpec((q_rows, width), lambda b, j, r: (q_off + b * n_rb + r, j)),
            pl.BlockSpec((DEC_SEQ, width), lambda b, j, r: (kv_off + b, n_groups + j)),
            pl.BlockSpec((DEC_SEQ, width), lambda b, j, r: (kv_off + b, 2 * n_groups + j)),
            pl.BlockSpec((1, PAST_LEN, width), lambda b, j, r: (b, 0, j)),
            pl.BlockSpec((1, PAST_LEN, width), lambda b, j, r: (b, 0, j)),
            pl.BlockSpec((2 * NA_PAIRS_PER_STEP, NA_DR_SLOTS, GRID_W, LANES), lambda b, j, r: (j, 0, 0, 0)),
        ],
        out_specs=pl.BlockSpec((q_rows, width), lambda b, j, r: (b * n_rb + r, j)),
        out_shape=jax.ShapeDtypeStruct((N_SAMPLE, qw), BF16),
        compiler_params=_cparams(("arbitrary", "arbitrary", "arbitrary")),
    )(qkv, qkv, qkv, k_ctx, v_ctx, pairs)


def _na_pair_table(rel_bias):
    cols = np.arange(GRID_W)
    col_start = np.clip(cols - NA_WIN_COLS // 2, 0, GRID_W - NA_WIN_COLS)
    col_ok = (cols[None, :] >= col_start[:, None]) & (cols[None, :] < col_start[:, None] + NA_WIN_COLS)
    dc = np.clip(cols[None, :] - cols[:, None] + NA_WIN_COLS - 1, 0, 2 * NA_WIN_COLS - 2)
    n_dc = 2 * NA_WIN_COLS - 1
    n_dr = 2 * NA_WIN_ROWS - 1
    pick = np.zeros((n_dc, GRID_W * GRID_W), np.float32)
    pick[dc.reshape(-1), np.arange(GRID_W * GRID_W)] = 1.0
    flat = jnp.dot(rel_bias.astype(F32).reshape(NA_HEADS * n_dr, n_dc), jnp.asarray(pick),
                   precision=lax.Precision.HIGHEST)
    bias = jnp.where(col_ok.reshape(1, 1, -1), flat.reshape(NA_HEADS, n_dr, -1) * LOG2E, NEG_INF)
    bias = bias.reshape(NA_HEADS, n_dr, GRID_W, GRID_W)
    masked = jnp.full((NA_HEADS, 1, GRID_W, GRID_W), NEG_INF, F32)
    ext = jnp.concatenate([masked, bias, masked], axis=1)
    return jnp.concatenate([ext[:, :-1], ext[:, 1:]], axis=-1)


GQA_Q_BLOCK = GQA_WINDOW
GQA_GROUP = GQA_HEADS // GQA_KV_HEADS


def _gqa_latent_kernel(sink_ref, q_ref, kp_ref, kc_ref, kn_ref, vp_ref, vc_ref, vn_ref,
                       kx_ref, vx_ref, o_ref):
    pair = pl.program_id(1)
    qb = pl.program_id(2)
    n_qb = pl.num_programs(2)
    blk = GQA_Q_BLOCK
    rows = GQA_GROUP * blk
    qq = lax.broadcasted_iota(jnp.int32, (rows, blk), 0) % blk
    kk = lax.broadcasted_iota(jnp.int32, (rows, blk), 1)
    neg = jnp.float32(NEG_INF)
    bias_prev = jnp.where((kk >= qq) & (qb > 0), 0.0, neg)
    bias_next = jnp.where((kk <= qq) & (qb < n_qb - 1), 0.0, neg)
    row_grp = lax.broadcasted_iota(jnp.int32, (rows, 1), 0) // blk
    q = jnp.concatenate([q_ref[:, m * LANES:(m + 1) * LANES] for m in range(GQA_GROUP)], axis=0)
    sinks = []
    for h in range(2):
        base = pair * 2 * GQA_GROUP + h * GQA_GROUP
        col = jnp.zeros((rows, 1), F32)
        for m in range(GQA_GROUP):
            col = jnp.where(row_grp == m, sink_ref[base + m] * LOG2E, col)
        sinks.append(col)
    o = _pair_attention(
        q, _half_masks(),
        [kp_ref[...].astype(BF16), kc_ref[...].astype(BF16), kn_ref[...].astype(BF16), kx_ref[0].astype(BF16)],
        [vp_ref[...].astype(BF16), vc_ref[...].astype(BF16), vn_ref[...].astype(BF16), vx_ref[0].astype(BF16)],
        [lambda h: bias_prev, None, lambda h: bias_next, None], HEAD_SCALE, sinks)
    for m in range(GQA_GROUP):
        o_ref[:, m * LANES:(m + 1) * LANES] = o[m * blk:(m + 1) * blk]


def _gqa_latent_attn(qkv, sink, k_ctx, v_ctx):
    qw = GQA_HEADS * GQA_HEAD_DIM
    blk = GQA_Q_BLOCK
    n_qb = DEC_SEQ // blk
    q_off = N_PROMPT // blk
    q_cols = GQA_GROUP * LANES
    n_pairs = GQA_KV_HEADS // 2
    k_col = qw // LANES
    v_col = k_col + n_pairs

    def kv_spec(col0, shift):
        def imap(b, p, i, s):
            return (q_off + b * n_qb + jnp.clip(i + shift, 0, n_qb - 1), col0 + p)
        return pl.BlockSpec((blk, LANES), imap)

    ctx_spec = pl.BlockSpec((1, PAST_LEN, LANES), lambda b, p, i, s: (b, 0, p))
    return pl.pallas_call(
        _gqa_latent_kernel,
        grid_spec=pltpu.PrefetchScalarGridSpec(
            num_scalar_prefetch=1,
            grid=(DEC_BATCH, n_pairs, n_qb),
            in_specs=[
                pl.BlockSpec((blk, q_cols), lambda b, p, i, s: (q_off + b * n_qb + i, p)),
                kv_spec(k_col, -1), kv_spec(k_col, 0), kv_spec(k_col, 1),
                kv_spec(v_col, -1), kv_spec(v_col, 0), kv_spec(v_col, 1),
                ctx_spec, ctx_spec,
            ],
            out_specs=pl.BlockSpec((blk, q_cols), lambda b, p, i, s: (b * n_qb + i, p)),
        ),
        out_shape=jax.ShapeDtypeStruct((N_SAMPLE, qw), BF16),
        compiler_params=_cparams(("arbitrary", "arbitrary", "arbitrary")),
    )(sink, qkv, qkv, qkv, qkv, qkv, qkv, qkv, k_ctx, v_ctx)


MLA_PAIRS = MLA_HEADS // 2


def _mla_masks():
    lane = lax.broadcasted_iota(jnp.int32, (1, 2 * LANES), 1)
    half = LANES // 2
    even = (lane < half) | ((lane >= LANES) & (lane < LANES + MLA_ROPE))
    odd = ((lane >= half) & (lane < LANES)) | ((lane >= LANES + MLA_ROPE) & (lane < LANES + 2 * MLA_ROPE))
    return [even, odd]


def _mla_ctx_kernel(q_ref, kv_ref, kpe_ref, o_ref):
    masks = _mla_masks()
    kpe = kpe_ref[...].astype(BF16)
    half = MLA_Q_COLS // 2
    for j in range(MLA_PAIRS):
        cols = slice(j * LANES, (j + 1) * LANES)
        rcols = slice(half + j * LANES, half + (j + 1) * LANES)
        q = jnp.concatenate([q_ref[:, cols], q_ref[:, rcols]], axis=1)
        k = jnp.concatenate([kv_ref[:, cols], kpe], axis=1)
        o_ref[:, cols] = _pair_attention(q, masks, [k], [kv_ref[:, rcols]], [None], 1.0, None)


def _mla_ctx_attn(q, kv, kpe):
    ow = MLA_HEADS * MLA_V
    return pl.pallas_call(
        _mla_ctx_kernel,
        grid=(BATCH,),
        in_specs=[
            pl.BlockSpec((SEQ, MLA_Q_COLS), lambda b: (b, 0)),
            pl.BlockSpec((SEQ, MLA_Q_COLS), lambda b: (b, 0)),
            pl.BlockSpec((SEQ, LANES), lambda b: (b, 0)),
        ],
        out_specs=pl.BlockSpec((SEQ, ow), lambda b: (b, 0)),
        out_shape=jax.ShapeDtypeStruct((N_PROMPT, ow), BF16),
        compiler_params=_cparams(("arbitrary",)),
    )(q, kv, kpe)


MLA_Q_BLOCK = 256


MLA_PAIRS_PER_STEP = 2


def _mla_latent_kernel(qn_ref, qr_ref, ka_ref, kpe_ref, v_ref, o_ref, kcat_ref):
    @pl.when(pl.program_id(2) == 0)
    def _():
        kpe = kpe_ref[...].astype(BF16)
        for p in range(MLA_PAIRS_PER_STEP):
            kcat_ref[p, :, :LANES] = ka_ref[:, p * LANES:(p + 1) * LANES]
            kcat_ref[p, :, LANES:] = kpe

    for p in range(MLA_PAIRS_PER_STEP):
        cols = slice(p * LANES, (p + 1) * LANES)
        q = jnp.concatenate([qn_ref[:, cols], qr_ref[:, cols]], axis=1)
        o_ref[:, cols] = _pair_attention(q, _mla_masks(), [kcat_ref[p]], [v_ref[:, cols]], [None], 1.0, None)


def _mla_latent_attn(q, kv, kpe):
    ow = MLA_HEADS * MLA_V
    n_qb = DEC_SEQ // MLA_Q_BLOCK
    q_off = N_PROMPT // MLA_Q_BLOCK
    width = MLA_PAIRS_PER_STEP * LANES
    n_groups = MLA_PAIRS // MLA_PAIRS_PER_STEP
    return pl.pallas_call(
        _mla_latent_kernel,
        grid=(DEC_BATCH, n_groups, n_qb),
        in_specs=[
            pl.BlockSpec((MLA_Q_BLOCK, width), lambda b, j, i: (q_off + b * n_qb + i, j)),
            pl.BlockSpec((MLA_Q_BLOCK, width), lambda b, j, i: (q_off + b * n_qb + i, n_groups + j)),
            pl.BlockSpec((MLA_KEYS, width), lambda b, j, i: (b, j)),
            pl.BlockSpec((MLA_KEYS, LANES), lambda b, j, i: (b, 0)),
            pl.BlockSpec((MLA_KEYS, width), lambda b, j, i: (b, n_groups + j)),
        ],
        out_specs=pl.BlockSpec((MLA_Q_BLOCK, width), lambda b, j, i: (b * n_qb + i, j)),
        out_shape=jax.ShapeDtypeStruct((N_SAMPLE, ow), BF16),
        scratch_shapes=[pltpu.VMEM((MLA_PAIRS_PER_STEP, MLA_KEYS, 2 * LANES), BF16)],
        compiler_params=_cparams(("arbitrary", "arbitrary", "arbitrary")),
    )(q, q, kv, kpe, kv)


def _mixer_residual(x_ref, op_ref, os_ref, gate_m_ref, wo_ref):
    is_prompt = pl.program_id(0) < N_PROMPT // ROW_TILE
    o = jnp.where(is_prompt, op_ref[...], os_ref[...])
    return x_ref[...] + gate_m_ref[0] * _dot(o, wo_ref[...])


def _attn_out_specs(k):
    n_p = N_PROMPT // ROW_TILE
    return [
        pl.BlockSpec((ROW_TILE, k), lambda i: (jnp.minimum(i, n_p - 1), 0)),
        pl.BlockSpec((ROW_TILE, k), lambda i: (jnp.maximum(i - n_p, 0), 0)),
        _mod_spec(ROW_TILE),
    ]


def _ffn_kernel(x_ref, op_ref, os_ref, gate_m_ref, wo_ref, g_ref, sh_ref, sc_ref, gate_ref,
                wg_ref, wu_ref, wd_ref, y_ref, acc_ref):
    x = _mixer_residual(x_ref, op_ref, os_ref, gate_m_ref, wo_ref)
    hb = (_rms(x, g_ref[...]) * (1.0 + sc_ref[0]) + sh_ref[0]).astype(BF16)
    acc_ref[...] = jnp.zeros_like(acc_ref)

    def chunk(c, carry):
        a = _silu(_dot(hb, wg_ref[c])) * _dot(hb, wu_ref[c])
        acc_ref[...] += _dot(a.astype(BF16), wd_ref[c])
        return carry

    lax.fori_loop(0, N_FFN_CHUNKS, chunk, 0)
    y_ref[...] = x + gate_ref[0] * acc_ref[...]


def _ffn(x, o_prompt, o_sample, gate_m, w_o, gain, shift, scale, gate, wg, wu, wd):
    return pl.pallas_call(
        _ffn_kernel,
        grid=(N_TOK // ROW_TILE,),
        in_specs=[
            pl.BlockSpec((ROW_TILE, D_MODEL), lambda i: (i, 0)),
            *_attn_out_specs(w_o.shape[0]),
            _resident(w_o.shape),
            pl.BlockSpec((1, D_MODEL), lambda i: (0, 0)),
            _mod_spec(ROW_TILE), _mod_spec(ROW_TILE), _mod_spec(ROW_TILE),
            _resident(wg.shape), _resident(wu.shape), _resident(wd.shape),
        ],
        out_specs=pl.BlockSpec((ROW_TILE, D_MODEL), lambda i: (i, 0)),
        out_shape=jax.ShapeDtypeStruct((N_TOK, D_MODEL), F32),
        scratch_shapes=[pltpu.VMEM((ROW_TILE, D_MODEL), F32)],
        compiler_params=_cparams(("arbitrary",)),
    )(x, o_prompt, o_sample, gate_m, w_o, gain, shift, scale, gate, wg, wu, wd)


SLAB = D_MODEL // LANES


def _store_slabs(ref, x):
    n = x.shape[0]
    for s in range(SLAB):
        ref[pl.ds(s, n, stride=SLAB), :] = x[:, s * LANES:(s + 1) * LANES]


def _load_slabs(ref, n, dtype, row0=0):
    return jnp.concatenate([ref[pl.ds(row0 + s, n, stride=SLAB), :].astype(dtype) for s in range(SLAB)], axis=1)


META_E1, META_E2, META_W1, META_W2, META_R1, META_R2 = range(6)


def _router_kernel(x_ref, op_ref, os_ref, gate_m_ref, wo_ref, g_ref, sh_ref, sc_ref, whi_ref, wlo_ref, tri_ref,
                   x1_ref, h_ref, meta_ref, cnt_ref, carry_ref):
    @pl.when(pl.program_id(0) == 0)
    def _():
        carry_ref[...] = jnp.zeros_like(carry_ref)

    x = _mixer_residual(x_ref, op_ref, os_ref, gate_m_ref, wo_ref)
    x1_ref[...] = x
    h = _rms(x, g_ref[...]) * (1.0 + sc_ref[0]) + sh_ref[0]
    _store_slabs(h_ref, h)
    h_hi, h_lo = _split_bf16(h)
    logits = _dot(h_hi, whi_ref[...]) + _dot(h_lo, whi_ref[...]) + _dot(h_hi, wlo_ref[...])
    lane = lax.broadcasted_iota(jnp.int32, logits.shape, 1)
    lane_f = lane.astype(F32)
    ninf = jnp.float32(-jnp.inf)
    lg = jnp.where(lane < N_EXPERTS, logits, ninf)
    m1 = jnp.max(lg, axis=-1, keepdims=True)
    i1 = jnp.min(jnp.where(lg == m1, lane_f, float(LANES)), axis=-1, keepdims=True)
    lg2 = jnp.where(lane_f == i1, ninf, lg)
    m2 = jnp.max(lg2, axis=-1, keepdims=True)
    i2 = jnp.min(jnp.where(lg2 == m2, lane_f, float(LANES)), axis=-1, keepdims=True)
    e = jnp.exp(m2 - m1)
    w1 = 1.0 / (1.0 + e)
    w2 = e / (1.0 + e)
    sel1 = lane_f == i1
    sel2 = lane_f == i2
    onehot = jnp.where(sel1 | sel2, 1.0, 0.0)
    ranks = _dot(tri_ref[...], onehot.astype(BF16)) + carry_ref[...]
    r1 = jnp.sum(jnp.where(sel1, ranks, 0.0), axis=-1, keepdims=True)
    r2 = jnp.sum(jnp.where(sel2, ranks, 0.0), axis=-1, keepdims=True)
    carry_ref[...] += jnp.sum(onehot, axis=0, keepdims=True)
    cnt_ref[...] = carry_ref[...]
    meta = jnp.zeros(logits.shape, F32)
    for idx, val in ((META_E1, i1), (META_E2, i2), (META_W1, w1), (META_W2, w2), (META_R1, r1), (META_R2, r2)):
        meta = jnp.where(lane == idx, val, meta)
    meta_ref[...] = meta


def _router(x, o_prompt, o_sample, gate_m, w_o, gain, shift, scale, w_hi, w_lo, tri):
    return pl.pallas_call(
        _router_kernel,
        grid=(N_TOK // ROW_TILE,),
        in_specs=[
            pl.BlockSpec((ROW_TILE, D_MODEL), lambda i: (i, 0)),
            *_attn_out_specs(w_o.shape[0]),
            _resident(w_o.shape),
            pl.BlockSpec((1, D_MODEL), lambda i: (0, 0)),
            _mod_spec(ROW_TILE), _mod_spec(ROW_TILE),
            _resident(w_hi.shape), _resident(w_lo.shape), _resident(tri.shape),
        ],
        out_specs=[
            pl.BlockSpec((ROW_TILE, D_MODEL), lambda i: (i, 0)),
            pl.BlockSpec((ROW_TILE * SLAB, LANES), lambda i: (i, 0)),
            pl.BlockSpec((ROW_TILE, LANES), lambda i: (i, 0)),
            pl.BlockSpec((1, LANES), lambda i: (0, 0)),
        ],
        out_shape=[
            jax.ShapeDtypeStruct((N_TOK, D_MODEL), F32),
            jax.ShapeDtypeStruct((N_TOK * SLAB, LANES), F32),
            jax.ShapeDtypeStruct((N_TOK, LANES), F32),
            jax.ShapeDtypeStruct((1, LANES), F32),
        ],
        scratch_shapes=[pltpu.VMEM((1, LANES), F32)],
        compiler_params=_cparams(("arbitrary",)),
    )(x, o_prompt, o_sample, gate_m, w_o, gain, shift, scale, w_hi, w_lo, tri)


def _route_plan(meta, counts):
    cnt = counts[0, :N_EXPERTS].astype(jnp.int32)
    tiles = (cnt + EXP_TILE - 1) // EXP_TILE
    tile_end = jnp.cumsum(tiles)
    tile_start = tile_end - tiles
    offs = tile_start * EXP_TILE
    experts = jnp.arange(N_EXPERTS, dtype=jnp.int32)

    def lookup(table, idx):
        return jnp.sum(jnp.where(idx[:, None] == experts[None, :], table[None, :], 0), axis=1)

    e1 = meta[:, META_E1].astype(jnp.int32)
    e2 = meta[:, META_E2].astype(jnp.int32)
    pos1 = lookup(offs, e1) + meta[:, META_R1].astype(jnp.int32)
    pos2 = lookup(offs, e2) + meta[:, META_R2].astype(jnp.int32)
    t = jnp.arange(N_EXP_TILES, dtype=jnp.int32)
    tile_expert = jnp.minimum(jnp.sum(t[:, None] >= tile_end[None, :], axis=1), N_EXPERTS - 1).astype(jnp.int32)
    in_group = (t - lookup(tile_start, tile_expert)) * EXP_TILE
    tile_rows = jnp.where(t < tile_end[-1], jnp.clip(lookup(cnt, tile_expert) - in_group, 0, EXP_TILE),
                          0).astype(jnp.int32)
    return pos1, pos2, tile_expert, tile_rows


DISPATCH_TILE = 512


def _dispatch_kernel(pos1_ref, pos2_ref, h_ref, init_ref, xs_ref, sem):
    del init_ref
    base = pl.program_id(0) * DISPATCH_TILE

    def row_copy(t, dst):
        return pltpu.make_async_copy(
            h_ref.at[pl.ds(pl.multiple_of(t * SLAB, SLAB), SLAB)],
            xs_ref.at[pl.ds(pl.multiple_of(dst * SLAB, SLAB), SLAB)], sem)

    def issue(t, carry):
        row_copy(t, pos1_ref[base + t]).start()
        row_copy(t, pos2_ref[base + t]).start()
        return carry

    lax.fori_loop(0, DISPATCH_TILE, issue, 0)

    def drain(t, carry):
        row_copy(0, 0).wait()
        row_copy(0, 0).wait()
        return carry

    lax.fori_loop(0, DISPATCH_TILE, drain, 0)


def _dispatch(pos1, pos2, h_slabs, xs_init):
    return pl.pallas_call(
        _dispatch_kernel,
        grid_spec=pltpu.PrefetchScalarGridSpec(
            num_scalar_prefetch=2,
            grid=(N_TOK // DISPATCH_TILE,),
            in_specs=[pl.BlockSpec((DISPATCH_TILE * SLAB, LANES), lambda i, p1, p2: (i, 0)),
                      pl.BlockSpec(memory_space=pl.ANY)],
            out_specs=pl.BlockSpec(memory_space=pl.ANY),
            scratch_shapes=[pltpu.SemaphoreType.DMA(())],
        ),
        out_shape=jax.ShapeDtypeStruct(xs_init.shape, F32),
        input_output_aliases={3: 0},
        compiler_params=_cparams(("arbitrary",)),
    )(pos1, pos2, h_slabs, xs_init)


def _expert_kernel(te_ref, tr_ref, xs_ref, wg_ref, wu_ref, wd_ref, ys_ref, xb_ref, acc_ref):
    t = pl.program_id(0)
    c = pl.program_id(1)

    @pl.when(c == 0)
    def _():
        xb_ref[...] = _load_slabs(xs_ref, EXP_TILE, BF16)
        acc_ref[...] = jnp.zeros_like(acc_ref)

    rows = tr_ref[t]

    def swiglu_rows(n):
        xb = xb_ref[:n]
        a = _silu(_dot(xb, wg_ref[...].astype(BF16))) * _dot(xb, wu_ref[...].astype(BF16))
        acc_ref[:n] += _dot(a.astype(BF16), wd_ref[...].astype(BF16))

    @pl.when(rows > EXP_PART)
    def _():
        swiglu_rows(EXP_TILE)

    @pl.when((rows > 0) & (rows <= EXP_PART))
    def _():
        swiglu_rows(EXP_PART)

    @pl.when(c == N_EXP_CHUNKS - 1)
    def _():
        _store_slabs(ys_ref, acc_ref[...])


def _experts(layer, tile_expert, tile_rows, xs, wg, wu, wd):
    def chunk_of(t, c, tr):
        return jnp.where(tr[t] > 0, c, N_EXP_CHUNKS - 1)

    return pl.pallas_call(
        _expert_kernel,
        grid_spec=pltpu.PrefetchScalarGridSpec(
            num_scalar_prefetch=2,
            grid=(N_EXP_TILES, N_EXP_CHUNKS),
            in_specs=[
                pl.BlockSpec((EXP_TILE * SLAB, LANES), lambda t, c, te, tr: (t, 0)),
                pl.BlockSpec((None, None, D_MODEL, EXP_CHUNK),
                             lambda t, c, te, tr: (layer, te[t], 0, chunk_of(t, c, tr))),
                pl.BlockSpec((None, None, D_MODEL, EXP_CHUNK),
                             lambda t, c, te, tr: (layer, te[t], 0, chunk_of(t, c, tr))),
                pl.BlockSpec((None, None, EXP_CHUNK, D_MODEL),
                             lambda t, c, te, tr: (layer, te[t], chunk_of(t, c, tr), 0)),
            ],
            out_specs=pl.BlockSpec((EXP_TILE * SLAB, LANES), lambda t, c, te, tr: (t, 0)),
            scratch_shapes=[pltpu.VMEM((EXP_TILE, D_MODEL), BF16), pltpu.VMEM((EXP_TILE, D_MODEL), F32)],
        ),
        out_shape=jax.ShapeDtypeStruct(xs.shape, F32),
        compiler_params=_cparams(("arbitrary", "arbitrary")),
    )(tile_expert, tile_rows, xs, wg, wu, wd)


def _combine_kernel(pos1_ref, pos2_ref, x_ref, gate_ref, meta_ref, ys_ref, y_ref, buf_ref, sem):
    base = pl.program_id(0) * COMBINE_TILE
    n = COMBINE_TILE

    def row_copy(src, slot):
        return pltpu.make_async_copy(
            ys_ref.at[pl.ds(pl.multiple_of(src * SLAB, SLAB), SLAB)],
            buf_ref.at[pl.ds(pl.multiple_of(slot * SLAB, SLAB), SLAB)], sem)

    def issue(t, carry):
        row_copy(pos1_ref[base + t], t).start()
        row_copy(pos2_ref[base + t], n + t).start()
        return carry

    lax.fori_loop(0, n, issue, 0)

    def drain(t, carry):
        row_copy(0, 0).wait()
        row_copy(0, 0).wait()
        return carry

    lax.fori_loop(0, n, drain, 0)

    meta = meta_ref[...]
    w1 = meta[:, META_W1:META_W1 + 1]
    w2 = meta[:, META_W2:META_W2 + 1]
    y1 = _load_slabs(buf_ref, n, F32)
    y2 = _load_slabs(buf_ref, n, F32, row0=n * SLAB)
    y_ref[...] = x_ref[...] + gate_ref[0] * (w1 * y1 + w2 * y2)


def _combine(pos1, pos2, x, gate, meta, ys):
    return pl.pallas_call(
        _combine_kernel,
        grid_spec=pltpu.PrefetchScalarGridSpec(
            num_scalar_prefetch=2,
            grid=(N_TOK // COMBINE_TILE,),
            in_specs=[
                pl.BlockSpec((COMBINE_TILE, D_MODEL), lambda i, p1, p2: (i, 0)),
                pl.BlockSpec((1, 1, D_MODEL), lambda i, p1, p2: (_cond_of_row(i * COMBINE_TILE), 0, 0)),
                pl.BlockSpec((COMBINE_TILE, LANES), lambda i, p1, p2: (i, 0)),
                pl.BlockSpec(memory_space=pl.ANY),
            ],
            out_specs=pl.BlockSpec((COMBINE_TILE, D_MODEL), lambda i, p1, p2: (i, 0)),
            scratch_shapes=[pltpu.VMEM((2 * COMBINE_TILE * SLAB, LANES), F32), pltpu.SemaphoreType.DMA(())],
        ),
        out_shape=jax.ShapeDtypeStruct((N_TOK, D_MODEL), F32),
        compiler_params=_cparams(("arbitrary",)),
    )(pos1, pos2, x, gate, meta, ys)


def _moe(layer, x, o_prompt, o_sample, gate_m, w_o, gain, shift, scale, gate, router_w, wg, wu, wd, tri):
    w = jnp.pad(router_w, ((0, 0), (0, LANES - N_EXPERTS)))
    w_hi = w.astype(BF16)
    w_lo = (w - w_hi.astype(F32)).astype(BF16)
    x1, h_slabs, meta, counts = _router(x, o_prompt, o_sample, gate_m, w_o, gain, shift, scale, w_hi, w_lo, tri)
    pos1, pos2, tile_expert, tile_rows = _route_plan(meta, counts)
    xs = _dispatch(pos1, pos2, h_slabs, jnp.zeros((N_EXP_TILES * EXP_TILE * SLAB, LANES), F32))
    ys = _experts(layer, tile_expert, tile_rows, xs, wg, wu, wd)
    return _combine(pos1, pos2, x1, gate, meta, ys)


def _final_norm_kernel(x_ref, g_ref, y_ref):
    y_ref[...] = _rms(x_ref[...], g_ref[...])


def _final_norm(x, gain, row0, rows):
    off = row0 // ROW_TILE
    return pl.pallas_call(
        _final_norm_kernel,
        grid=(rows // ROW_TILE,),
        in_specs=[
            pl.BlockSpec((ROW_TILE, D_MODEL), lambda i: (off + i, 0)),
            pl.BlockSpec((1, D_MODEL), lambda i: (0, 0)),
        ],
        out_specs=pl.BlockSpec((ROW_TILE, D_MODEL), lambda i: (i, 0)),
        out_shape=jax.ShapeDtypeStruct((rows, D_MODEL), F32),
        compiler_params=_cparams(("arbitrary",)),
    )(x, gain)


def _rope_tables(rot_dim, reps):
    n_freq = rot_dim // 4
    inv_freq = ROPE_THETA ** (-np.arange(n_freq, dtype=np.float64) / n_freq)
    t = np.arange(DEC_SEQ)
    ang_r = (t // GRID_W)[:, None] * inv_freq
    ang_c = (t % GRID_W)[:, None] * inv_freq
    cos = np.concatenate([np.cos(ang_r), np.cos(ang_r), np.cos(ang_c), np.cos(ang_c)], axis=1)
    sin = np.concatenate([-np.sin(ang_r), np.sin(ang_r), -np.sin(ang_c), np.sin(ang_c)], axis=1)
    pad = LANES - reps * rot_dim
    cos = np.concatenate([cos] * reps + [np.ones((DEC_SEQ, pad))], axis=1)
    sin = np.concatenate([sin] * reps + [np.zeros((DEC_SEQ, pad))], axis=1)
    cos = np.concatenate([cos, np.ones((ROW_TILE, LANES))], axis=0)
    sin = np.concatenate([sin, np.zeros((ROW_TILE, LANES))], axis=0)
    return jnp.asarray(cos, F32), jnp.asarray(sin, F32)


def _swap_halves(w, rot_dim):
    k, n = w.shape
    q = rot_dim // 4
    return w.reshape(k, n // (2 * q), 2, q)[:, :, ::-1, :].reshape(k, n)


def _gqa_arrange(w_q, axis):
    pairs = GQA_KV_HEADS // 2
    shape = w_q.shape
    split = shape[:axis] + (pairs, 2, GQA_GROUP, GQA_HEAD_DIM) + shape[axis + 1:]
    order = list(range(len(split)))
    order[axis + 1], order[axis + 2] = order[axis + 2], order[axis + 1]
    return w_q.reshape(split).transpose(order).reshape(shape)


def _mla_layouts(wq_b, wkv_a, wkv_b):
    hd = MLA_NOPE + MLA_ROPE
    w3 = wq_b.reshape(MLA_Q_LORA, MLA_HEADS, hd)
    nope = w3[:, :, :MLA_NOPE].reshape(MLA_Q_LORA, MLA_HEADS * MLA_NOPE)
    rope = w3[:, :, MLA_NOPE:].reshape(MLA_Q_LORA, MLA_HEADS * MLA_ROPE)

    def rope_blocks(r):
        r = r.reshape(-1, MLA_PAIRS, 2 * MLA_ROPE)
        return jnp.pad(r, ((0, 0), (0, 0), (0, LANES - 2 * MLA_ROPE))).reshape(-1, MLA_PAIRS * LANES)

    wb = jnp.concatenate([nope, rope_blocks(rope)], axis=1)
    wbs = rope_blocks(_swap_halves(rope, MLA_ROPE))
    kpe_w = wkv_a[:, MLA_KV_LORA:]
    kpe_ws = _swap_halves(kpe_w, MLA_ROPE)
    zeros_a = jnp.zeros((D_MODEL, LANES - 2 * MLA_ROPE), F32)
    wa_tail = jnp.concatenate([wkv_a[:, :MLA_KV_LORA], kpe_w, kpe_w, zeros_a, kpe_ws, kpe_ws, zeros_a], axis=1)
    kv3 = wkv_b.reshape(MLA_KV_LORA, MLA_HEADS, MLA_NOPE + MLA_V)
    wkvb = jnp.concatenate([kv3[:, :, :MLA_NOPE].reshape(MLA_KV_LORA, -1),
                            kv3[:, :, MLA_NOPE:].reshape(MLA_KV_LORA, -1)], axis=1)
    return wb, wbs, wa_tail, wkvb


def _chunk_cols(w, chunk):
    k, n = w.shape
    return w.reshape(k, n // chunk, chunk).transpose(1, 0, 2).astype(BF16)


def kernel(x_prompt, x_sample, cache_l0_k, cache_l0_v, cache_l1_ckv, cache_l1_kpe, cache_l2_k, cache_l2_v, cache_l3_k, cache_l3_v, c, c_ctx, ada_w, ada_b, norm_mix, norm_ffn, norm_final, na_w_qkv, na_w_o, na_rel_bias, mla_wq_a, mla_q_norm, mla_wq_b, mla_wkv_a, mla_kv_norm, mla_wkv_b, mla_w_o, gqa_w_qkv, gqa_w_o, gqa_sink, ffn_w_gate, ffn_w_up, ffn_w_down, moe_router, moe_w_gate, moe_w_up, moe_w_down):
    x = jnp.concatenate([x_prompt.reshape(N_PROMPT, D_MODEL), x_sample.reshape(N_SAMPLE, D_MODEL)], axis=0)
    cond = jnp.concatenate([c_ctx[None, :], c, jnp.zeros((N_COND - 1 - DEC_BATCH, D_MODEL), F32)], axis=0)
    mods = _adaln(cond, ada_w, ada_b).reshape(DEPTH, N_COND, 6, 1, D_MODEL)
    tri = jnp.asarray(np.tril(np.ones((ROW_TILE, ROW_TILE), np.float32), -1), BF16)
    no_sink = jnp.zeros((GQA_HEADS,), F32)
    na_caches = {0: (cache_l0_k, cache_l0_v), 3: (cache_l3_k, cache_l3_v)}
    state = []

    for i in range(DEPTH):
        sh_m, sc_m, g_m, sh_f, sc_f, g_f = [mods[i, :, j] for j in range(6)]
        gain_m = norm_mix[i][None, :]
        gain_f = norm_ffn[i][None, :]
        kind, s = i % 3, i // 3
        if kind == 0:
            qw = NA_HEADS * NA_HEAD_DIM
            qkv, k_rows, v_rows = _norm_proj(x, gain_m, sh_m, sc_m, na_w_qkv[s].astype(BF16), qw)
            o_p = _ctx_attn(qkv, no_sink, qw, False)
            ck, cv = na_caches[i]
            o_s = _na_latent_attn(qkv, ck.reshape(DEC_BATCH, PAST_LEN, qw), cv.reshape(DEC_BATCH, PAST_LEN, qw),
                                  _na_pair_table(na_rel_bias[s]))
            w_o = na_w_o[s].astype(BF16)
            kv_shape = (BATCH, SEQ, NA_HEADS, NA_HEAD_DIM)
            state += [k_rows.reshape(kv_shape), v_rows.reshape(kv_shape)]
        elif kind == 1:
            wb, wbs, wa_tail, wkvb = _mla_layouts(mla_wq_b[s], mla_wkv_a[s], mla_wkv_b[s])
            wa = jnp.concatenate([mla_wq_a[s], wa_tail], axis=1).astype(BF16)
            cos, sin = _rope_tables(MLA_ROPE, 2)
            q, ckv, kpe = _mla_proj(x, gain_m, sh_m, sc_m, wa, mla_q_norm[s][None, :], mla_kv_norm[s][None, :],
                                    wb.astype(BF16), wbs.astype(BF16), cos, sin)
            wkvb = wkvb.astype(BF16)
            kv_p = _rows_matmul(ckv[:N_PROMPT], wkvb, BF16)
            o_p = _mla_ctx_attn(q, kv_p, kpe)
            ckv_s = ckv[N_PROMPT:].reshape(DEC_BATCH, DEC_SEQ, MLA_KV_LORA)
            ckv_all = jnp.concatenate([cache_l1_ckv, ckv_s], axis=1).reshape(DEC_BATCH * MLA_KEYS, MLA_KV_LORA)
            kv_s = _rows_matmul(ckv_all, wkvb, BF16)
            kpe_cache = jnp.concatenate(
                [cache_l1_kpe, cache_l1_kpe, jnp.zeros((DEC_BATCH, PAST_LEN, LANES - 2 * MLA_ROPE), F32)], axis=2)
            kpe_all = jnp.concatenate([kpe_cache, kpe[N_PROMPT:].reshape(DEC_BATCH, DEC_SEQ, LANES)], axis=1)
            o_s = _mla_latent_attn(q, kv_s, kpe_all.reshape(DEC_BATCH * MLA_KEYS, LANES))
            w_o = mla_w_o[s].astype(BF16)
            state += [ckv[:N_PROMPT].reshape(BATCH, SEQ, MLA_KV_LORA),
                      kpe[:N_PROMPT, :MLA_ROPE].reshape(BATCH, SEQ, MLA_ROPE)]
        else:
            qw = GQA_HEADS * GQA_HEAD_DIM
            kw = GQA_KV_HEADS * GQA_HEAD_DIM
            w_qkv = jnp.concatenate([_gqa_arrange(gqa_w_qkv[s][:, :qw], 1), gqa_w_qkv[s][:, qw:]], axis=1)
            w_swap = _swap_halves(w_qkv[:, :qw + kw], GQA_HEAD_DIM)
            cos, sin = _rope_tables(GQA_HEAD_DIM, 2)
            qkv, k_rows, v_rows = _norm_proj(x, gain_m, sh_m, sc_m, w_qkv.astype(BF16), kw,
                                             rope=(w_swap.astype(BF16), cos, sin))
            sink = gqa_sink[s]
            o_p = _ctx_attn(qkv, sink, kw, True)
            o_s = _gqa_latent_attn(qkv, sink, cache_l2_k.reshape(DEC_BATCH, PAST_LEN, kw),
                                   cache_l2_v.reshape(DEC_BATCH, PAST_LEN, kw))
            w_o = _gqa_arrange(gqa_w_o[s], 0).astype(BF16)
            kv_shape = (BATCH, SEQ, GQA_KV_HEADS, GQA_HEAD_DIM)
            state += [k_rows.reshape(kv_shape), v_rows.reshape(kv_shape)]
        s = i // 2
        if i % 2 == 0:
            x = _ffn(x, o_p, o_s, g_m, w_o, gain_f, sh_f, sc_f, g_f, _chunk_cols(ffn_w_gate[s], FFN_CHUNK),
                     _chunk_cols(ffn_w_up[s], FFN_CHUNK),
                     ffn_w_down[s].reshape(N_FFN_CHUNKS, FFN_CHUNK, D_MODEL).astype(BF16))
        else:
            x = _moe(s, x, o_p, o_s, g_m, w_o, gain_f, sh_f, sc_f, g_f, moe_router[s],
                     moe_w_gate, moe_w_up, moe_w_down, tri)

    y_prompt = _final_norm(x, norm_final[None, :], 0, N_PROMPT).reshape(BATCH, SEQ, D_MODEL)
    y_sample = _final_norm(x, norm_final[None, :], N_PROMPT, N_SAMPLE).reshape(DEC_BATCH, DEC_SEQ, D_MODEL)
    return (y_prompt, y_sample, *state)
```

```python
import functools
import math

import numpy as np
import jax
import jax.numpy as jnp
from jax import lax
from jax.experimental import pallas as pl
from jax.experimental.pallas import tpu as pltpu

F32 = jnp.float32
BF16 = jnp.bfloat16

D_MODEL = 1024
BATCH = 16
SEQ = 256
DEPTH = 4
DEC_BATCH = 4
DEC_SEQ = 2048
PAST_LEN = 512
GRID_W = 64
GRID_ROWS = DEC_SEQ // GRID_W
NORM_EPS = 1e-6
NEG_INF = -1e30
ROPE_THETA = 10000.0
NA_HEADS = 16
NA_HEAD_DIM = 64
NA_WIN_ROWS = 8
NA_WIN_COLS = 16
MLA_HEADS = 16
MLA_Q_LORA = 512
MLA_KV_LORA = 256
MLA_NOPE = 64
MLA_ROPE = 32
MLA_V = 64
GQA_HEADS = 16
GQA_KV_HEADS = 4
GQA_HEAD_DIM = 64
GQA_WINDOW = 128
FFN_DIM = 2816
N_EXPERTS = 8
EXPERT_DIM = 3584

N_PROMPT = BATCH * SEQ
N_SAMPLE = DEC_BATCH * DEC_SEQ
N_TOK = N_PROMPT + N_SAMPLE
N_COND = 8
LOG2E = math.log2(math.e)
HEAD_SCALE = NA_HEAD_DIM ** -0.5 * LOG2E
MLA_SCALE = (MLA_NOPE + MLA_ROPE) ** -0.5 * LOG2E

LANES = 128
SUBLANES = 8
VMEM_LIMIT = 56 * 1024 * 1024

ROW_TILE = 512
FFN_CHUNK = 256
N_FFN_CHUNKS = FFN_DIM // FFN_CHUNK
EXP_TILE = 1024
EXP_PART = 512
EXP_CHUNK = 512
N_EXP_CHUNKS = EXPERT_DIM // EXP_CHUNK
N_EXP_TILES = (2 * N_TOK) // EXP_TILE + N_EXPERTS
COMBINE_TILE = 256
NA_QROWS = 4
NA_UNION = NA_QROWS + NA_WIN_ROWS
NA_PAIRS_PER_STEP = 4
MLA_KEYS = PAST_LEN + DEC_SEQ


def _cparams(sem):
    return pltpu.CompilerParams(dimension_semantics=sem, vmem_limit_bytes=VMEM_LIMIT)


def _cond_of_row(row):
    return jnp.where(row < N_PROMPT, 0, 1 + (row - N_PROMPT) // DEC_SEQ)


def _nt_dot(a, b):
    return lax.dot_general(a, b, (((1,), (1,)), ((), ())), preferred_element_type=F32)


def _dot(a, b):
    return jnp.dot(a, b, preferred_element_type=F32)


def _rms(x, gain):
    return x * lax.rsqrt(jnp.mean(x * x, axis=-1, keepdims=True) + NORM_EPS) * gain


def _silu(x):
    return x / (1.0 + jnp.exp(-x))


def _split_bf16(x):
    hi = x.astype(BF16)
    lo = (x - hi.astype(F32)).astype(BF16)
    return hi, lo


def _resident(shape):
    nd = len(shape)
    return pl.BlockSpec(shape, lambda *_: (0,) * nd, pipeline_mode=pl.Buffered(1))


def _mod_spec(tile):
    return pl.BlockSpec((1, 1, D_MODEL), lambda i: (_cond_of_row(i * tile), 0, 0))


ADA_TILE = 1536


def _adaln_kernel(c_ref, w_ref, b_ref, o_ref):
    a_hi, a_lo = _split_bf16(_silu(c_ref[...]))
    w_hi, w_lo = _split_bf16(w_ref[0])
    o_ref[0] = _dot(a_hi, w_hi) + _dot(a_lo, w_hi) + _dot(a_hi, w_lo) + b_ref[0]


def _adaln(cond, ada_w, ada_b):
    n = 6 * D_MODEL
    return pl.pallas_call(
        _adaln_kernel,
        grid=(DEPTH, n // ADA_TILE),
        in_specs=[
            pl.BlockSpec((N_COND, D_MODEL), lambda l, j: (0, 0)),
            pl.BlockSpec((1, D_MODEL, ADA_TILE), lambda l, j: (l, 0, j)),
            pl.BlockSpec((1, 1, ADA_TILE), lambda l, j: (l, 0, j)),
        ],
        out_specs=pl.BlockSpec((1, N_COND, ADA_TILE), lambda l, j: (l, 0, j)),
        out_shape=jax.ShapeDtypeStruct((DEPTH, N_COND, n), F32),
        compiler_params=_cparams(("arbitrary", "arbitrary")),
    )(cond, ada_w, ada_b.reshape(DEPTH, 1, n))


PROJ_COLS = 512


def _norm_proj_kernel(x_ref, g_ref, sh_ref, sc_ref, w_ref, *rest, n_out, n_rope, kv_cols):
    rest = list(rest)
    if n_rope:
        w2_ref, cos_ref, sin_ref = rest[:3]
        rest = rest[3:]
    o_ref, k_ref, v_ref = rest
    h = _rms(x_ref[...], g_ref[...]) * (1.0 + sc_ref[0]) + sh_ref[0]
    hb = h.astype(BF16)
    is_prompt = pl.program_id(0) < N_PROMPT // ROW_TILE
    cuts = sorted({0, n_out, n_rope, n_out - 2 * kv_cols, n_out - kv_cols}
                  | set(range(0, n_out, PROJ_COLS)))
    for c0, c1 in zip(cuts[:-1], cuts[1:]):
        cols = slice(c0, c1)
        y = _dot(hb, w_ref[:, cols])
        if c0 < n_rope:
            ys = _dot(hb, w2_ref[:, cols])
            reps = (c1 - c0) // LANES
            cos = jnp.concatenate([cos_ref[...]] * reps, axis=1)
            sin = jnp.concatenate([sin_ref[...]] * reps, axis=1)
            y = y * cos + ys * sin
        is_query = c0 < n_out - 2 * kv_cols
        o_ref[:, cols] = (y * HEAD_SCALE if is_query else y).astype(BF16)
        for kv_ref, start in ((k_ref, n_out - 2 * kv_cols), (v_ref, n_out - kv_cols)):
            if start <= c0 < start + kv_cols:
                @pl.when(is_prompt)
                def _(kv_ref=kv_ref, start=start, y=y, c0=c0, c1=c1):
                    kv_ref[:, c0 - start:c1 - start] = y


def _rope_block(i):
    row = i * ROW_TILE
    return jnp.where(row < N_PROMPT, DEC_SEQ // ROW_TILE, ((row - N_PROMPT) % DEC_SEQ) // ROW_TILE)


def _norm_proj(x, gain, shift, scale, w, kv_cols, rope=None):
    n_out = w.shape[1]
    last_prompt = N_PROMPT // ROW_TILE - 1
    kv_spec = pl.BlockSpec((ROW_TILE, kv_cols), lambda i: (jnp.minimum(i, last_prompt), 0))
    kv_shape = jax.ShapeDtypeStruct((N_PROMPT, kv_cols), F32)
    in_specs = [
        pl.BlockSpec((ROW_TILE, D_MODEL), lambda i: (i, 0)),
        pl.BlockSpec((1, D_MODEL), lambda i: (0, 0)),
        _mod_spec(ROW_TILE),
        _mod_spec(ROW_TILE),
        _resident(w.shape),
    ]
    args = [x, gain, shift, scale, w]
    n_rope = 0
    if rope is not None:
        w2, cos, sin = rope
        n_rope = w2.shape[1]
        in_specs += [
            _resident(w2.shape),
            pl.BlockSpec((ROW_TILE, LANES), lambda i: (_rope_block(i), 0)),
            pl.BlockSpec((ROW_TILE, LANES), lambda i: (_rope_block(i), 0)),
        ]
        args += [w2, cos, sin]
    return pl.pallas_call(
        functools.partial(_norm_proj_kernel, n_out=n_out, n_rope=n_rope, kv_cols=kv_cols),
        grid=(N_TOK // ROW_TILE,),
        in_specs=in_specs,
        out_specs=[pl.BlockSpec((ROW_TILE, n_out), lambda i: (i, 0)), kv_spec, kv_spec],
        out_shape=[jax.ShapeDtypeStruct((N_TOK, n_out), BF16), kv_shape, kv_shape],
        compiler_params=_cparams(("arbitrary",)),
    )(*args)


MLA_Q_COLS = 2 * MLA_HEADS * MLA_NOPE


def _mla_proj_kernel(x_ref, g_ref, sh_ref, sc_ref, wa_ref, qn_ref, kvn_ref, wb_ref, wbs_ref,
                     cos_ref, sin_ref, q_ref, ckv_ref, kpe_ref):
    h = _rms(x_ref[...], g_ref[...]) * (1.0 + sc_ref[0]) + sh_ref[0]
    a = _dot(h.astype(BF16), wa_ref[...])
    c0 = MLA_Q_LORA
    c1 = c0 + MLA_KV_LORA
    cos = cos_ref[...]
    sin = sin_ref[...]
    ckv_ref[...] = _rms(a[:, c0:c1], kvn_ref[...])
    kpe_ref[...] = a[:, c1:c1 + LANES] * cos + a[:, c1 + LANES:c1 + 2 * LANES] * sin
    qn = _rms(a[:, :c0], qn_ref[...]).astype(BF16)
    half = MLA_Q_COLS // 2
    q_scale = MLA_SCALE
    q_ref[:, :half] = (_dot(qn, wb_ref[:, :half]) * q_scale).astype(BF16)
    q_rope = _dot(qn, wb_ref[:, half:])
    q_swap = _dot(qn, wbs_ref[...])
    reps = half // LANES
    cos_r = jnp.concatenate([cos] * reps, axis=1) * q_scale
    sin_r = jnp.concatenate([sin] * reps, axis=1) * q_scale
    q_ref[:, half:] = (q_rope * cos_r + q_swap * sin_r).astype(BF16)


def _mla_proj(x, gain, shift, scale, wa, q_norm, kv_norm, wb, wbs, cos, sin):
    return pl.pallas_call(
        _mla_proj_kernel,
        grid=(N_TOK // ROW_TILE,),
        in_specs=[
            pl.BlockSpec((ROW_TILE, D_MODEL), lambda i: (i, 0)),
            pl.BlockSpec((1, D_MODEL), lambda i: (0, 0)),
            _mod_spec(ROW_TILE),
            _mod_spec(ROW_TILE),
            _resident(wa.shape),
            pl.BlockSpec((1, MLA_Q_LORA), lambda i: (0, 0)),
            pl.BlockSpec((1, MLA_KV_LORA), lambda i: (0, 0)),
            _resident(wb.shape),
            _resident(wbs.shape),
            pl.BlockSpec((ROW_TILE, LANES), lambda i: (_rope_block(i), 0)),
            pl.BlockSpec((ROW_TILE, LANES), lambda i: (_rope_block(i), 0)),
        ],
        out_specs=[
            pl.BlockSpec((ROW_TILE, MLA_Q_COLS), lambda i: (i, 0)),
            pl.BlockSpec((ROW_TILE, MLA_KV_LORA), lambda i: (i, 0)),
            pl.BlockSpec((ROW_TILE, LANES), lambda i: (i, 0)),
        ],
        out_shape=[
            jax.ShapeDtypeStruct((N_TOK, MLA_Q_COLS), BF16),
            jax.ShapeDtypeStruct((N_TOK, MLA_KV_LORA), F32),
            jax.ShapeDtypeStruct((N_TOK, LANES), F32),
        ],
        compiler_params=_cparams(("arbitrary",)),
    )(x, gain, shift, scale, wa, q_norm, kv_norm, wb, wbs, cos, sin)


def _rows_matmul_kernel(x_ref, w_ref, o_ref):
    o_ref[...] = _dot(x_ref[...].astype(BF16), w_ref[...]).astype(o_ref.dtype)


def _rows_matmul(x, w, out_dtype):
    rows, k = x.shape
    n = w.shape[1]
    return pl.pallas_call(
        _rows_matmul_kernel,
        grid=(rows // ROW_TILE,),
        in_specs=[pl.BlockSpec((ROW_TILE, k), lambda i: (i, 0)), _resident(w.shape)],
        out_specs=pl.BlockSpec((ROW_TILE, n), lambda i: (i, 0)),
        out_shape=jax.ShapeDtypeStruct((rows, n), out_dtype),
        compiler_params=_cparams(("arbitrary",)),
    )(x, w)


def _pair_attention(q, masks, keys, values, biases, q_scale, sinks):
    half = LANES // 2
    lane = lax.broadcasted_iota(jnp.int32, (1, LANES), 1)
    outs = []
    sink_terms = []
    for h in range(2):
        qh = jnp.where(masks[h], q, jnp.zeros_like(q))
        if q_scale != 1.0:
            qh = qh * q_scale
        qh = qh.astype(BF16)
        scores = []
        for k, bias in zip(keys, biases):
            s = _nt_dot(qh, k)
            if bias is not None:
                s = s + bias(h)
            scores.append(s)
        m = functools.reduce(jnp.maximum, [jnp.max(s, axis=-1, keepdims=True) for s in scores])
        if sinks is not None:
            m = jnp.maximum(m, sinks[h])
        own = (lane < half) if h == 0 else (lane >= half)
        acc = None
        for s, v in zip(scores, values):
            part = _dot(jnp.exp2(s - m).astype(BF16), jnp.where(own, v, jnp.ones_like(v)))
            acc = part if acc is None else acc + part
        outs.append(acc)
        if sinks is not None:
            sink_terms.append(jnp.exp2(sinks[h] - m))
    first = lane < half
    numer = jnp.where(first, outs[0], outs[1])
    denom = pltpu.roll(jnp.where(first, outs[1], outs[0]), half, axis=1)
    if sinks is not None:
        denom = denom + jnp.where(first, sink_terms[0], sink_terms[1])
    return (numer / denom).astype(BF16)


def _half_masks(width=LANES):
    lane = lax.broadcasted_iota(jnp.int32, (1, width), 1)
    return [lane < LANES // 2, lane >= LANES // 2]


def _ctx_attn_kernel(sink_ref, q_ref, k_ref, v_ref, o_ref, *, kv_blocks, use_sink):
    masks = _half_masks()
    n_q_blocks = o_ref.shape[1] // LANES
    for qb in range(n_q_blocks):
        kb = qb * kv_blocks // n_q_blocks
        cols = slice(kb * LANES, (kb + 1) * LANES)
        sinks = None
        if use_sink:
            base = (qb // 4) * 8 + qb % 4
            sinks = [sink_ref[base] * LOG2E, sink_ref[base + 4] * LOG2E]
        o_ref[:, qb * LANES:(qb + 1) * LANES] = _pair_attention(
            q_ref[:, qb * LANES:(qb + 1) * LANES], masks,
            [k_ref[:, cols]], [v_ref[:, cols]], [None], 1.0, sinks)


def _ctx_attn(qkv, sink, kv_width, use_sink):
    qw = NA_HEADS * NA_HEAD_DIM
    k_blk = qw // kv_width
    return pl.pallas_call(
        functools.partial(_ctx_attn_kernel, kv_blocks=kv_width // LANES, use_sink=use_sink),
        grid_spec=pltpu.PrefetchScalarGridSpec(
            num_scalar_prefetch=1,
            grid=(BATCH,),
            in_specs=[
                pl.BlockSpec((SEQ, qw), lambda b, s: (b, 0)),
                pl.BlockSpec((SEQ, kv_width), lambda b, s: (b, k_blk)),
                pl.BlockSpec((SEQ, kv_width), lambda b, s: (b, k_blk + 1)),
            ],
            out_specs=pl.BlockSpec((SEQ, qw), lambda b, s: (b, 0)),
        ),
        out_shape=jax.ShapeDtypeStruct((N_PROMPT, qw), BF16),
        compiler_params=_cparams(("arbitrary",)),
    )(sink, qkv, qkv, qkv)


def _na_union_start(rb):
    return jnp.clip(rb * NA_QROWS - NA_WIN_ROWS // 2, 0, GRID_ROWS - NA_UNION)


NA_DR_SLOTS = 2 * NA_WIN_ROWS


def _na_bias(pairs_ref, h, rb):
    u0 = _na_union_start(rb)
    left = lax.broadcasted_iota(jnp.int32, (1, LANES), 1) < GRID_W
    rows = []
    for j in range(NA_QROWS):
        r = rb * NA_QROWS + j
        start = jnp.clip(r - NA_WIN_ROWS // 2, 0, GRID_ROWS - NA_WIN_ROWS)
        blocks = []
        for ip in range(NA_UNION // 2):
            key_row = u0 + 2 * ip
            dr = key_row - r + NA_WIN_ROWS - 1
            ok_l = ((key_row >= start) & (key_row < start + NA_WIN_ROWS)).astype(jnp.int32)
            ok_r = ((key_row + 1 >= start) & (key_row + 1 < start + NA_WIN_ROWS)).astype(jnp.int32)
            blk = pairs_ref[h, jnp.clip(dr + 1, 0, NA_DR_SLOTS - 1)]
            blocks.append(jnp.where(jnp.where(left, ok_l, ok_r) > 0, blk, NEG_INF))
        rows.append(jnp.concatenate(blocks, axis=1))
    return jnp.concatenate(rows, axis=0)


def _na_latent_kernel(q_ref, k_ref, v_ref, kc_ref, vc_ref, pairs_ref, o_ref):
    rb = pl.program_id(2)
    start = pl.multiple_of(_na_union_start(rb) * GRID_W, GRID_W)
    n_loc = NA_UNION * GRID_W
    for p in range(NA_PAIRS_PER_STEP):
        cols = slice(p * LANES, (p + 1) * LANES)
        k_loc = k_ref[pl.ds(start, n_loc), cols]
        v_loc = v_ref[pl.ds(start, n_loc), cols]
        o_ref[:, cols] = _pair_attention(
            q_ref[:, cols], _half_masks(),
            [k_loc, kc_ref[0, :, cols].astype(BF16)], [v_loc, vc_ref[0, :, cols].astype(BF16)],
            [lambda h, p=p: _na_bias(pairs_ref, 2 * p + h, rb), None], 1.0, None)


def _na_latent_attn(qkv, k_ctx, v_ctx, pairs):
    qw = NA_HEADS * NA_HEAD_DIM
    width = NA_PAIRS_PER_STEP * LANES
    n_groups = qw // width
    q_rows = NA_QROWS * GRID_W
    n_rb = DEC_SEQ // q_rows
    q_off = N_PROMPT // q_rows
    kv_off = N_PROMPT // DEC_SEQ
    return pl.pallas_call(
        _na_latent_kernel,
        grid=(DEC_BATCH, n_groups, n_rb),
        in_specs=[
            pl.BlockSpec((q_rows, width), lambda b, j, r: (q_off + b * n_rb + r, j)),
            pl.BlockSpec((DEC_SEQ, width), lambda b, j, r: (kv_off + b, n_groups + j)),
            pl.BlockSpec((DEC_SEQ, width), lambda b, j, r: (kv_off + b, 2 * n_groups + j)),
            pl.BlockSpec((1, PAST_LEN, width), lambda b, j, r: (b, 0, j)),
            pl.BlockSpec((1, PAST_LEN, width), lambda b, j, r: (b, 0, j)),
            pl.BlockSpec((2 * NA_PAIRS_PER_STEP, NA_DR_SLOTS, GRID_W, LANES), lambda b, j, r: (j, 0, 0, 0)),
        ],
        out_specs=pl.BlockSpec((q_rows, width), lambda b, j, r: (b * n_rb + r, j)),
        out_shape=jax.ShapeDtypeStruct((N_SAMPLE, qw), BF16),
        compiler_params=_cparams(("arbitrary", "arbitrary", "arbitrary")),
    )(qkv, qkv, qkv, k_ctx, v_ctx, pairs)


def _na_pair_table(rel_bias):
    cols = np.arange(GRID_W)
    col_start = np.clip(cols - NA_WIN_COLS // 2, 0, GRID_W - NA_WIN_COLS)
    col_ok = (cols[None, :] >= col_start[:, None]) & (cols[None, :] < col_start[:, None] + NA_WIN_COLS)
    dc = np.clip(cols[None, :] - cols[:, None] + NA_WIN_COLS - 1, 0, 2 * NA_WIN_COLS - 2)
    n_dc = 2 * NA_WIN_COLS - 1
    n_dr = 2 * NA_WIN_ROWS - 1
    pick = np.zeros((n_dc, GRID_W * GRID_W), np.float32)
    pick[dc.reshape(-1), np.arange(GRID_W * GRID_W)] = 1.0
    flat = jnp.dot(rel_bias.astype(F32).reshape(NA_HEADS * n_dr, n_dc), jnp.asarray(pick),
                   precision=lax.Precision.HIGHEST)
    bias = jnp.where(col_ok.reshape(1, 1, -1), flat.reshape(NA_HEADS, n_dr, -1) * LOG2E, NEG_INF)
    bias = bias.reshape(NA_HEADS, n_dr, GRID_W, GRID_W)
    masked = jnp.full((NA_HEADS, 1, GRID_W, GRID_W), NEG_INF, F32)
    ext = jnp.concatenate([masked, bias, masked], axis=1)
    return jnp.concatenate([ext[:, :-1], ext[:, 1:]], axis=-1)


GQA_Q_BLOCK = GQA_WINDOW
GQA_GROUP = GQA_HEADS // GQA_KV_HEADS


def _gqa_latent_kernel(sink_ref, q_ref, kp_ref, kc_ref, kn_ref, vp_ref, vc_ref, vn_ref,
                       kx_ref, vx_ref, o_ref):
    pair = pl.program_id(1)
    qb = pl.program_id(2)
    n_qb = pl.num_programs(2)
    blk = GQA_Q_BLOCK
    rows = GQA_GROUP * blk
    qq = lax.broadcasted_iota(jnp.int32, (rows, blk), 0) % blk
    kk = lax.broadcasted_iota(jnp.int32, (rows, blk), 1)
    neg = jnp.float32(NEG_INF)
    bias_prev = jnp.where((kk >= qq) & (qb > 0), 0.0, neg)
    bias_next = jnp.where((kk <= qq) & (qb < n_qb - 1), 0.0, neg)
    row_grp = lax.broadcasted_iota(jnp.int32, (rows, 1), 0) // blk
    q = jnp.concatenate([q_ref[:, m * LANES:(m + 1) * LANES] for m in range(GQA_GROUP)], axis=0)
    sinks = []
    for h in range(2):
        base = pair * 2 * GQA_GROUP + h * GQA_GROUP
        col = jnp.zeros((rows, 1), F32)
        for m in range(GQA_GROUP):
            col = jnp.where(row_grp == m, sink_ref[base + m] * LOG2E, col)
        sinks.append(col)
    o = _pair_attention(
        q, _half_masks(),
        [kp_ref[...], kc_ref[...], kn_ref[...], kx_ref[0].astype(BF16)],
        [vp_ref[...], vc_ref[...], vn_ref[...], vx_ref[0].astype(BF16)],
        [lambda h: bias_prev, None, lambda h: bias_next, None], 1.0, sinks)
    for m in range(GQA_GROUP):
        o_ref[:, m * LANES:(m + 1) * LANES] = o[m * blk:(m + 1) * blk]


def _gqa_latent_attn(qkv, sink, k_ctx, v_ctx):
    qw = GQA_HEADS * GQA_HEAD_DIM
    blk = GQA_Q_BLOCK
    n_qb = DEC_SEQ // blk
    q_off = N_PROMPT // blk
    q_cols = GQA_GROUP * LANES
    n_pairs = GQA_KV_HEADS // 2
    k_col = qw // LANES
    v_col = k_col + n_pairs

    def kv_spec(col0, shift):
        def imap(b, p, i, s):
            return (q_off + b * n_qb + jnp.clip(i + shift, 0, n_qb - 1), col0 + p)
        return pl.BlockSpec((blk, LANES), imap)

    ctx_spec = pl.BlockSpec((1, PAST_LEN, LANES), lambda b, p, i, s: (b, 0, p))
    return pl.pallas_call(
        _gqa_latent_kernel,
        grid_spec=pltpu.PrefetchScalarGridSpec(
            num_scalar_prefetch=1,
            grid=(DEC_BATCH, n_pairs, n_qb),
            in_specs=[
                pl.BlockSpec((blk, q_cols), lambda b, p, i, s: (q_off + b * n_qb + i, p)),
                kv_spec(k_col, -1), kv_spec(k_col, 0), kv_spec(k_col, 1),
                kv_spec(v_col, -1), kv_spec(v_col, 0), kv_spec(v_col, 1),
                ctx_spec, ctx_spec,
            ],
            out_specs=pl.BlockSpec((blk, q_cols), lambda b, p, i, s: (b * n_qb + i, p)),
        ),
        out_shape=jax.ShapeDtypeStruct((N_SAMPLE, qw), BF16),
        compiler_params=_cparams(("arbitrary", "arbitrary", "arbitrary")),
    )(sink, qkv, qkv, qkv, qkv, qkv, qkv, qkv, k_ctx, v_ctx)


MLA_PAIRS = MLA_HEADS // 2


def _mla_masks():
    lane = lax.broadcasted_iota(jnp.int32, (1, 2 * LANES), 1)
    half = LANES // 2
    even = (lane < half) | ((lane >= LANES) & (lane < LANES + MLA_ROPE))
    odd = ((lane >= half) & (lane < LANES)) | ((lane >= LANES + MLA_ROPE) & (lane < LANES + 2 * MLA_ROPE))
    return [even, odd]


def _mla_ctx_kernel(q_ref, kv_ref, kpe_ref, o_ref):
    masks = _mla_masks()
    kpe = kpe_ref[...].astype(BF16)
    half = MLA_Q_COLS // 2
    for j in range(MLA_PAIRS):
        cols = slice(j * LANES, (j + 1) * LANES)
        rcols = slice(half + j * LANES, half + (j + 1) * LANES)
        q = jnp.concatenate([q_ref[:, cols], q_ref[:, rcols]], axis=1)
        k = jnp.concatenate([kv_ref[:, cols], kpe], axis=1)
        o_ref[:, cols] = _pair_attention(q, masks, [k], [kv_ref[:, rcols]], [None], 1.0, None)


def _mla_ctx_attn(q, kv, kpe):
    ow = MLA_HEADS * MLA_V
    return pl.pallas_call(
        _mla_ctx_kernel,
        grid=(BATCH,),
        in_specs=[
            pl.BlockSpec((SEQ, MLA_Q_COLS), lambda b: (b, 0)),
            pl.BlockSpec((SEQ, MLA_Q_COLS), lambda b: (b, 0)),
            pl.BlockSpec((SEQ, LANES), lambda b: (b, 0)),
        ],
        out_specs=pl.BlockSpec((SEQ, ow), lambda b: (b, 0)),
        out_shape=jax.ShapeDtypeStruct((N_PROMPT, ow), BF16),
        compiler_params=_cparams(("arbitrary",)),
    )(q, kv, kpe)


MLA_Q_BLOCK = 256


MLA_PAIRS_PER_STEP = 2


def _mla_latent_kernel(qn_ref, qr_ref, ka_ref, kpe_ref, v_ref, o_ref, kcat_ref):
    @pl.when(pl.program_id(2) == 0)
    def _():
        kpe = kpe_ref[...].astype(BF16)
        for p in range(MLA_PAIRS_PER_STEP):
            kcat_ref[p, :, :LANES] = ka_ref[:, p * LANES:(p + 1) * LANES]
            kcat_ref[p, :, LANES:] = kpe

    for p in range(MLA_PAIRS_PER_STEP):
        cols = slice(p * LANES, (p + 1) * LANES)
        q = jnp.concatenate([qn_ref[:, cols], qr_ref[:, cols]], axis=1)
        o_ref[:, cols] = _pair_attention(q, _mla_masks(), [kcat_ref[p]], [v_ref[:, cols]], [None], 1.0, None)


def _mla_latent_attn(q, kv, kpe):
    ow = MLA_HEADS * MLA_V
    n_qb = DEC_SEQ // MLA_Q_BLOCK
    q_off = N_PROMPT // MLA_Q_BLOCK
    width = MLA_PAIRS_PER_STEP * LANES
    n_groups = MLA_PAIRS // MLA_PAIRS_PER_STEP
    return pl.pallas_call(
        _mla_latent_kernel,
        grid=(DEC_BATCH, n_groups, n_qb),
        in_specs=[
            pl.BlockSpec((MLA_Q_BLOCK, width), lambda b, j, i: (q_off + b * n_qb + i, j)),
            pl.BlockSpec((MLA_Q_BLOCK, width), lambda b, j, i: (q_off + b * n_qb + i, n_groups + j)),
            pl.BlockSpec((MLA_KEYS, width), lambda b, j, i: (b, j)),
            pl.BlockSpec((MLA_KEYS, LANES), lambda b, j, i: (b, 0)),
            pl.BlockSpec((MLA_KEYS, width), lambda b, j, i: (b, n_groups + j)),
        ],
        out_specs=pl.BlockSpec((MLA_Q_BLOCK, width), lambda b, j, i: (b * n_qb + i, j)),
        out_shape=jax.ShapeDtypeStruct((N_SAMPLE, ow), BF16),
        scratch_shapes=[pltpu.VMEM((MLA_PAIRS_PER_STEP, MLA_KEYS, 2 * LANES), BF16)],
        compiler_params=_cparams(("arbitrary", "arbitrary", "arbitrary")),
    )(q, q, kv, kpe, kv)


def _mixer_residual(x_ref, op_ref, os_ref, gate_m_ref, wo_ref):
    is_prompt = pl.program_id(0) < N_PROMPT // ROW_TILE
    o = jnp.where(is_prompt, op_ref[...], os_ref[...])
    return x_ref[...] + gate_m_ref[0] * _dot(o, wo_ref[...])


def _attn_out_specs(k):
    n_p = N_PROMPT // ROW_TILE
    return [
        pl.BlockSpec((ROW_TILE, k), lambda i: (jnp.minimum(i, n_p - 1), 0)),
        pl.BlockSpec((ROW_TILE, k), lambda i: (jnp.maximum(i - n_p, 0), 0)),
        _mod_spec(ROW_TILE),
    ]


def _ffn_kernel(x_ref, op_ref, os_ref, gate_m_ref, wo_ref, g_ref, sh_ref, sc_ref, gate_ref,
                wg_ref, wu_ref, wd_ref, y_ref, acc_ref):
    x = _mixer_residual(x_ref, op_ref, os_ref, gate_m_ref, wo_ref)
    hb = (_rms(x, g_ref[...]) * (1.0 + sc_ref[0]) + sh_ref[0]).astype(BF16)
    acc_ref[...] = jnp.zeros_like(acc_ref)

    def chunk(c, carry):
        a = _silu(_dot(hb, wg_ref[c])) * _dot(hb, wu_ref[c])
        acc_ref[...] += _dot(a.astype(BF16), wd_ref[c])
        return carry

    lax.fori_loop(0, N_FFN_CHUNKS, chunk, 0)
    y_ref[...] = x + gate_ref[0] * acc_ref[...]


def _ffn(x, o_prompt, o_sample, gate_m, w_o, gain, shift, scale, gate, wg, wu, wd):
    return pl.pallas_call(
        _ffn_kernel,
        grid=(N_TOK // ROW_TILE,),
        in_specs=[
            pl.BlockSpec((ROW_TILE, D_MODEL), lambda i: (i, 0)),
            *_attn_out_specs(w_o.shape[0]),
            _resident(w_o.shape),
            pl.BlockSpec((1, D_MODEL), lambda i: (0, 0)),
            _mod_spec(ROW_TILE), _mod_spec(ROW_TILE), _mod_spec(ROW_TILE),
            _resident(wg.shape), _resident(wu.shape), _resident(wd.shape),
        ],
        out_specs=pl.BlockSpec((ROW_TILE, D_MODEL), lambda i: (i, 0)),
        out_shape=jax.ShapeDtypeStruct((N_TOK, D_MODEL), F32),
        scratch_shapes=[pltpu.VMEM((ROW_TILE, D_MODEL), F32)],
        compiler_params=_cparams(("arbitrary",)),
    )(x, o_prompt, o_sample, gate_m, w_o, gain, shift, scale, gate, wg, wu, wd)


SLAB = D_MODEL // LANES


def _store_slabs(ref, x):
    n = x.shape[0]
    for s in range(SLAB):
        ref[pl.ds(s, n, stride=SLAB), :] = x[:, s * LANES:(s + 1) * LANES]


def _load_slabs(ref, n, dtype, row0=0):
    return jnp.concatenate([ref[pl.ds(row0 + s, n, stride=SLAB), :].astype(dtype) for s in range(SLAB)], axis=1)


META_E1, META_E2, META_W1, META_W2, META_R1, META_R2 = range(6)


def _router_kernel(x_ref, op_ref, os_ref, gate_m_ref, wo_ref, g_ref, sh_ref, sc_ref, whi_ref, wlo_ref, tri_ref,
                   x1_ref, h_ref, meta_ref, cnt_ref, carry_ref):
    @pl.when(pl.program_id(0) == 0)
    def _():
        carry_ref[...] = jnp.zeros_like(carry_ref)

    x = _mixer_residual(x_ref, op_ref, os_ref, gate_m_ref, wo_ref)
    x1_ref[...] = x
    h = _rms(x, g_ref[...]) * (1.0 + sc_ref[0]) + sh_ref[0]
    _store_slabs(h_ref, h)
    h_hi, h_lo = _split_bf16(h)
    logits = _dot(h_hi, whi_ref[...]) + _dot(h_lo, whi_ref[...]) + _dot(h_hi, wlo_ref[...])
    lane = lax.broadcasted_iota(jnp.int32, logits.shape, 1)
    lane_f = lane.astype(F32)
    ninf = jnp.float32(-jnp.inf)
    lg = jnp.where(lane < N_EXPERTS, logits, ninf)
    m1 = jnp.max(lg, axis=-1, keepdims=True)
    i1 = jnp.min(jnp.where(lg == m1, lane_f, float(LANES)), axis=-1, keepdims=True)
    lg2 = jnp.where(lane_f == i1, ninf, lg)
    m2 = jnp.max(lg2, axis=-1, keepdims=True)
    i2 = jnp.min(jnp.where(lg2 == m2, lane_f, float(LANES)), axis=-1, keepdims=True)
    e = jnp.exp(m2 - m1)
    w1 = 1.0 / (1.0 + e)
    w2 = e / (1.0 + e)
    sel1 = lane_f == i1
    sel2 = lane_f == i2
    onehot = jnp.where(sel1 | sel2, 1.0, 0.0)
    ranks = _dot(tri_ref[...], onehot.astype(BF16)) + carry_ref[...]
    r1 = jnp.sum(jnp.where(sel1, ranks, 0.0), axis=-1, keepdims=True)
    r2 = jnp.sum(jnp.where(sel2, ranks, 0.0), axis=-1, keepdims=True)
    carry_ref[...] += jnp.sum(onehot, axis=0, keepdims=True)
    cnt_ref[...] = carry_ref[...]
    meta = jnp.zeros(logits.shape, F32)
    for idx, val in ((META_E1, i1), (META_E2, i2), (META_W1, w1), (META_W2, w2), (META_R1, r1), (META_R2, r2)):
        meta = jnp.where(lane == idx, val, meta)
    meta_ref[...] = meta


def _router(x, o_prompt, o_sample, gate_m, w_o, gain, shift, scale, w_hi, w_lo, tri):
    return pl.pallas_call(
        _router_kernel,
        grid=(N_TOK // ROW_TILE,),
        in_specs=[
            pl.BlockSpec((ROW_TILE, D_MODEL), lambda i: (i, 0)),
            *_attn_out_specs(w_o.shape[0]),
            _resident(w_o.shape),
            pl.BlockSpec((1, D_MODEL), lambda i: (0, 0)),
            _mod_spec(ROW_TILE), _mod_spec(ROW_TILE),
            _resident(w_hi.shape), _resident(w_lo.shape), _resident(tri.shape),
        ],
        out_specs=[
            pl.BlockSpec((ROW_TILE, D_MODEL), lambda i: (i, 0)),
            pl.BlockSpec((ROW_TILE * SLAB, LANES), lambda i: (i, 0)),
            pl.BlockSpec((ROW_TILE, LANES), lambda i: (i, 0)),
            pl.BlockSpec((1, LANES), lambda i: (0, 0)),
        ],
        out_shape=[
            jax.ShapeDtypeStruct((N_TOK, D_MODEL), F32),
            jax.ShapeDtypeStruct((N_TOK * SLAB, LANES), F32),
            jax.ShapeDtypeStruct((N_TOK, LANES), F32),
            jax.ShapeDtypeStruct((1, LANES), F32),
        ],
        scratch_shapes=[pltpu.VMEM((1, LANES), F32)],
        compiler_params=_cparams(("arbitrary",)),
    )(x, o_prompt, o_sample, gate_m, w_o, gain, shift, scale, w_hi, w_lo, tri)


def _route_plan(meta, counts):
    cnt = counts[0, :N_EXPERTS].astype(jnp.int32)
    tiles = (cnt + EXP_TILE - 1) // EXP_TILE
    tile_end = jnp.cumsum(tiles)
    tile_start = tile_end - tiles
    offs = tile_start * EXP_TILE
    experts = jnp.arange(N_EXPERTS, dtype=jnp.int32)

    def lookup(table, idx):
        return jnp.sum(jnp.where(idx[:, None] == experts[None, :], table[None, :], 0), axis=1)

    e1 = meta[:, META_E1].astype(jnp.int32)
    e2 = meta[:, META_E2].astype(jnp.int32)
    pos1 = lookup(offs, e1) + meta[:, META_R1].astype(jnp.int32)
    pos2 = lookup(offs, e2) + meta[:, META_R2].astype(jnp.int32)
    t = jnp.arange(N_EXP_TILES, dtype=jnp.int32)
    tile_expert = jnp.minimum(jnp.sum(t[:, None] >= tile_end[None, :], axis=1), N_EXPERTS - 1).astype(jnp.int32)
    in_group = (t - lookup(tile_start, tile_expert)) * EXP_TILE
    tile_rows = jnp.where(t < tile_end[-1], jnp.clip(lookup(cnt, tile_expert) - in_group, 0, EXP_TILE),
                          0).astype(jnp.int32)
    return pos1, pos2, tile_expert, tile_rows


DISPATCH_TILE = 512


def _dispatch_kernel(pos1_ref, pos2_ref, h_ref, init_ref, xs_ref, sem):
    del init_ref
    base = pl.program_id(0) * DISPATCH_TILE

    def row_copy(t, dst):
        return pltpu.make_async_copy(
            h_ref.at[pl.ds(pl.multiple_of(t * SLAB, SLAB), SLAB)],
            xs_ref.at[pl.ds(pl.multiple_of(dst * SLAB, SLAB), SLAB)], sem)

    def issue(t, carry):
        row_copy(t, pos1_ref[base + t]).start()
        row_copy(t, pos2_ref[base + t]).start()
        return carry

    lax.fori_loop(0, DISPATCH_TILE, issue, 0)

    def drain(t, carry):
        row_copy(0, 0).wait()
        row_copy(0, 0).wait()
        return carry

    lax.fori_loop(0, DISPATCH_TILE, drain, 0)


def _dispatch(pos1, pos2, h_slabs, xs_init):
    return pl.pallas_call(
        _dispatch_kernel,
        grid_spec=pltpu.PrefetchScalarGridSpec(
            num_scalar_prefetch=2,
            grid=(N_TOK // DISPATCH_TILE,),
            in_specs=[pl.BlockSpec((DISPATCH_TILE * SLAB, LANES), lambda i, p1, p2: (i, 0)),
                      pl.BlockSpec(memory_space=pl.ANY)],
            out_specs=pl.BlockSpec(memory_space=pl.ANY),
            scratch_shapes=[pltpu.SemaphoreType.DMA(())],
        ),
        out_shape=jax.ShapeDtypeStruct(xs_init.shape, F32),
        input_output_aliases={3: 0},
        compiler_params=_cparams(("arbitrary",)),
    )(pos1, pos2, h_slabs, xs_init)


def _expert_kernel(te_ref, tr_ref, xs_ref, wg_ref, wu_ref, wd_ref, ys_ref, xb_ref, acc_ref):
    t = pl.program_id(0)
    c = pl.program_id(1)

    @pl.when(c == 0)
    def _():
        xb_ref[...] = _load_slabs(xs_ref, EXP_TILE, BF16)
        acc_ref[...] = jnp.zeros_like(acc_ref)

    rows = tr_ref[t]

    def swiglu_rows(n):
        xb = xb_ref[:n]
        a = _silu(_dot(xb, wg_ref[...].astype(BF16))) * _dot(xb, wu_ref[...].astype(BF16))
        acc_ref[:n] += _dot(a.astype(BF16), wd_ref[...].astype(BF16))

    @pl.when(rows > EXP_PART)
    def _():
        swiglu_rows(EXP_TILE)

    @pl.when((rows > 0) & (rows <= EXP_PART))
    def _():
        swiglu_rows(EXP_PART)

    @pl.when(c == N_EXP_CHUNKS - 1)
    def _():
        _store_slabs(ys_ref, acc_ref[...])


def _experts(layer, tile_expert, tile_rows, xs, wg, wu, wd):
    def chunk_of(t, c, tr):
        return jnp.where(tr[t] > 0, c, N_EXP_CHUNKS - 1)

    return pl.pallas_call(
        _expert_kernel,
        grid_spec=pltpu.PrefetchScalarGridSpec(
            num_scalar_prefetch=2,
            grid=(N_EXP_TILES, N_EXP_CHUNKS),
            in_specs=[
                pl.BlockSpec((EXP_TILE * SLAB, LANES), lambda t, c, te, tr: (t, 0)),
                pl.BlockSpec((None, None, D_MODEL, EXP_CHUNK),
                             lambda t, c, te, tr: (layer, te[t], 0, chunk_of(t, c, tr))),
                pl.BlockSpec((None, None, D_MODEL, EXP_CHUNK),
                             lambda t, c, te, tr: (layer, te[t], 0, chunk_of(t, c, tr))),
                pl.BlockSpec((None, None, EXP_CHUNK, D_MODEL),
                             lambda t, c, te, tr: (layer, te[t], chunk_of(t, c, tr), 0)),
            ],
            out_specs=pl.BlockSpec((EXP_TILE * SLAB, LANES), lambda t, c, te, tr: (t, 0)),
            scratch_shapes=[pltpu.VMEM((EXP_TILE, D_MODEL), BF16), pltpu.VMEM((EXP_TILE, D_MODEL), F32)],
        ),
        out_shape=jax.ShapeDtypeStruct(xs.shape, F32),
        compiler_params=_cparams(("arbitrary", "arbitrary")),
    )(tile_expert, tile_rows, xs, wg, wu, wd)


def _combine_kernel(pos1_ref, pos2_ref, x_ref, gate_ref, meta_ref, ys_ref, y_ref, buf_ref, sem):
    i = pl.program_id(0)
    n = COMBINE_TILE
    slot = i % 2

    def fetch(step, dst):
        base = step * n

        def body(t, carry):
            for pos_ref, row in ((pos1_ref, t), (pos2_ref, n + t)):
                pltpu.make_async_copy(
                    ys_ref.at[pl.ds(pl.multiple_of(pos_ref[base + t] * SLAB, SLAB), SLAB)],
                    buf_ref.at[dst, pl.ds(pl.multiple_of(row * SLAB, SLAB), SLAB)], sem.at[dst]).start()
            return carry

        lax.fori_loop(0, n, body, 0)

    @pl.when(i == 0)
    def _():
        fetch(0, 0)

    @pl.when(i + 1 < pl.num_programs(0))
    def _():
        fetch(i + 1, 1 - slot)

    pltpu.make_async_copy(ys_ref.at[pl.ds(0, 2 * n * SLAB)], buf_ref.at[slot], sem.at[slot]).wait()
    meta = meta_ref[...]
    w1 = meta[:, META_W1:META_W1 + 1]
    w2 = meta[:, META_W2:META_W2 + 1]
    rows = buf_ref.at[slot]
    y1 = _load_slabs(rows, n, F32)
    y2 = _load_slabs(rows, n, F32, row0=n * SLAB)
    y_ref[...] = x_ref[...] + gate_ref[0] * (w1 * y1 + w2 * y2)


def _combine(pos1, pos2, x, gate, meta, ys):
    return pl.pallas_call(
        _combine_kernel,
        grid_spec=pltpu.PrefetchScalarGridSpec(
            num_scalar_prefetch=2,
            grid=(N_TOK // COMBINE_TILE,),
            in_specs=[
                pl.BlockSpec((COMBINE_TILE, D_MODEL), lambda i, p1, p2: (i, 0)),
                pl.BlockSpec((1, 1, D_MODEL), lambda i, p1, p2: (_cond_of_row(i * COMBINE_TILE), 0, 0)),
                pl.BlockSpec((COMBINE_TILE, LANES), lambda i, p1, p2: (i, 0)),
                pl.BlockSpec(memory_space=pl.ANY),
            ],
            out_specs=pl.BlockSpec((COMBINE_TILE, D_MODEL), lambda i, p1, p2: (i, 0)),
            scratch_shapes=[pltpu.VMEM((2, 2 * COMBINE_TILE * SLAB, LANES), F32), pltpu.SemaphoreType.DMA((2,))],
        ),
        out_shape=jax.ShapeDtypeStruct((N_TOK, D_MODEL), F32),
        compiler_params=_cparams(("arbitrary",)),
    )(pos1, pos2, x, gate, meta, ys)


def _moe(layer, x, o_prompt, o_sample, gate_m, w_o, gain, shift, scale, gate, router_w, wg, wu, wd, tri):
    w = jnp.pad(router_w, ((0, 0), (0, LANES - N_EXPERTS)))
    w_hi = w.astype(BF16)
    w_lo = (w - w_hi.astype(F32)).astype(BF16)
    x1, h_slabs, meta, counts = _router(x, o_prompt, o_sample, gate_m, w_o, gain, shift, scale, w_hi, w_lo, tri)
    pos1, pos2, tile_expert, tile_rows = _route_plan(meta, counts)
    xs = _dispatch(pos1, pos2, h_slabs, jnp.zeros((N_EXP_TILES * EXP_TILE * SLAB, LANES), F32))
    ys = _experts(layer, tile_expert, tile_rows, xs, wg, wu, wd)
    return _combine(pos1, pos2, x1, gate, meta, ys)


def _final_norm_kernel(x_ref, g_ref, y_ref):
    y_ref[...] = _rms(x_ref[...], g_ref[...])


def _final_norm(x, gain, row0, rows):
    off = row0 // ROW_TILE
    return pl.pallas_call(
        _final_norm_kernel,
        grid=(rows // ROW_TILE,),
        in_specs=[
            pl.BlockSpec((ROW_TILE, D_MODEL), lambda i: (off + i, 0)),
            pl.BlockSpec((1, D_MODEL), lambda i: (0, 0)),
        ],
        out_specs=pl.BlockSpec((ROW_TILE, D_MODEL), lambda i: (i, 0)),
        out_shape=jax.ShapeDtypeStruct((rows, D_MODEL), F32),
        compiler_params=_cparams(("arbitrary",)),
    )(x, gain)


def _rope_tables(rot_dim, reps):
    n_freq = rot_dim // 4
    inv_freq = ROPE_THETA ** (-np.arange(n_freq, dtype=np.float64) / n_freq)
    t = np.arange(DEC_SEQ)
    ang_r = (t // GRID_W)[:, None] * inv_freq
    ang_c = (t % GRID_W)[:, None] * inv_freq
    cos = np.concatenate([np.cos(ang_r), np.cos(ang_r), np.cos(ang_c), np.cos(ang_c)], axis=1)
    sin = np.concatenate([-np.sin(ang_r), np.sin(ang_r), -np.sin(ang_c), np.sin(ang_c)], axis=1)
    pad = LANES - reps * rot_dim
    cos = np.concatenate([cos] * reps + [np.ones((DEC_SEQ, pad))], axis=1)
    sin = np.concatenate([sin] * reps + [np.zeros((DEC_SEQ, pad))], axis=1)
    cos = np.concatenate([cos, np.ones((ROW_TILE, LANES))], axis=0)
    sin = np.concatenate([sin, np.zeros((ROW_TILE, LANES))], axis=0)
    return jnp.asarray(cos, F32), jnp.asarray(sin, F32)


def _swap_halves(w, rot_dim):
    k, n = w.shape
    q = rot_dim // 4
    return w.reshape(k, n // (2 * q), 2, q)[:, :, ::-1, :].reshape(k, n)


def _gqa_arrange(w_q, axis):
    pairs = GQA_KV_HEADS // 2
    shape = w_q.shape
    split = shape[:axis] + (pairs, 2, GQA_GROUP, GQA_HEAD_DIM) + shape[axis + 1:]
    order = list(range(len(split)))
    order[axis + 1], order[axis + 2] = order[axis + 2], order[axis + 1]
    return w_q.reshape(split).transpose(order).reshape(shape)


def _mla_layouts(wq_b, wkv_a, wkv_b):
    hd = MLA_NOPE + MLA_ROPE
    w3 = wq_b.reshape(MLA_Q_LORA, MLA_HEADS, hd)
    nope = w3[:, :, :MLA_NOPE].reshape(MLA_Q_LORA, MLA_HEADS * MLA_NOPE)
    rope = w3[:, :, MLA_NOPE:].reshape(MLA_Q_LORA, MLA_HEADS * MLA_ROPE)

    def rope_blocks(r):
        r = r.reshape(-1, MLA_PAIRS, 2 * MLA_ROPE)
        return jnp.pad(r, ((0, 0), (0, 0), (0, LANES - 2 * MLA_ROPE))).reshape(-1, MLA_PAIRS * LANES)

    wb = jnp.concatenate([nope, rope_blocks(rope)], axis=1)
    wbs = rope_blocks(_swap_halves(rope, MLA_ROPE))
    kpe_w = wkv_a[:, MLA_KV_LORA:]
    kpe_ws = _swap_halves(kpe_w, MLA_ROPE)
    zeros_a = jnp.zeros((D_MODEL, LANES - 2 * MLA_ROPE), F32)
    wa_tail = jnp.concatenate([wkv_a[:, :MLA_KV_LORA], kpe_w, kpe_w, zeros_a, kpe_ws, kpe_ws, zeros_a], axis=1)
    kv3 = wkv_b.reshape(MLA_KV_LORA, MLA_HEADS, MLA_NOPE + MLA_V)
    wkvb = jnp.concatenate([kv3[:, :, :MLA_NOPE].reshape(MLA_KV_LORA, -1),
                            kv3[:, :, MLA_NOPE:].reshape(MLA_KV_LORA, -1)], axis=1)
    return wb, wbs, wa_tail, wkvb


def _chunk_cols(w, chunk):
    k, n = w.shape
    return w.reshape(k, n // chunk, chunk).transpose(1, 0, 2).astype(BF16)


def kernel(x_prompt, x_sample, cache_l0_k, cache_l0_v, cache_l1_ckv, cache_l1_kpe, cache_l2_k, cache_l2_v, cache_l3_k, cache_l3_v, c, c_ctx, ada_w, ada_b, norm_mix, norm_ffn, norm_final, na_w_qkv, na_w_o, na_rel_bias, mla_wq_a, mla_q_norm, mla_wq_b, mla_wkv_a, mla_kv_norm, mla_wkv_b, mla_w_o, gqa_w_qkv, gqa_w_o, gqa_sink, ffn_w_gate, ffn_w_up, ffn_w_down, moe_router, moe_w_gate, moe_w_up, moe_w_down):
    x = jnp.concatenate([x_prompt.reshape(N_PROMPT, D_MODEL), x_sample.reshape(N_SAMPLE, D_MODEL)], axis=0)
    cond = jnp.concatenate([c_ctx[None, :], c, jnp.zeros((N_COND - 1 - DEC_BATCH, D_MODEL), F32)], axis=0)
    mods = _adaln(cond, ada_w, ada_b).reshape(DEPTH, N_COND, 6, 1, D_MODEL)
    tri = jnp.asarray(np.tril(np.ones((ROW_TILE, ROW_TILE), np.float32), -1), BF16)
    no_sink = jnp.zeros((GQA_HEADS,), F32)
    na_caches = {0: (cache_l0_k, cache_l0_v), 3: (cache_l3_k, cache_l3_v)}
    state = []

    for i in range(DEPTH):
        sh_m, sc_m, g_m, sh_f, sc_f, g_f = [mods[i, :, j] for j in range(6)]
        gain_m = norm_mix[i][None, :]
        gain_f = norm_ffn[i][None, :]
        kind, s = i % 3, i // 3
        if kind == 0:
            qw = NA_HEADS * NA_HEAD_DIM
            qkv, k_rows, v_rows = _norm_proj(x, gain_m, sh_m, sc_m, na_w_qkv[s].astype(BF16), qw)
            o_p = _ctx_attn(qkv, no_sink, qw, False)
            ck, cv = na_caches[i]
            o_s = _na_latent_attn(qkv, ck.reshape(DEC_BATCH, PAST_LEN, qw), cv.reshape(DEC_BATCH, PAST_LEN, qw),
                                  _na_pair_table(na_rel_bias[s]))
            w_o = na_w_o[s].astype(BF16)
            kv_shape = (BATCH, SEQ, NA_HEADS, NA_HEAD_DIM)
            state += [k_rows.reshape(kv_shape), v_rows.reshape(kv_shape)]
        elif kind == 1:
            wb, wbs, wa_tail, wkvb = _mla_layouts(mla_wq_b[s], mla_wkv_a[s], mla_wkv_b[s])
            wa = jnp.concatenate([mla_wq_a[s], wa_tail], axis=1).astype(BF16)
            cos, sin = _rope_tables(MLA_ROPE, 2)
            q, ckv, kpe = _mla_proj(x, gain_m, sh_m, sc_m, wa, mla_q_norm[s][None, :], mla_kv_norm[s][None, :],
                                    wb.astype(BF16), wbs.astype(BF16), cos, sin)
            wkvb = wkvb.astype(BF16)
            kv_p = _rows_matmul(ckv[:N_PROMPT], wkvb, BF16)
            o_p = _mla_ctx_attn(q, kv_p, kpe)
            ckv_s = ckv[N_PROMPT:].reshape(DEC_BATCH, DEC_SEQ, MLA_KV_LORA)
            ckv_all = jnp.concatenate([cache_l1_ckv, ckv_s], axis=1).reshape(DEC_BATCH * MLA_KEYS, MLA_KV_LORA)
            kv_s = _rows_matmul(ckv_all, wkvb, BF16)
            kpe_cache = jnp.concatenate(
                [cache_l1_kpe, cache_l1_kpe, jnp.zeros((DEC_BATCH, PAST_LEN, LANES - 2 * MLA_ROPE), F32)], axis=2)
            kpe_all = jnp.concatenate([kpe_cache, kpe[N_PROMPT:].reshape(DEC_BATCH, DEC_SEQ, LANES)], axis=1)
            o_s = _mla_latent_attn(q, kv_s, kpe_all.reshape(DEC_BATCH * MLA_KEYS, LANES))
            w_o = mla_w_o[s].astype(BF16)
            state += [ckv[:N_PROMPT].reshape(BATCH, SEQ, MLA_KV_LORA),
                      kpe[:N_PROMPT, :MLA_ROPE].reshape(BATCH, SEQ, MLA_ROPE)]
        else:
            qw = GQA_HEADS * GQA_HEAD_DIM
            kw = GQA_KV_HEADS * GQA_HEAD_DIM
            w_qkv = jnp.concatenate([_gqa_arrange(gqa_w_qkv[s][:, :qw], 1), gqa_w_qkv[s][:, qw:]], axis=1)
            w_swap = _swap_halves(w_qkv[:, :qw + kw], GQA_HEAD_DIM)
            cos, sin = _rope_tables(GQA_HEAD_DIM, 2)
            qkv, k_rows, v_rows = _norm_proj(x, gain_m, sh_m, sc_m, w_qkv.astype(BF16), kw,
                                             rope=(w_swap.astype(BF16), cos, sin))
            sink = gqa_sink[s]
            o_p = _ctx_attn(qkv, sink, kw, True)
            o_s = _gqa_latent_attn(qkv, sink, cache_l2_k.reshape(DEC_BATCH, PAST_LEN, kw),
                                   cache_l2_v.reshape(DEC_BATCH, PAST_LEN, kw))
            w_o = _gqa_arrange(gqa_w_o[s], 0).astype(BF16)
            kv_shape = (BATCH, SEQ, GQA_KV_HEADS, GQA_HEAD_DIM)
            state += [k_rows.reshape(kv_shape), v_rows.reshape(kv_shape)]
        s = i // 2
        if i % 2 == 0:
            x = _ffn(x, o_p, o_s, g_m, w_o, gain_f, sh_f, sc_f, g_f, _chunk_cols(ffn_w_gate[s], FFN_CHUNK),
                     _chunk_cols(ffn_w_up[s], FFN_CHUNK),
                     ffn_w_down[s].reshape(N_FFN_CHUNKS, FFN_CHUNK, D_MODEL).astype(BF16))
        else:
            x = _moe(s, x, o_p, o_s, g_m, w_o, gain_f, sh_f, sc_f, g_f, moe_router[s],
                     moe_w_gate, moe_w_up, moe_w_down, tri)

    y_prompt = _final_norm(x, norm_final[None, :], 0, N_PROMPT).reshape(BATCH, SEQ, D_MODEL)
    y_sample = _final_norm(x, norm_final[None, :], N_PROMPT, N_SAMPLE).reshape(DEC_BATCH, DEC_SEQ, D_MODEL)
    return (y_prompt, y_sample, *state)
```

```python
import functools
import math

import numpy as np
import jax
import jax.numpy as jnp
from jax import lax
from jax.experimental import pallas as pl
from jax.experimental.pallas import tpu as pltpu

F32 = jnp.float32
BF16 = jnp.bfloat16

D_MODEL = 1024
BATCH = 16
SEQ = 256
DEPTH = 4
DEC_BATCH = 4
DEC_SEQ = 2048
PAST_LEN = 512
GRID_W = 64
GRID_ROWS = DEC_SEQ // GRID_W
NORM_EPS = 1e-6
NEG_INF = -1e30
ROPE_THETA = 10000.0
NA_HEADS = 16
NA_HEAD_DIM = 64
NA_WIN_ROWS = 8
NA_WIN_COLS = 16
MLA_HEADS = 16
MLA_Q_LORA = 512
MLA_KV_LORA = 256
MLA_NOPE = 64
MLA_ROPE = 32
MLA_V = 64
GQA_HEADS = 16
GQA_KV_HEADS = 4
GQA_HEAD_DIM = 64
GQA_WINDOW = 128
FFN_DIM = 2816
N_EXPERTS = 8
EXPERT_DIM = 3584

N_PROMPT = BATCH * SEQ
N_SAMPLE = DEC_BATCH * DEC_SEQ
N_TOK = N_PROMPT + N_SAMPLE
N_COND = 8
LOG2E = math.log2(math.e)
HEAD_SCALE = NA_HEAD_DIM ** -0.5 * LOG2E
MLA_SCALE = (MLA_NOPE + MLA_ROPE) ** -0.5 * LOG2E

LANES = 128
SUBLANES = 8
VMEM_LIMIT = 56 * 1024 * 1024

ROW_TILE = 512
FFN_CHUNK = 256
N_FFN_CHUNKS = FFN_DIM // FFN_CHUNK
EXP_TILE = 1024
EXP_PART = 256
EXP_CHUNK = 512
N_EXP_CHUNKS = EXPERT_DIM // EXP_CHUNK
N_EXP_TILES = (2 * N_TOK) // EXP_TILE + N_EXPERTS
COMBINE_TILE = 256
NA_QROWS = 4
NA_UNION = NA_QROWS + NA_WIN_ROWS
NA_PAIRS_PER_STEP = 4
MLA_KEYS = PAST_LEN + DEC_SEQ


def _cparams(sem):
    return pltpu.CompilerParams(dimension_semantics=sem, vmem_limit_bytes=VMEM_LIMIT)


def _cond_of_row(row):
    return jnp.where(row < N_PROMPT, 0, 1 + (row - N_PROMPT) // DEC_SEQ)


def _nt_dot(a, b):
    return lax.dot_general(a, b, (((1,), (1,)), ((), ())), preferred_element_type=F32)


def _dot(a, b):
    return jnp.dot(a, b, preferred_element_type=F32)


def _rms(x, gain):
    return x * lax.rsqrt(jnp.mean(x * x, axis=-1, keepdims=True) + NORM_EPS) * gain


def _silu(x):
    return x / (1.0 + jnp.exp(-x))


def _split_bf16(x):
    hi = x.astype(BF16)
    lo = (x - hi.astype(F32)).astype(BF16)
    return hi, lo


def _resident(shape):
    nd = len(shape)
    return pl.BlockSpec(shape, lambda *_: (0,) * nd, pipeline_mode=pl.Buffered(1))


def _mod_spec(tile):
    return pl.BlockSpec((1, 1, D_MODEL), lambda i: (_cond_of_row(i * tile), 0, 0))


ADA_TILE = 1536


def _adaln_kernel(c_ref, w_ref, b_ref, o_ref):
    a_hi, a_lo = _split_bf16(_silu(c_ref[...]))
    w_hi, w_lo = _split_bf16(w_ref[0])
    o_ref[0] = _dot(a_hi, w_hi) + _dot(a_lo, w_hi) + _dot(a_hi, w_lo) + b_ref[0]


def _adaln(cond, ada_w, ada_b):
    n = 6 * D_MODEL
    return pl.pallas_call(
        _adaln_kernel,
        grid=(DEPTH, n // ADA_TILE),
        in_specs=[
            pl.BlockSpec((N_COND, D_MODEL), lambda l, j: (0, 0)),
            pl.BlockSpec((1, D_MODEL, ADA_TILE), lambda l, j: (l, 0, j)),
            pl.BlockSpec((1, 1, ADA_TILE), lambda l, j: (l, 0, j)),
        ],
        out_specs=pl.BlockSpec((1, N_COND, ADA_TILE), lambda l, j: (l, 0, j)),
        out_shape=jax.ShapeDtypeStruct((DEPTH, N_COND, n), F32),
        compiler_params=_cparams(("arbitrary", "arbitrary")),
    )(cond, ada_w, ada_b.reshape(DEPTH, 1, n))


PROJ_COLS = 512


def _norm_proj_kernel(x_ref, g_ref, sh_ref, sc_ref, w_ref, *rest, n_out, n_rope, kv_cols):
    rest = list(rest)
    if n_rope:
        w2_ref, cos_ref, sin_ref = rest[:3]
        rest = rest[3:]
    o_ref, k_ref, v_ref = rest
    h = _rms(x_ref[...], g_ref[...]) * (1.0 + sc_ref[0]) + sh_ref[0]
    hb = h.astype(BF16)
    is_prompt = pl.program_id(0) < N_PROMPT // ROW_TILE
    cuts = sorted({0, n_out, n_rope, n_out - 2 * kv_cols, n_out - kv_cols}
                  | set(range(0, n_out, PROJ_COLS)))
    for c0, c1 in zip(cuts[:-1], cuts[1:]):
        cols = slice(c0, c1)
        y = _dot(hb, w_ref[:, cols])
        if c0 < n_rope:
            ys = _dot(hb, w2_ref[:, cols])
            reps = (c1 - c0) // LANES
            cos = jnp.concatenate([cos_ref[...]] * reps, axis=1)
            sin = jnp.concatenate([sin_ref[...]] * reps, axis=1)
            y = y * cos + ys * sin
        is_query = c0 < n_out - 2 * kv_cols
        o_ref[:, cols] = (y * HEAD_SCALE if is_query else y).astype(BF16)
        for kv_ref, start in ((k_ref, n_out - 2 * kv_cols), (v_ref, n_out - kv_cols)):
            if start <= c0 < start + kv_cols:
                @pl.when(is_prompt)
                def _(kv_ref=kv_ref, start=start, y=y, c0=c0, c1=c1):
                    kv_ref[:, c0 - start:c1 - start] = y


def _rope_block(i):
    row = i * ROW_TILE
    return jnp.where(row < N_PROMPT, DEC_SEQ // ROW_TILE, ((row - N_PROMPT) % DEC_SEQ) // ROW_TILE)


def _norm_proj(x, gain, shift, scale, w, kv_cols, rope=None):
    n_out = w.shape[1]
    last_prompt = N_PROMPT // ROW_TILE - 1
    kv_spec = pl.BlockSpec((ROW_TILE, kv_cols), lambda i: (jnp.minimum(i, last_prompt), 0))
    kv_shape = jax.ShapeDtypeStruct((N_PROMPT, kv_cols), F32)
    in_specs = [
        pl.BlockSpec((ROW_TILE, D_MODEL), lambda i: (i, 0)),
        pl.BlockSpec((1, D_MODEL), lambda i: (0, 0)),
        _mod_spec(ROW_TILE),
        _mod_spec(ROW_TILE),
        _resident(w.shape),
    ]
    args = [x, gain, shift, scale, w]
    n_rope = 0
    if rope is not None:
        w2, cos, sin = rope
        n_rope = w2.shape[1]
        in_specs += [
            _resident(w2.shape),
            pl.BlockSpec((ROW_TILE, LANES), lambda i: (_rope_block(i), 0)),
            pl.BlockSpec((ROW_TILE, LANES), lambda i: (_rope_block(i), 0)),
        ]
        args += [w2, cos, sin]
    return pl.pallas_call(
        functools.partial(_norm_proj_kernel, n_out=n_out, n_rope=n_rope, kv_cols=kv_cols),
        grid=(N_TOK // ROW_TILE,),
        in_specs=in_specs,
        out_specs=[pl.BlockSpec((ROW_TILE, n_out), lambda i: (i, 0)), kv_spec, kv_spec],
        out_shape=[jax.ShapeDtypeStruct((N_TOK, n_out), BF16), kv_shape, kv_shape],
        compiler_params=_cparams(("arbitrary",)),
    )(*args)


MLA_Q_COLS = 2 * MLA_HEADS * MLA_NOPE


def _mla_proj_kernel(x_ref, g_ref, sh_ref, sc_ref, wa_ref, qn_ref, kvn_ref, wb_ref, wbs_ref,
                     cos_ref, sin_ref, q_ref, ckv_ref, kpe_ref):
    h = _rms(x_ref[...], g_ref[...]) * (1.0 + sc_ref[0]) + sh_ref[0]
    a = _dot(h.astype(BF16), wa_ref[...])
    c0 = MLA_Q_LORA
    c1 = c0 + MLA_KV_LORA
    cos = cos_ref[...]
    sin = sin_ref[...]
    ckv_ref[...] = _rms(a[:, c0:c1], kvn_ref[...])
    kpe_ref[...] = a[:, c1:c1 + LANES] * cos + a[:, c1 + LANES:c1 + 2 * LANES] * sin
    qn = _rms(a[:, :c0], qn_ref[...]).astype(BF16)
    half = MLA_Q_COLS // 2
    q_scale = MLA_SCALE
    q_ref[:, :half] = (_dot(qn, wb_ref[:, :half]) * q_scale).astype(BF16)
    q_rope = _dot(qn, wb_ref[:, half:])
    q_swap = _dot(qn, wbs_ref[...])
    reps = half // LANES
    cos_r = jnp.concatenate([cos] * reps, axis=1) * q_scale
    sin_r = jnp.concatenate([sin] * reps, axis=1) * q_scale
    q_ref[:, half:] = (q_rope * cos_r + q_swap * sin_r).astype(BF16)


def _mla_proj(x, gain, shift, scale, wa, q_norm, kv_norm, wb, wbs, cos, sin):
    return pl.pallas_call(
        _mla_proj_kernel,
        grid=(N_TOK // ROW_TILE,),
        in_specs=[
            pl.BlockSpec((ROW_TILE, D_MODEL), lambda i: (i, 0)),
            pl.BlockSpec((1, D_MODEL), lambda i: (0, 0)),
            _mod_spec(ROW_TILE),
            _mod_spec(ROW_TILE),
            _resident(wa.shape),
            pl.BlockSpec((1, MLA_Q_LORA), lambda i: (0, 0)),
            pl.BlockSpec((1, MLA_KV_LORA), lambda i: (0, 0)),
            _resident(wb.shape),
            _resident(wbs.shape),
            pl.BlockSpec((ROW_TILE, LANES), lambda i: (_rope_block(i), 0)),
            pl.BlockSpec((ROW_TILE, LANES), lambda i: (_rope_block(i), 0)),
        ],
        out_specs=[
            pl.BlockSpec((ROW_TILE, MLA_Q_COLS), lambda i: (i, 0)),
            pl.BlockSpec((ROW_TILE, MLA_KV_LORA), lambda i: (i, 0)),
            pl.BlockSpec((ROW_TILE, LANES), lambda i: (i, 0)),
        ],
        out_shape=[
            jax.ShapeDtypeStruct((N_TOK, MLA_Q_COLS), BF16),
            jax.ShapeDtypeStruct((N_TOK, MLA_KV_LORA), F32),
            jax.ShapeDtypeStruct((N_TOK, LANES), F32),
        ],
        compiler_params=_cparams(("arbitrary",)),
    )(x, gain, shift, scale, wa, q_norm, kv_norm, wb, wbs, cos, sin)


def _rows_matmul_kernel(x_ref, w_ref, o_ref):
    o_ref[...] = _dot(x_ref[...].astype(BF16), w_ref[...]).astype(o_ref.dtype)


def _rows_matmul(x, w, out_dtype):
    rows, k = x.shape
    n = w.shape[1]
    return pl.pallas_call(
        _rows_matmul_kernel,
        grid=(rows // ROW_TILE,),
        in_specs=[pl.BlockSpec((ROW_TILE, k), lambda i: (i, 0)), _resident(w.shape)],
        out_specs=pl.BlockSpec((ROW_TILE, n), lambda i: (i, 0)),
        out_shape=jax.ShapeDtypeStruct((rows, n), out_dtype),
        compiler_params=_cparams(("arbitrary",)),
    )(x, w)


def _pair_attention(q, masks, keys, values, biases, q_scale, sinks):
    half = LANES // 2
    lane = lax.broadcasted_iota(jnp.int32, (1, LANES), 1)
    outs = []
    sink_terms = []
    for h in range(2):
        qh = jnp.where(masks[h], q, jnp.zeros_like(q))
        if q_scale != 1.0:
            qh = qh * q_scale
        qh = qh.astype(BF16)
        scores = []
        for k, bias in zip(keys, biases):
            s = _nt_dot(qh, k)
            if bias is not None:
                s = s + bias(h)
            scores.append(s)
        m = functools.reduce(jnp.maximum, [jnp.max(s, axis=-1, keepdims=True) for s in scores])
        if sinks is not None:
            m = jnp.maximum(m, sinks[h])
        own = (lane < half) if h == 0 else (lane >= half)
        acc = None
        for s, v in zip(scores, values):
            part = _dot(jnp.exp2(s - m).astype(BF16), jnp.where(own, v, jnp.ones_like(v)))
            acc = part if acc is None else acc + part
        outs.append(acc)
        if sinks is not None:
            sink_terms.append(jnp.exp2(sinks[h] - m))
    first = lane < half
    numer = jnp.where(first, outs[0], outs[1])
    denom = pltpu.roll(jnp.where(first, outs[1], outs[0]), half, axis=1)
    if sinks is not None:
        denom = denom + jnp.where(first, sink_terms[0], sink_terms[1])
    return (numer / denom).astype(BF16)


def _half_masks(width=LANES):
    lane = lax.broadcasted_iota(jnp.int32, (1, width), 1)
    return [lane < LANES // 2, lane >= LANES // 2]


def _ctx_attn_kernel(sink_ref, q_ref, k_ref, v_ref, o_ref, *, kv_blocks, use_sink):
    masks = _half_masks()
    n_q_blocks = o_ref.shape[1] // LANES
    for qb in range(n_q_blocks):
        kb = qb * kv_blocks // n_q_blocks
        cols = slice(kb * LANES, (kb + 1) * LANES)
        sinks = None
        if use_sink:
            base = (qb // 4) * 8 + qb % 4
            sinks = [sink_ref[base] * LOG2E, sink_ref[base + 4] * LOG2E]
        o_ref[:, qb * LANES:(qb + 1) * LANES] = _pair_attention(
            q_ref[:, qb * LANES:(qb + 1) * LANES], masks,
            [k_ref[:, cols]], [v_ref[:, cols]], [None], 1.0, sinks)


def _ctx_attn(qkv, sink, kv_width, use_sink):
    qw = NA_HEADS * NA_HEAD_DIM
    k_blk = qw // kv_width
    return pl.pallas_call(
        functools.partial(_ctx_attn_kernel, kv_blocks=kv_width // LANES, use_sink=use_sink),
        grid_spec=pltpu.PrefetchScalarGridSpec(
            num_scalar_prefetch=1,
            grid=(BATCH,),
            in_specs=[
                pl.BlockSpec((SEQ, qw), lambda b, s: (b, 0)),
                pl.BlockSpec((SEQ, kv_width), lambda b, s: (b, k_blk)),
                pl.BlockSpec((SEQ, kv_width), lambda b, s: (b, k_blk + 1)),
            ],
            out_specs=pl.BlockSpec((SEQ, qw), lambda b, s: (b, 0)),
        ),
        out_shape=jax.ShapeDtypeStruct((N_PROMPT, qw), BF16),
        compiler_params=_cparams(("arbitrary",)),
    )(sink, qkv, qkv, qkv)


def _na_union_start(rb):
    return jnp.clip(rb * NA_QROWS - NA_WIN_ROWS // 2, 0, GRID_ROWS - NA_UNION)


NA_DR_SLOTS = 2 * NA_WIN_ROWS


def _na_bias(pairs_ref, h, rb):
    u0 = _na_union_start(rb)
    left = lax.broadcasted_iota(jnp.int32, (1, LANES), 1) < GRID_W
    rows = []
    for j in range(NA_QROWS):
        r = rb * NA_QROWS + j
        start = jnp.clip(r - NA_WIN_ROWS // 2, 0, GRID_ROWS - NA_WIN_ROWS)
        blocks = []
        for ip in range(NA_UNION // 2):
            key_row = u0 + 2 * ip
            dr = key_row - r + NA_WIN_ROWS - 1
            ok_l = ((key_row >= start) & (key_row < start + NA_WIN_ROWS)).astype(jnp.int32)
            ok_r = ((key_row + 1 >= start) & (key_row + 1 < start + NA_WIN_ROWS)).astype(jnp.int32)
            blk = pairs_ref[h, jnp.clip(dr + 1, 0, NA_DR_SLOTS - 1)]
            blocks.append(jnp.where(jnp.where(left, ok_l, ok_r) > 0, blk, NEG_INF))
        rows.append(jnp.concatenate(blocks, axis=1))
    return jnp.concatenate(rows, axis=0)


def _na_latent_kernel(q_ref, k_ref, v_ref, kc_ref, vc_ref, pairs_ref, o_ref):
    rb = pl.program_id(2)
    start = pl.multiple_of(_na_union_start(rb) * GRID_W, GRID_W)
    n_loc = NA_UNION * GRID_W
    for p in range(NA_PAIRS_PER_STEP):
        cols = slice(p * LANES, (p + 1) * LANES)
        k_loc = k_ref[pl.ds(start, n_loc), cols]
        v_loc = v_ref[pl.ds(start, n_loc), cols]
        o_ref[:, cols] = _pair_attention(
            q_ref[:, cols], _half_masks(),
            [k_loc, kc_ref[0, :, cols].astype(BF16)], [v_loc, vc_ref[0, :, cols].astype(BF16)],
            [lambda h, p=p: _na_bias(pairs_ref, 2 * p + h, rb), None], 1.0, None)


def _na_latent_attn(qkv, k_ctx, v_ctx, pairs):
    qw = NA_HEADS * NA_HEAD_DIM
    width = NA_PAIRS_PER_STEP * LANES
    n_groups = qw // width
    q_rows = NA_QROWS * GRID_W
    n_rb = DEC_SEQ // q_rows
    q_off = N_PROMPT // q_rows
    kv_off = N_PROMPT // DEC_SEQ
    return pl.pallas_call(
        _na_latent_kernel,
        grid=(DEC_BATCH, n_groups, n_rb),
        in_specs=[
            pl.BlockSpec((q_rows, width), lambda b, j, r: (q_off + b * n_rb + r, j)),
            pl.BlockSpec((DEC_SEQ, width), lambda b, j, r: (kv_off + b, n_groups + j)),
            pl.BlockSpec((DEC_SEQ, width), lambda b, j, r: (kv_off + b, 2 * n_groups + j)),
            pl.BlockSpec((1, PAST_LEN, width), lambda b, j, r: (b, 0, j)),
            pl.BlockSpec((1, PAST_LEN, width), lambda b, j, r: (b, 0, j)),
            pl.BlockSpec((2 * NA_PAIRS_PER_STEP, NA_DR_SLOTS, GRID_W, LANES), lambda b, j, r: (j, 0, 0, 0)),
        ],
        out_specs=pl.BlockSpec((q_rows, width), lambda b, j, r: (b * n_rb + r, j)),
        out_shape=jax.ShapeDtypeStruct((N_SAMPLE, qw), BF16),
        compiler_params=_cparams(("arbitrary", "arbitrary", "arbitrary")),
    )(qkv, qkv, qkv, k_ctx, v_ctx, pairs)


def _na_pair_table(rel_bias):
    cols = np.arange(GRID_W)
    col_start = np.clip(cols - NA_WIN_COLS // 2, 0, GRID_W - NA_WIN_COLS)
    col_ok = (cols[None, :] >= col_start[:, None]) & (cols[None, :] < col_start[:, None] + NA_WIN_COLS)
    dc = np.clip(cols[None, :] - cols[:, None] + NA_WIN_COLS - 1, 0, 2 * NA_WIN_COLS - 2)
    n_dc = 2 * NA_WIN_COLS - 1
    n_dr = 2 * NA_WIN_ROWS - 1
    pick = np.zeros((n_dc, GRID_W * GRID_W), np.float32)
    pick[dc.reshape(-1), np.arange(GRID_W * GRID_W)] = 1.0
    flat = jnp.dot(rel_bias.astype(F32).reshape(NA_HEADS * n_dr, n_dc), jnp.asarray(pick),
                   precision=lax.Precision.HIGHEST)
    bias = jnp.where(col_ok.reshape(1, 1, -1), flat.reshape(NA_HEADS, n_dr, -1) * LOG2E, NEG_INF)
    bias = bias.reshape(NA_HEADS, n_dr, GRID_W, GRID_W)
    masked = jnp.full((NA_HEADS, 1, GRID_W, GRID_W), NEG_INF, F32)
    ext = jnp.concatenate([masked, bias, masked], axis=1)
    return jnp.concatenate([ext[:, :-1], ext[:, 1:]], axis=-1)


GQA_Q_BLOCK = GQA_WINDOW
GQA_GROUP = GQA_HEADS // GQA_KV_HEADS


def _gqa_latent_kernel(sink_ref, q_ref, kp_ref, kc_ref, kn_ref, vp_ref, vc_ref, vn_ref,
                       kx_ref, vx_ref, o_ref):
    pair = pl.program_id(1)
    qb = pl.program_id(2)
    n_qb = pl.num_programs(2)
    blk = GQA_Q_BLOCK
    rows = GQA_GROUP * blk
    qq = lax.broadcasted_iota(jnp.int32, (rows, blk), 0) % blk
    kk = lax.broadcasted_iota(jnp.int32, (rows, blk), 1)
    neg = jnp.float32(NEG_INF)
    bias_prev = jnp.where((kk >= qq) & (qb > 0), 0.0, neg)
    bias_next = jnp.where((kk <= qq) & (qb < n_qb - 1), 0.0, neg)
    row_grp = lax.broadcasted_iota(jnp.int32, (rows, 1), 0) // blk
    q = jnp.concatenate([q_ref[:, m * LANES:(m + 1) * LANES] for m in range(GQA_GROUP)], axis=0)
    sinks = []
    for h in range(2):
        base = pair * 2 * GQA_GROUP + h * GQA_GROUP
        col = jnp.zeros((rows, 1), F32)
        for m in range(GQA_GROUP):
            col = jnp.where(row_grp == m, sink_ref[base + m] * LOG2E, col)
        sinks.append(col)
    o = _pair_attention(
        q, _half_masks(),
        [kp_ref[...], kc_ref[...], kn_ref[...], kx_ref[0].astype(BF16)],
        [vp_ref[...], vc_ref[...], vn_ref[...], vx_ref[0].astype(BF16)],
        [lambda h: bias_prev, None, lambda h: bias_next, None], 1.0, sinks)
    for m in range(GQA_GROUP):
        o_ref[:, m * LANES:(m + 1) * LANES] = o[m * blk:(m + 1) * blk]


def _gqa_latent_attn(qkv, sink, k_ctx, v_ctx):
    qw = GQA_HEADS * GQA_HEAD_DIM
    blk = GQA_Q_BLOCK
    n_qb = DEC_SEQ // blk
    q_off = N_PROMPT // blk
    q_cols = GQA_GROUP * LANES
    n_pairs = GQA_KV_HEADS // 2
    k_col = qw // LANES
    v_col = k_col + n_pairs

    def kv_spec(col0, shift):
        def imap(b, p, i, s):
            return (q_off + b * n_qb + jnp.clip(i + shift, 0, n_qb - 1), col0 + p)
        return pl.BlockSpec((blk, LANES), imap)

    ctx_spec = pl.BlockSpec((1, PAST_LEN, LANES), lambda b, p, i, s: (b, 0, p))
    return pl.pallas_call(
        _gqa_latent_kernel,
        grid_spec=pltpu.PrefetchScalarGridSpec(
            num_scalar_prefetch=1,
            grid=(DEC_BATCH, n_pairs, n_qb),
            in_specs=[
                pl.BlockSpec((blk, q_cols), lambda b, p, i, s: (q_off + b * n_qb + i, p)),
                kv_spec(k_col, -1), kv_spec(k_col, 0), kv_spec(k_col, 1),
                kv_spec(v_col, -1), kv_spec(v_col, 0), kv_spec(v_col, 1),
                ctx_spec, ctx_spec,
            ],
            out_specs=pl.BlockSpec((blk, q_cols), lambda b, p, i, s: (b * n_qb + i, p)),
        ),
        out_shape=jax.ShapeDtypeStruct((N_SAMPLE, qw), BF16),
        compiler_params=_cparams(("arbitrary", "arbitrary", "arbitrary")),
    )(sink, qkv, qkv, qkv, qkv, qkv, qkv, qkv, k_ctx, v_ctx)


MLA_PAIRS = MLA_HEADS // 2


def _mla_masks():
    lane = lax.broadcasted_iota(jnp.int32, (1, 2 * LANES), 1)
    half = LANES // 2
    even = (lane < half) | ((lane >= LANES) & (lane < LANES + MLA_ROPE))
    odd = ((lane >= half) & (lane < LANES)) | ((lane >= LANES + MLA_ROPE) & (lane < LANES + 2 * MLA_ROPE))
    return [even, odd]


def _mla_ctx_kernel(q_ref, kv_ref, kpe_ref, o_ref):
    masks = _mla_masks()
    kpe = kpe_ref[...].astype(BF16)
    half = MLA_Q_COLS // 2
    for j in range(MLA_PAIRS):
        cols = slice(j * LANES, (j + 1) * LANES)
        rcols = slice(half + j * LANES, half + (j + 1) * LANES)
        q = jnp.concatenate([q_ref[:, cols], q_ref[:, rcols]], axis=1)
        k = jnp.concatenate([kv_ref[:, cols], kpe], axis=1)
        o_ref[:, cols] = _pair_attention(q, masks, [k], [kv_ref[:, rcols]], [None], 1.0, None)


def _mla_ctx_attn(q, kv, kpe):
    ow = MLA_HEADS * MLA_V
    return pl.pallas_call(
        _mla_ctx_kernel,
        grid=(BATCH,),
        in_specs=[
            pl.BlockSpec((SEQ, MLA_Q_COLS), lambda b: (b, 0)),
            pl.BlockSpec((SEQ, MLA_Q_COLS), lambda b: (b, 0)),
            pl.BlockSpec((SEQ, LANES), lambda b: (b, 0)),
        ],
        out_specs=pl.BlockSpec((SEQ, ow), lambda b: (b, 0)),
        out_shape=jax.ShapeDtypeStruct((N_PROMPT, ow), BF16),
        compiler_params=_cparams(("arbitrary",)),
    )(q, kv, kpe)


MLA_Q_BLOCK = 256


MLA_PAIRS_PER_STEP = 4


def _mla_latent_kernel(qn_ref, qr_ref, ka_ref, kpe_ref, v_ref, o_ref, kcat_ref):
    @pl.when(pl.program_id(2) == 0)
    def _():
        kpe = kpe_ref[...].astype(BF16)
        for p in range(MLA_PAIRS_PER_STEP):
            kcat_ref[p, :, :LANES] = ka_ref[:, p * LANES:(p + 1) * LANES]
            kcat_ref[p, :, LANES:] = kpe

    for p in range(MLA_PAIRS_PER_STEP):
        cols = slice(p * LANES, (p + 1) * LANES)
        q = jnp.concatenate([qn_ref[:, cols], qr_ref[:, cols]], axis=1)
        o_ref[:, cols] = _pair_attention(q, _mla_masks(), [kcat_ref[p]], [v_ref[:, cols]], [None], 1.0, None)


def _mla_latent_attn(q, kv, kpe):
    ow = MLA_HEADS * MLA_V
    n_qb = DEC_SEQ // MLA_Q_BLOCK
    q_off = N_PROMPT // MLA_Q_BLOCK
    width = MLA_PAIRS_PER_STEP * LANES
    n_groups = MLA_PAIRS // MLA_PAIRS_PER_STEP
    return pl.pallas_call(
        _mla_latent_kernel,
        grid=(DEC_BATCH, n_groups, n_qb),
        in_specs=[
            pl.BlockSpec((MLA_Q_BLOCK, width), lambda b, j, i: (q_off + b * n_qb + i, j)),
            pl.BlockSpec((MLA_Q_BLOCK, width), lambda b, j, i: (q_off + b * n_qb + i, n_groups + j)),
            pl.BlockSpec((MLA_KEYS, width), lambda b, j, i: (b, j)),
            pl.BlockSpec((MLA_KEYS, LANES), lambda b, j, i: (b, 0)),
            pl.BlockSpec((MLA_KEYS, width), lambda b, j, i: (b, n_groups + j)),
        ],
        out_specs=pl.BlockSpec((MLA_Q_BLOCK, width), lambda b, j, i: (b * n_qb + i, j)),
        out_shape=jax.ShapeDtypeStruct((N_SAMPLE, ow), BF16),
        scratch_shapes=[pltpu.VMEM((MLA_PAIRS_PER_STEP, MLA_KEYS, 2 * LANES), BF16)],
        compiler_params=_cparams(("arbitrary", "arbitrary", "arbitrary")),
    )(q, q, kv, kpe, kv)


def _mixer_residual(x_ref, op_ref, os_ref, gate_m_ref, wo_ref):
    is_prompt = pl.program_id(0) < N_PROMPT // ROW_TILE
    o = jnp.where(is_prompt, op_ref[...], os_ref[...])
    return x_ref[...] + gate_m_ref[0] * _dot(o, wo_ref[...])


def _attn_out_specs(k):
    n_p = N_PROMPT // ROW_TILE
    return [
        pl.BlockSpec((ROW_TILE, k), lambda i: (jnp.minimum(i, n_p - 1), 0)),
        pl.BlockSpec((ROW_TILE, k), lambda i: (jnp.maximum(i - n_p, 0), 0)),
        _mod_spec(ROW_TILE),
    ]


def _ffn_kernel(x_ref, op_ref, os_ref, gate_m_ref, wo_ref, g_ref, sh_ref, sc_ref, gate_ref,
                wg_ref, wu_ref, wd_ref, y_ref, acc_ref):
    x = _mixer_residual(x_ref, op_ref, os_ref, gate_m_ref, wo_ref)
    hb = (_rms(x, g_ref[...]) * (1.0 + sc_ref[0]) + sh_ref[0]).astype(BF16)
    acc_ref[...] = jnp.zeros_like(acc_ref)

    def chunk(c, carry):
        a = _silu(_dot(hb, wg_ref[c])) * _dot(hb, wu_ref[c])
        acc_ref[...] += _dot(a.astype(BF16), wd_ref[c])
        return carry

    lax.fori_loop(0, N_FFN_CHUNKS, chunk, 0)
    y_ref[...] = x + gate_ref[0] * acc_ref[...]


def _ffn(x, o_prompt, o_sample, gate_m, w_o, gain, shift, scale, gate, wg, wu, wd):
    return pl.pallas_call(
        _ffn_kernel,
        grid=(N_TOK // ROW_TILE,),
        in_specs=[
            pl.BlockSpec((ROW_TILE, D_MODEL), lambda i: (i, 0)),
            *_attn_out_specs(w_o.shape[0]),
            _resident(w_o.shape),
            pl.BlockSpec((1, D_MODEL), lambda i: (0, 0)),
            _mod_spec(ROW_TILE), _mod_spec(ROW_TILE), _mod_spec(ROW_TILE),
            _resident(wg.shape), _resident(wu.shape), _resident(wd.shape),
        ],
        out_specs=pl.BlockSpec((ROW_TILE, D_MODEL), lambda i: (i, 0)),
        out_shape=jax.ShapeDtypeStruct((N_TOK, D_MODEL), F32),
        scratch_shapes=[pltpu.VMEM((ROW_TILE, D_MODEL), F32)],
        compiler_params=_cparams(("arbitrary",)),
    )(x, o_prompt, o_sample, gate_m, w_o, gain, shift, scale, gate, wg, wu, wd)


SLAB = D_MODEL // LANES


def _store_slabs(ref, x):
    n = x.shape[0]
    for s in range(SLAB):
        ref[pl.ds(s, n, stride=SLAB), :] = x[:, s * LANES:(s + 1) * LANES]


def _load_slabs(ref, n, dtype, row0=0):
    return jnp.concatenate([ref[pl.ds(row0 + s, n, stride=SLAB), :].astype(dtype) for s in range(SLAB)], axis=1)


META_E1, META_E2, META_W1, META_W2, META_R1, META_R2 = range(6)


def _router_kernel(x_ref, op_ref, os_ref, gate_m_ref, wo_ref, g_ref, sh_ref, sc_ref, whi_ref, wlo_ref, tri_ref,
                   x1_ref, h_ref, meta_ref, cnt_ref, carry_ref):
    @pl.when(pl.program_id(0) == 0)
    def _():
        carry_ref[...] = jnp.zeros_like(carry_ref)

    x = _mixer_residual(x_ref, op_ref, os_ref, gate_m_ref, wo_ref)
    x1_ref[...] = x
    h = _rms(x, g_ref[...]) * (1.0 + sc_ref[0]) + sh_ref[0]
    _store_slabs(h_ref, h)
    h_hi, h_lo = _split_bf16(h)
    logits = _dot(h_hi, whi_ref[...]) + _dot(h_lo, whi_ref[...]) + _dot(h_hi, wlo_ref[...])
    lane = lax.broadcasted_iota(jnp.int32, logits.shape, 1)
    lane_f = lane.astype(F32)
    ninf = jnp.float32(-jnp.inf)
    lg = jnp.where(lane < N_EXPERTS, logits, ninf)
    m1 = jnp.max(lg, axis=-1, keepdims=True)
    i1 = jnp.min(jnp.where(lg == m1, lane_f, float(LANES)), axis=-1, keepdims=True)
    lg2 = jnp.where(lane_f == i1, ninf, lg)
    m2 = jnp.max(lg2, axis=-1, keepdims=True)
    i2 = jnp.min(jnp.where(lg2 == m2, lane_f, float(LANES)), axis=-1, keepdims=True)
    e = jnp.exp(m2 - m1)
    w1 = 1.0 / (1.0 + e)
    w2 = e / (1.0 + e)
    sel1 = lane_f == i1
    sel2 = lane_f == i2
    onehot = jnp.where(sel1 | sel2, 1.0, 0.0)
    ranks = _dot(tri_ref[...], onehot.astype(BF16)) + carry_ref[...]
    r1 = jnp.sum(jnp.where(sel1, ranks, 0.0), axis=-1, keepdims=True)
    r2 = jnp.sum(jnp.where(sel2, ranks, 0.0), axis=-1, keepdims=True)
    carry_ref[...] += jnp.sum(onehot, axis=0, keepdims=True)
    cnt_ref[...] = carry_ref[...]
    meta = jnp.zeros(logits.shape, F32)
    for idx, val in ((META_E1, i1), (META_E2, i2), (META_W1, w1), (META_W2, w2), (META_R1, r1), (META_R2, r2)):
        meta = jnp.where(lane == idx, val, meta)
    meta_ref[...] = meta


def _router(x, o_prompt, o_sample, gate_m, w_o, gain, shift, scale, w_hi, w_lo, tri):
    return pl.pallas_call(
        _router_kernel,
        grid=(N_TOK // ROW_TILE,),
        in_specs=[
            pl.BlockSpec((ROW_TILE, D_MODEL), lambda i: (i, 0)),
            *_attn_out_specs(w_o.shape[0]),
            _resident(w_o.shape),
            pl.BlockSpec((1, D_MODEL), lambda i: (0, 0)),
            _mod_spec(ROW_TILE), _mod_spec(ROW_TILE),
            _resident(w_hi.shape), _resident(w_lo.shape), _resident(tri.shape),
        ],
        out_specs=[
            pl.BlockSpec((ROW_TILE, D_MODEL), lambda i: (i, 0)),
            pl.BlockSpec((ROW_TILE * SLAB, LANES), lambda i: (i, 0)),
            pl.BlockSpec((ROW_TILE, LANES), lambda i: (i, 0)),
            pl.BlockSpec((1, LANES), lambda i: (0, 0)),
        ],
        out_shape=[
            jax.ShapeDtypeStruct((N_TOK, D_MODEL), F32),
            jax.ShapeDtypeStruct((N_TOK * SLAB, LANES), F32),
            jax.ShapeDtypeStruct((N_TOK, LANES), F32),
            jax.ShapeDtypeStruct((1, LANES), F32),
        ],
        scratch_shapes=[pltpu.VMEM((1, LANES), F32)],
        compiler_params=_cparams(("arbitrary",)),
    )(x, o_prompt, o_sample, gate_m, w_o, gain, shift, scale, w_hi, w_lo, tri)


def _route_plan(meta, counts):
    cnt = counts[0, :N_EXPERTS].astype(jnp.int32)
    tiles = (cnt + EXP_TILE - 1) // EXP_TILE
    tile_end = jnp.cumsum(tiles)
    tile_start = tile_end - tiles
    offs = tile_start * EXP_TILE
    experts = jnp.arange(N_EXPERTS, dtype=jnp.int32)

    def lookup(table, idx):
        return jnp.sum(jnp.where(idx[:, None] == experts[None, :], table[None, :], 0), axis=1)

    e1 = meta[:, META_E1].astype(jnp.int32)
    e2 = meta[:, META_E2].astype(jnp.int32)
    pos1 = lookup(offs, e1) + meta[:, META_R1].astype(jnp.int32)
    pos2 = lookup(offs, e2) + meta[:, META_R2].astype(jnp.int32)
    t = jnp.arange(N_EXP_TILES, dtype=jnp.int32)
    tile_expert = jnp.minimum(jnp.sum(t[:, None] >= tile_end[None, :], axis=1), N_EXPERTS - 1).astype(jnp.int32)
    in_group = (t - lookup(tile_start, tile_expert)) * EXP_TILE
    tile_rows = jnp.where(t < tile_end[-1], jnp.clip(lookup(cnt, tile_expert) - in_group, 0, EXP_TILE),
                          0).astype(jnp.int32)
    return pos1, pos2, tile_expert, tile_rows


DISPATCH_TILE = 512


def _dispatch_kernel(pos1_ref, pos2_ref, h_ref, init_ref, xs_ref, sem):
    del init_ref
    base = pl.program_id(0) * DISPATCH_TILE

    def row_copy(t, dst):
        return pltpu.make_async_copy(
            h_ref.at[pl.ds(pl.multiple_of(t * SLAB, SLAB), SLAB)],
            xs_ref.at[pl.ds(pl.multiple_of(dst * SLAB, SLAB), SLAB)], sem)

    def issue(t, carry):
        row_copy(t, pos1_ref[base + t]).start()
        row_copy(t, pos2_ref[base + t]).start(priority=1)
        return carry

    lax.fori_loop(0, DISPATCH_TILE, issue, 0)

    def drain(t, carry):
        row_copy(0, 0).wait()
        row_copy(0, 0).wait()
        return carry

    lax.fori_loop(0, DISPATCH_TILE, drain, 0)


def _dispatch(pos1, pos2, h_slabs, xs_init):
    return pl.pallas_call(
        _dispatch_kernel,
        grid_spec=pltpu.PrefetchScalarGridSpec(
            num_scalar_prefetch=2,
            grid=(N_TOK // DISPATCH_TILE,),
            in_specs=[pl.BlockSpec((DISPATCH_TILE * SLAB, LANES), lambda i, p1, p2: (i, 0)),
                      pl.BlockSpec(memory_space=pl.ANY)],
            out_specs=pl.BlockSpec(memory_space=pl.ANY),
            scratch_shapes=[pltpu.SemaphoreType.DMA(())],
        ),
        out_shape=jax.ShapeDtypeStruct(xs_init.shape, F32),
        input_output_aliases={3: 0},
        compiler_params=_cparams(("arbitrary",)),
    )(pos1, pos2, h_slabs, xs_init)


def _expert_kernel(te_ref, tr_ref, xs_ref, wg_ref, wu_ref, wd_ref, ys_ref, xb_ref, acc_ref):
    t = pl.program_id(0)
    c = pl.program_id(1)

    @pl.when(c == 0)
    def _():
        xb_ref[...] = _load_slabs(xs_ref, EXP_TILE, BF16)
        acc_ref[...] = jnp.zeros_like(acc_ref)

    rows = tr_ref[t]

    def swiglu_rows(n):
        xb = xb_ref[:n]
        a = _silu(_dot(xb, wg_ref[...].astype(BF16))) * _dot(xb, wu_ref[...].astype(BF16))
        acc_ref[:n] += _dot(a.astype(BF16), wd_ref[...].astype(BF16))

    for parts in range(1, EXP_TILE // EXP_PART + 1):
        @pl.when((rows > (parts - 1) * EXP_PART) & (rows <= parts * EXP_PART))
        def _(parts=parts):
            swiglu_rows(parts * EXP_PART)

    @pl.when(c == N_EXP_CHUNKS - 1)
    def _():
        _store_slabs(ys_ref, acc_ref[...])


def _experts(layer, tile_expert, tile_rows, xs, wg, wu, wd):
    def chunk_of(t, c, tr):
        return jnp.where(tr[t] > 0, c, N_EXP_CHUNKS - 1)

    return pl.pallas_call(
        _expert_kernel,
        grid_spec=pltpu.PrefetchScalarGridSpec(
            num_scalar_prefetch=2,
            grid=(N_EXP_TILES, N_EXP_CHUNKS),
            in_specs=[
                pl.BlockSpec((EXP_TILE * SLAB, LANES), lambda t, c, te, tr: (t, 0)),
                pl.BlockSpec((None, None, D_MODEL, EXP_CHUNK),
                             lambda t, c, te, tr: (layer, te[t], 0, chunk_of(t, c, tr))),
                pl.BlockSpec((None, None, D_MODEL, EXP_CHUNK),
                             lambda t, c, te, tr: (layer, te[t], 0, chunk_of(t, c, tr))),
                pl.BlockSpec((None, None, EXP_CHUNK, D_MODEL),
                             lambda t, c, te, tr: (layer, te[t], chunk_of(t, c, tr), 0)),
            ],
            out_specs=pl.BlockSpec((EXP_TILE * SLAB, LANES), lambda t, c, te, tr: (t, 0)),
            scratch_shapes=[pltpu.VMEM((EXP_TILE, D_MODEL), BF16), pltpu.VMEM((EXP_TILE, D_MODEL), F32)],
        ),
        out_shape=jax.ShapeDtypeStruct(xs.shape, F32),
        compiler_params=_cparams(("arbitrary", "arbitrary")),
    )(tile_expert, tile_rows, xs, wg, wu, wd)


def _combine_kernel(pos1_ref, pos2_ref, x_ref, gate_ref, meta_ref, ys_ref, y_ref, buf_ref, sem):
    i = pl.program_id(0)
    n = COMBINE_TILE
    slot = i % 2

    def fetch(step, dst):
        base = step * n

        def body(t, carry):
            for queue, (pos_ref, row) in enumerate(((pos1_ref, t), (pos2_ref, n + t))):
                pltpu.make_async_copy(
                    ys_ref.at[pl.ds(pl.multiple_of(pos_ref[base + t] * SLAB, SLAB), SLAB)],
                    buf_ref.at[dst, pl.ds(pl.multiple_of(row * SLAB, SLAB), SLAB)],
                    sem.at[dst]).start(priority=queue)
            return carry

        lax.fori_loop(0, n, body, 0)

    @pl.when(i == 0)
    def _():
        fetch(0, 0)

    @pl.when(i + 1 < pl.num_programs(0))
    def _():
        fetch(i + 1, 1 - slot)

    pltpu.make_async_copy(ys_ref.at[pl.ds(0, 2 * n * SLAB)], buf_ref.at[slot], sem.at[slot]).wait()
    meta = meta_ref[...]
    w1 = meta[:, META_W1:META_W1 + 1]
    w2 = meta[:, META_W2:META_W2 + 1]
    rows = buf_ref.at[slot]
    y1 = _load_slabs(rows, n, F32)
    y2 = _load_slabs(rows, n, F32, row0=n * SLAB)
    y_ref[...] = x_ref[...] + gate_ref[0] * (w1 * y1 + w2 * y2)


def _combine(pos1, pos2, x, gate, meta, ys):
    return pl.pallas_call(
        _combine_kernel,
        grid_spec=pltpu.PrefetchScalarGridSpec(
            num_scalar_prefetch=2,
            grid=(N_TOK // COMBINE_TILE,),
            in_specs=[
                pl.BlockSpec((COMBINE_TILE, D_MODEL), lambda i, p1, p2: (i, 0)),
                pl.BlockSpec((1, 1, D_MODEL), lambda i, p1, p2: (_cond_of_row(i * COMBINE_TILE), 0, 0)),
                pl.BlockSpec((COMBINE_TILE, LANES), lambda i, p1, p2: (i, 0)),
                pl.BlockSpec(memory_space=pl.ANY),
            ],
            out_specs=pl.BlockSpec((COMBINE_TILE, D_MODEL), lambda i, p1, p2: (i, 0)),
            scratch_shapes=[pltpu.VMEM((2, 2 * COMBINE_TILE * SLAB, LANES), F32), pltpu.SemaphoreType.DMA((2,))],
        ),
        out_shape=jax.ShapeDtypeStruct((N_TOK, D_MODEL), F32),
        compiler_params=_cparams(("arbitrary",)),
    )(pos1, pos2, x, gate, meta, ys)


def _moe(layer, x, o_prompt, o_sample, gate_m, w_o, gain, shift, scale, gate, router_w, wg, wu, wd, tri):
    w = jnp.pad(router_w, ((0, 0), (0, LANES - N_EXPERTS)))
    w_hi = w.astype(BF16)
    w_lo = (w - w_hi.astype(F32)).astype(BF16)
    x1, h_slabs, meta, counts = _router(x, o_prompt, o_sample, gate_m, w_o, gain, shift, scale, w_hi, w_lo, tri)
    pos1, pos2, tile_expert, tile_rows = _route_plan(meta, counts)
    xs = _dispatch(pos1, pos2, h_slabs, jnp.zeros((N_EXP_TILES * EXP_TILE * SLAB, LANES), F32))
    ys = _experts(layer, tile_expert, tile_rows, xs, wg, wu, wd)
    return _combine(pos1, pos2, x1, gate, meta, ys)


def _final_norm_kernel(x_ref, g_ref, y_ref):
    y_ref[...] = _rms(x_ref[...], g_ref[...])


def _final_norm(x, gain, row0, rows):
    off = row0 // ROW_TILE
    return pl.pallas_call(
        _final_norm_kernel,
        grid=(rows // ROW_TILE,),
        in_specs=[
            pl.BlockSpec((ROW_TILE, D_MODEL), lambda i: (off + i, 0)),
            pl.BlockSpec((1, D_MODEL), lambda i: (0, 0)),
        ],
        out_specs=pl.BlockSpec((ROW_TILE, D_MODEL), lambda i: (i, 0)),
        out_shape=jax.ShapeDtypeStruct((rows, D_MODEL), F32),
        compiler_params=_cparams(("arbitrary",)),
    )(x, gain)


def _rope_tables(rot_dim, reps):
    n_freq = rot_dim // 4
    inv_freq = ROPE_THETA ** (-np.arange(n_freq, dtype=np.float64) / n_freq)
    t = np.arange(DEC_SEQ)
    ang_r = (t // GRID_W)[:, None] * inv_freq
    ang_c = (t % GRID_W)[:, None] * inv_freq
    cos = np.concatenate([np.cos(ang_r), np.cos(ang_r), np.cos(ang_c), np.cos(ang_c)], axis=1)
    sin = np.concatenate([-np.sin(ang_r), np.sin(ang_r), -np.sin(ang_c), np.sin(ang_c)], axis=1)
    pad = LANES - reps * rot_dim
    cos = np.concatenate([cos] * reps + [np.ones((DEC_SEQ, pad))], axis=1)
    sin = np.concatenate([sin] * reps + [np.zeros((DEC_SEQ, pad))], axis=1)
    cos = np.concatenate([cos, np.ones((ROW_TILE, LANES))], axis=0)
    sin = np.concatenate([sin, np.zeros((ROW_TILE, LANES))], axis=0)
    return jnp.asarray(cos, F32), jnp.asarray(sin, F32)


def _swap_halves(w, rot_dim):
    k, n = w.shape
    q = rot_dim // 4
    return w.reshape(k, n // (2 * q), 2, q)[:, :, ::-1, :].reshape(k, n)


def _gqa_arrange(w_q, axis):
    pairs = GQA_KV_HEADS // 2
    shape = w_q.shape
    split = shape[:axis] + (pairs, 2, GQA_GROUP, GQA_HEAD_DIM) + shape[axis + 1:]
    order = list(range(len(split)))
    order[axis + 1], order[axis + 2] = order[axis + 2], order[axis + 1]
    return w_q.reshape(split).transpose(order).reshape(shape)


def _mla_layouts(wq_b, wkv_a, wkv_b):
    hd = MLA_NOPE + MLA_ROPE
    w3 = wq_b.reshape(MLA_Q_LORA, MLA_HEADS, hd)
    nope = w3[:, :, :MLA_NOPE].reshape(MLA_Q_LORA, MLA_HEADS * MLA_NOPE)
    rope = w3[:, :, MLA_NOPE:].reshape(MLA_Q_LORA, MLA_HEADS * MLA_ROPE)

    def rope_blocks(r):
        r = r.reshape(-1, MLA_PAIRS, 2 * MLA_ROPE)
        return jnp.pad(r, ((0, 0), (0, 0), (0, LANES - 2 * MLA_ROPE))).reshape(-1, MLA_PAIRS * LANES)

    wb = jnp.concatenate([nope, rope_blocks(rope)], axis=1)
    wbs = rope_blocks(_swap_halves(rope, MLA_ROPE))
    kpe_w = wkv_a[:, MLA_KV_LORA:]
    kpe_ws = _swap_halves(kpe_w, MLA_ROPE)
    zeros_a = jnp.zeros((D_MODEL, LANES - 2 * MLA_ROPE), F32)
    wa_tail = jnp.concatenate([wkv_a[:, :MLA_KV_LORA], kpe_w, kpe_w, zeros_a, kpe_ws, kpe_ws, zeros_a], axis=1)
    kv3 = wkv_b.reshape(MLA_KV_LORA, MLA_HEADS, MLA_NOPE + MLA_V)
    wkvb = jnp.concatenate([kv3[:, :, :MLA_NOPE].reshape(MLA_KV_LORA, -1),
                            kv3[:, :, MLA_NOPE:].reshape(MLA_KV_LORA, -1)], axis=1)
    return wb, wbs, wa_tail, wkvb


def _chunk_cols(w, chunk):
    k, n = w.shape
    return w.reshape(k, n // chunk, chunk).transpose(1, 0, 2).astype(BF16)


def kernel(x_prompt, x_sample, cache_l0_k, cache_l0_v, cache_l1_ckv, cache_l1_kpe, cache_l2_k, cache_l2_v, cache_l3_k, cache_l3_v, c, c_ctx, ada_w, ada_b, norm_mix, norm_ffn, norm_final, na_w_qkv, na_w_o, na_rel_bias, mla_wq_a, mla_q_norm, mla_wq_b, mla_wkv_a, mla_kv_norm, mla_wkv_b, mla_w_o, gqa_w_qkv, gqa_w_o, gqa_sink, ffn_w_gate, ffn_w_up, ffn_w_down, moe_router, moe_w_gate, moe_w_up, moe_w_down):
    x = jnp.concatenate([x_prompt.reshape(N_PROMPT, D_MODEL), x_sample.reshape(N_SAMPLE, D_MODEL)], axis=0)
    cond = jnp.concatenate([c_ctx[None, :], c, jnp.zeros((N_COND - 1 - DEC_BATCH, D_MODEL), F32)], axis=0)
    mods = _adaln(cond, ada_w, ada_b).reshape(DEPTH, N_COND, 6, 1, D_MODEL)
    tri = jnp.asarray(np.tril(np.ones((ROW_TILE, ROW_TILE), np.float32), -1), BF16)
    no_sink = jnp.zeros((GQA_HEADS,), F32)
    na_caches = {0: (cache_l0_k, cache_l0_v), 3: (cache_l3_k, cache_l3_v)}
    state = []

    for i in range(DEPTH):
        sh_m, sc_m, g_m, sh_f, sc_f, g_f = [mods[i, :, j] for j in range(6)]
        gain_m = norm_mix[i][None, :]
        gain_f = norm_ffn[i][None, :]
        kind, s = i % 3, i // 3
        if kind == 0:
            qw = NA_HEADS * NA_HEAD_DIM
            qkv, k_rows, v_rows = _norm_proj(x, gain_m, sh_m, sc_m, na_w_qkv[s].astype(BF16), qw)
            o_p = _ctx_attn(qkv, no_sink, qw, False)
            ck, cv = na_caches[i]
            o_s = _na_latent_attn(qkv, ck.reshape(DEC_BATCH, PAST_LEN, qw), cv.reshape(DEC_BATCH, PAST_LEN, qw),
                                  _na_pair_table(na_rel_bias[s]))
            w_o = na_w_o[s].astype(BF16)
            kv_shape = (BATCH, SEQ, NA_HEADS, NA_HEAD_DIM)
            state += [k_rows.reshape(kv_shape), v_rows.reshape(kv_shape)]
        elif kind == 1:
            wb, wbs, wa_tail, wkvb = _mla_layouts(mla_wq_b[s], mla_wkv_a[s], mla_wkv_b[s])
            wa = jnp.concatenate([mla_wq_a[s], wa_tail], axis=1).astype(BF16)
            cos, sin = _rope_tables(MLA_ROPE, 2)
            q, ckv, kpe = _mla_proj(x, gain_m, sh_m, sc_m, wa, mla_q_norm[s][None, :], mla_kv_norm[s][None, :],
                                    wb.astype(BF16), wbs.astype(BF16), cos, sin)
            wkvb = wkvb.astype(BF16)
            kv_p = _rows_matmul(ckv[:N_PROMPT], wkvb, BF16)
            o_p = _mla_ctx_attn(q, kv_p, kpe)
            ckv_s = ckv[N_PROMPT:].reshape(DEC_BATCH, DEC_SEQ, MLA_KV_LORA)
            ckv_all = jnp.concatenate([cache_l1_ckv, ckv_s], axis=1).reshape(DEC_BATCH * MLA_KEYS, MLA_KV_LORA)
            kv_s = _rows_matmul(ckv_all, wkvb, BF16)
            kpe_cache = jnp.concatenate(
                [cache_l1_kpe, cache_l1_kpe, jnp.zeros((DEC_BATCH, PAST_LEN, LANES - 2 * MLA_ROPE), F32)], axis=2)
            kpe_all = jnp.concatenate([kpe_cache, kpe[N_PROMPT:].reshape(DEC_BATCH, DEC_SEQ, LANES)], axis=1)
            o_s = _mla_latent_attn(q, kv_s, kpe_all.reshape(DEC_BATCH * MLA_KEYS, LANES))
            w_o = mla_w_o[s].astype(BF16)
            state += [ckv[:N_PROMPT].reshape(BATCH, SEQ, MLA_KV_LORA),
                      kpe[:N_PROMPT, :MLA_ROPE].reshape(BATCH, SEQ, MLA_ROPE)]
        else:
            qw = GQA_HEADS * GQA_HEAD_DIM
            kw = GQA_KV_HEADS * GQA_HEAD_DIM
            w_qkv = jnp.concatenate([_gqa_arrange(gqa_w_qkv[s][:, :qw], 1), gqa_w_qkv[s][:, qw:]], axis=1)
            w_swap = _swap_halves(w_qkv[:, :qw + kw], GQA_HEAD_DIM)
            cos, sin = _rope_tables(GQA_HEAD_DIM, 2)
            qkv, k_rows, v_rows = _norm_proj(x, gain_m, sh_m, sc_m, w_qkv.astype(BF16), kw,
                                             rope=(w_swap.astype(BF16), cos, sin))
            sink = gqa_sink[s]
            o_p = _ctx_attn(qkv, sink, kw, True)
            o_s = _gqa_latent_attn(qkv, sink, cache_l2_k.reshape(DEC_BATCH, PAST_LEN, kw),
                                   cache_l2_v.reshape(DEC_BATCH, PAST_LEN, kw))
            w_o = _gqa_arrange(gqa_w_o[s], 0).astype(BF16)
            kv_shape = (BATCH, SEQ, GQA_KV_HEADS, GQA_HEAD_DIM)
            state += [k_rows.reshape(kv_shape), v_rows.reshape(kv_shape)]
        s = i // 2
        if i % 2 == 0:
            x = _ffn(x, o_p, o_s, g_m, w_o, gain_f, sh_f, sc_f, g_f, _chunk_cols(ffn_w_gate[s], FFN_CHUNK),
                     _chunk_cols(ffn_w_up[s], FFN_CHUNK),
                     ffn_w_down[s].reshape(N_FFN_CHUNKS, FFN_CHUNK, D_MODEL).astype(BF16))
        else:
            x = _moe(s, x, o_p, o_s, g_m, w_o, gain_f, sh_f, sc_f, g_f, moe_router[s],
                     moe_w_gate, moe_w_up, moe_w_down, tri)

    y_prompt = _final_norm(x, norm_final[None, :], 0, N_PROMPT).reshape(BATCH, SEQ, D_MODEL)
    y_sample = _final_norm(x, norm_final[None, :], N_PROMPT, N_SAMPLE).reshape(DEC_BATCH, DEC_SEQ, D_MODEL)
    return (y_prompt, y_sample, *state)
```

```python
import functools
import math

import numpy as np
import jax
import jax.numpy as jnp
from jax import lax
from jax.experimental import pallas as pl
from jax.experimental.pallas import tpu as pltpu

F32 = jnp.float32
BF16 = jnp.bfloat16

D_MODEL = 1024
BATCH = 16
SEQ = 256
DEPTH = 4
DEC_BATCH = 4
DEC_SEQ = 2048
PAST_LEN = 512
GRID_W = 64
GRID_ROWS = DEC_SEQ // GRID_W
NORM_EPS = 1e-6
NEG_INF = -1e30
ROPE_THETA = 10000.0
NA_HEADS = 16
NA_HEAD_DIM = 64
NA_WIN_ROWS = 8
NA_WIN_COLS = 16
MLA_HEADS = 16
MLA_Q_LORA = 512
MLA_KV_LORA = 256
MLA_NOPE = 64
MLA_ROPE = 32
MLA_V = 64
GQA_HEADS = 16
GQA_KV_HEADS = 4
GQA_HEAD_DIM = 64
GQA_WINDOW = 128
FFN_DIM = 2816
N_EXPERTS = 8
EXPERT_DIM = 3584

N_PROMPT = BATCH * SEQ
N_SAMPLE = DEC_BATCH * DEC_SEQ
N_TOK = N_PROMPT + N_SAMPLE
N_COND = 8
LOG2E = math.log2(math.e)
HEAD_SCALE = NA_HEAD_DIM ** -0.5 * LOG2E
MLA_SCALE = (MLA_NOPE + MLA_ROPE) ** -0.5 * LOG2E

LANES = 128
SUBLANES = 8
VMEM_LIMIT = 56 * 1024 * 1024

ROW_TILE = 512
FFN_CHUNK = 256
N_FFN_CHUNKS = FFN_DIM // FFN_CHUNK
EXP_TILE = 1024
EXP_PART = 256
EXP_CHUNK = 512
N_EXP_CHUNKS = EXPERT_DIM // EXP_CHUNK
N_EXP_TILES = (2 * N_TOK) // EXP_TILE + N_EXPERTS
COMBINE_TILE = 256
NA_QROWS = 4
NA_UNION = NA_QROWS + NA_WIN_ROWS
NA_PAIRS_PER_STEP = 8
MLA_KEYS = PAST_LEN + DEC_SEQ


def _cparams(sem):
    return pltpu.CompilerParams(dimension_semantics=sem, vmem_limit_bytes=VMEM_LIMIT)


def _cond_of_row(row):
    return jnp.where(row < N_PROMPT, 0, 1 + (row - N_PROMPT) // DEC_SEQ)


def _nt_dot(a, b):
    return lax.dot_general(a, b, (((1,), (1,)), ((), ())), preferred_element_type=F32)


def _dot(a, b):
    return jnp.dot(a, b, preferred_element_type=F32)


def _rms(x, gain):
    return x * lax.rsqrt(jnp.mean(x * x, axis=-1, keepdims=True) + NORM_EPS) * gain


def _silu(x):
    return x / (1.0 + jnp.exp(-x))


def _split_bf16(x):
    hi = x.astype(BF16)
    lo = (x - hi.astype(F32)).astype(BF16)
    return hi, lo


def _resident(shape):
    nd = len(shape)
    return pl.BlockSpec(shape, lambda *_: (0,) * nd, pipeline_mode=pl.Buffered(1))


def _mod_spec(tile):
    return pl.BlockSpec((1, 1, D_MODEL), lambda i: (_cond_of_row(i * tile), 0, 0))


ADA_TILE = 1536


def _adaln_kernel(c_ref, w_ref, b_ref, o_ref):
    a_hi, a_lo = _split_bf16(_silu(c_ref[...]))
    w_hi, w_lo = _split_bf16(w_ref[0])
    o_ref[0] = _dot(a_hi, w_hi) + _dot(a_lo, w_hi) + _dot(a_hi, w_lo) + b_ref[0]


def _adaln(cond, ada_w, ada_b):
    n = 6 * D_MODEL
    return pl.pallas_call(
        _adaln_kernel,
        grid=(DEPTH, n // ADA_TILE),
        in_specs=[
            pl.BlockSpec((N_COND, D_MODEL), lambda l, j: (0, 0)),
            pl.BlockSpec((1, D_MODEL, ADA_TILE), lambda l, j: (l, 0, j)),
            pl.BlockSpec((1, 1, ADA_TILE), lambda l, j: (l, 0, j)),
        ],
        out_specs=pl.BlockSpec((1, N_COND, ADA_TILE), lambda l, j: (l, 0, j)),
        out_shape=jax.ShapeDtypeStruct((DEPTH, N_COND, n), F32),
        compiler_params=_cparams(("arbitrary", "arbitrary")),
    )(cond, ada_w, ada_b.reshape(DEPTH, 1, n))


PROJ_COLS = 512


def _norm_proj_kernel(x_ref, g_ref, sh_ref, sc_ref, w_ref, *rest, n_out, n_rope, kv_cols):
    rest = list(rest)
    if n_rope:
        w2_ref, cos_ref, sin_ref = rest[:3]
        rest = rest[3:]
    o_ref, k_ref, v_ref = rest
    h = _rms(x_ref[...], g_ref[...]) * (1.0 + sc_ref[0]) + sh_ref[0]
    hb = h.astype(BF16)
    is_prompt = pl.program_id(0) < N_PROMPT // ROW_TILE
    cuts = sorted({0, n_out, n_rope, n_out - 2 * kv_cols, n_out - kv_cols}
                  | set(range(0, n_out, PROJ_COLS)))
    for c0, c1 in zip(cuts[:-1], cuts[1:]):
        cols = slice(c0, c1)
        y = _dot(hb, w_ref[:, cols])
        if c0 < n_rope:
            ys = _dot(hb, w2_ref[:, cols])
            reps = (c1 - c0) // LANES
            cos = jnp.concatenate([cos_ref[...]] * reps, axis=1)
            sin = jnp.concatenate([sin_ref[...]] * reps, axis=1)
            y = y * cos + ys * sin
        is_query = c0 < n_out - 2 * kv_cols
        o_ref[:, cols] = (y * HEAD_SCALE if is_query else y).astype(BF16)
        for kv_ref, start in ((k_ref, n_out - 2 * kv_cols), (v_ref, n_out - kv_cols)):
            if start <= c0 < start + kv_cols:
                @pl.when(is_prompt)
                def _(kv_ref=kv_ref, start=start, y=y, c0=c0, c1=c1):
                    kv_ref[:, c0 - start:c1 - start] = y


def _rope_block(i):
    row = i * ROW_TILE
    return jnp.where(row < N_PROMPT, DEC_SEQ // ROW_TILE, ((row - N_PROMPT) % DEC_SEQ) // ROW_TILE)


def _norm_proj(x, gain, shift, scale, w, kv_cols, rope=None):
    n_out = w.shape[1]
    last_prompt = N_PROMPT // ROW_TILE - 1
    kv_spec = pl.BlockSpec((ROW_TILE, kv_cols), lambda i: (jnp.minimum(i, last_prompt), 0))
    kv_shape = jax.ShapeDtypeStruct((N_PROMPT, kv_cols), F32)
    in_specs = [
        pl.BlockSpec((ROW_TILE, D_MODEL), lambda i: (i, 0)),
        pl.BlockSpec((1, D_MODEL), lambda i: (0, 0)),
        _mod_spec(ROW_TILE),
        _mod_spec(ROW_TILE),
        _resident(w.shape),
    ]
    args = [x, gain, shift, scale, w]
    n_rope = 0
    if rope is not None:
        w2, cos, sin = rope
        n_rope = w2.shape[1]
        in_specs += [
            _resident(w2.shape),
            pl.BlockSpec((ROW_TILE, LANES), lambda i: (_rope_block(i), 0)),
            pl.BlockSpec((ROW_TILE, LANES), lambda i: (_rope_block(i), 0)),
        ]
        args += [w2, cos, sin]
    return pl.pallas_call(
        functools.partial(_norm_proj_kernel, n_out=n_out, n_rope=n_rope, kv_cols=kv_cols),
        grid=(N_TOK // ROW_TILE,),
        in_specs=in_specs,
        out_specs=[pl.BlockSpec((ROW_TILE, n_out), lambda i: (i, 0)), kv_spec, kv_spec],
        out_shape=[jax.ShapeDtypeStruct((N_TOK, n_out), BF16), kv_shape, kv_shape],
        compiler_params=_cparams(("arbitrary",)),
    )(*args)


MLA_Q_COLS = 2 * MLA_HEADS * MLA_NOPE


def _mla_proj_kernel(x_ref, g_ref, sh_ref, sc_ref, wa_ref, qn_ref, kvn_ref, wb_ref, wbs_ref,
                     cos_ref, sin_ref, q_ref, ckv_ref, kpe_ref):
    h = _rms(x_ref[...], g_ref[...]) * (1.0 + sc_ref[0]) + sh_ref[0]
    a = _dot(h.astype(BF16), wa_ref[...])
    c0 = MLA_Q_LORA
    c1 = c0 + MLA_KV_LORA
    cos = cos_ref[...]
    sin = sin_ref[...]
    ckv_ref[...] = _rms(a[:, c0:c1], kvn_ref[...])
    kpe_ref[...] = a[:, c1:c1 + LANES] * cos + a[:, c1 + LANES:c1 + 2 * LANES] * sin
    qn = _rms(a[:, :c0], qn_ref[...]).astype(BF16)
    half = MLA_Q_COLS // 2
    q_scale = MLA_SCALE
    q_ref[:, :half] = (_dot(qn, wb_ref[:, :half]) * q_scale).astype(BF16)
    q_rope = _dot(qn, wb_ref[:, half:])
    q_swap = _dot(qn, wbs_ref[...])
    reps = half // LANES
    cos_r = jnp.concatenate([cos] * reps, axis=1) * q_scale
    sin_r = jnp.concatenate([sin] * reps, axis=1) * q_scale
    q_ref[:, half:] = (q_rope * cos_r + q_swap * sin_r).astype(BF16)


def _mla_proj(x, gain, shift, scale, wa, q_norm, kv_norm, wb, wbs, cos, sin):
    return pl.pallas_call(
        _mla_proj_kernel,
        grid=(N_TOK // ROW_TILE,),
        in_specs=[
            pl.BlockSpec((ROW_TILE, D_MODEL), lambda i: (i, 0)),
            pl.BlockSpec((1, D_MODEL), lambda i: (0, 0)),
            _mod_spec(ROW_TILE),
            _mod_spec(ROW_TILE),
            _resident(wa.shape),
            pl.BlockSpec((1, MLA_Q_LORA), lambda i: (0, 0)),
            pl.BlockSpec((1, MLA_KV_LORA), lambda i: (0, 0)),
            _resident(wb.shape),
            _resident(wbs.shape),
            pl.BlockSpec((ROW_TILE, LANES), lambda i: (_rope_block(i), 0)),
            pl.BlockSpec((ROW_TILE, LANES), lambda i: (_rope_block(i), 0)),
        ],
        out_specs=[
            pl.BlockSpec((ROW_TILE, MLA_Q_COLS), lambda i: (i, 0)),
            pl.BlockSpec((ROW_TILE, MLA_KV_LORA), lambda i: (i, 0)),
            pl.BlockSpec((ROW_TILE, LANES), lambda i: (i, 0)),
        ],
        out_shape=[
            jax.ShapeDtypeStruct((N_TOK, MLA_Q_COLS), BF16),
            jax.ShapeDtypeStruct((N_TOK, MLA_KV_LORA), F32),
            jax.ShapeDtypeStruct((N_TOK, LANES), F32),
        ],
        compiler_params=_cparams(("arbitrary",)),
    )(x, gain, shift, scale, wa, q_norm, kv_norm, wb, wbs, cos, sin)


def _rows_matmul_kernel(x_ref, w_ref, o_ref):
    o_ref[...] = _dot(x_ref[...].astype(BF16), w_ref[...]).astype(o_ref.dtype)


def _rows_matmul(x, w, out_dtype):
    rows, k = x.shape
    n = w.shape[1]
    return pl.pallas_call(
        _rows_matmul_kernel,
        grid=(rows // ROW_TILE,),
        in_specs=[pl.BlockSpec((ROW_TILE, k), lambda i: (i, 0)), _resident(w.shape)],
        out_specs=pl.BlockSpec((ROW_TILE, n), lambda i: (i, 0)),
        out_shape=jax.ShapeDtypeStruct((rows, n), out_dtype),
        compiler_params=_cparams(("arbitrary",)),
    )(x, w)


def _pair_attention(q, masks, keys, values, biases, q_scale, sinks):
    half = LANES // 2
    lane = lax.broadcasted_iota(jnp.int32, (1, LANES), 1)
    outs = []
    sink_terms = []
    for h in range(2):
        qh = jnp.where(masks[h], q, jnp.zeros_like(q))
        if q_scale != 1.0:
            qh = qh * q_scale
        qh = qh.astype(BF16)
        scores = []
        for k, bias in zip(keys, biases):
            s = _nt_dot(qh, k)
            if bias is not None:
                s = s + bias(h)
            scores.append(s)
        m = functools.reduce(jnp.maximum, [jnp.max(s, axis=-1, keepdims=True) for s in scores])
        if sinks is not None:
            m = jnp.maximum(m, sinks[h])
        own = (lane < half) if h == 0 else (lane >= half)
        acc = None
        for s, v in zip(scores, values):
            part = _dot(jnp.exp2(s - m).astype(BF16), jnp.where(own, v, jnp.ones_like(v)))
            acc = part if acc is None else acc + part
        outs.append(acc)
        if sinks is not None:
            sink_terms.append(jnp.exp2(sinks[h] - m))
    first = lane < half
    numer = jnp.where(first, outs[0], outs[1])
    denom = pltpu.roll(jnp.where(first, outs[1], outs[0]), half, axis=1)
    if sinks is not None:
        denom = denom + jnp.where(first, sink_terms[0], sink_terms[1])
    return (numer / denom).astype(BF16)


def _half_masks(width=LANES):
    lane = lax.broadcasted_iota(jnp.int32, (1, width), 1)
    return [lane < LANES // 2, lane >= LANES // 2]


def _ctx_attn_kernel(sink_ref, q_ref, k_ref, v_ref, o_ref, *, kv_blocks, use_sink):
    masks = _half_masks()
    n_q_blocks = o_ref.shape[1] // LANES
    for qb in range(n_q_blocks):
        kb = qb * kv_blocks // n_q_blocks
        cols = slice(kb * LANES, (kb + 1) * LANES)
        sinks = None
        if use_sink:
            base = (qb // 4) * 8 + qb % 4
            sinks = [sink_ref[base] * LOG2E, sink_ref[base + 4] * LOG2E]
        o_ref[:, qb * LANES:(qb + 1) * LANES] = _pair_attention(
            q_ref[:, qb * LANES:(qb + 1) * LANES], masks,
            [k_ref[:, cols]], [v_ref[:, cols]], [None], 1.0, sinks)


def _ctx_attn(qkv, sink, kv_width, use_sink):
    qw = NA_HEADS * NA_HEAD_DIM
    k_blk = qw // kv_width
    return pl.pallas_call(
        functools.partial(_ctx_attn_kernel, kv_blocks=kv_width // LANES, use_sink=use_sink),
        grid_spec=pltpu.PrefetchScalarGridSpec(
            num_scalar_prefetch=1,
            grid=(BATCH,),
            in_specs=[
                pl.BlockSpec((SEQ, qw), lambda b, s: (b, 0)),
                pl.BlockSpec((SEQ, kv_width), lambda b, s: (b, k_blk)),
                pl.BlockSpec((SEQ, kv_width), lambda b, s: (b, k_blk + 1)),
            ],
            out_specs=pl.BlockSpec((SEQ, qw), lambda b, s: (b, 0)),
        ),
        out_shape=jax.ShapeDtypeStruct((N_PROMPT, qw), BF16),
        compiler_params=_cparams(("arbitrary",)),
    )(sink, qkv, qkv, qkv)


def _na_union_start(rb):
    return jnp.clip(rb * NA_QROWS - NA_WIN_ROWS // 2, 0, GRID_ROWS - NA_UNION)


NA_DR_SLOTS = 2 * NA_WIN_ROWS


def _na_bias(pairs_ref, h, rb):
    u0 = _na_union_start(rb)
    left = lax.broadcasted_iota(jnp.int32, (1, LANES), 1) < GRID_W
    rows = []
    for j in range(NA_QROWS):
        r = rb * NA_QROWS + j
        start = jnp.clip(r - NA_WIN_ROWS // 2, 0, GRID_ROWS - NA_WIN_ROWS)
        blocks = []
        for ip in range(NA_UNION // 2):
            key_row = u0 + 2 * ip
            dr = key_row - r + NA_WIN_ROWS - 1
            ok_l = ((key_row >= start) & (key_row < start + NA_WIN_ROWS)).astype(jnp.int32)
            ok_r = ((key_row + 1 >= start) & (key_row + 1 < start + NA_WIN_ROWS)).astype(jnp.int32)
            blk = pairs_ref[h, jnp.clip(dr + 1, 0, NA_DR_SLOTS - 1)]
            blocks.append(jnp.where(jnp.where(left, ok_l, ok_r) > 0, blk, NEG_INF))
        rows.append(jnp.concatenate(blocks, axis=1))
    return jnp.concatenate(rows, axis=0)


def _na_latent_kernel(q_ref, k_ref, v_ref, kc_ref, vc_ref, pairs_ref, o_ref):
    rb = pl.program_id(2)
    start = pl.multiple_of(_na_union_start(rb) * GRID_W, GRID_W)
    n_loc = NA_UNION * GRID_W
    for p in range(NA_PAIRS_PER_STEP):
        cols = slice(p * LANES, (p + 1) * LANES)
        k_loc = k_ref[pl.ds(start, n_loc), cols]
        v_loc = v_ref[pl.ds(start, n_loc), cols]
        o_ref[:, cols] = _pair_attention(
            q_ref[:, cols], _half_masks(),
            [k_loc, kc_ref[0, :, cols].astype(BF16)], [v_loc, vc_ref[0, :, cols].astype(BF16)],
            [lambda h, p=p: _na_bias(pairs_ref, 2 * p + h, rb), None], 1.0, None)


def _na_latent_attn(qkv, k_ctx, v_ctx, pairs):
    qw = NA_HEADS * NA_HEAD_DIM
    width = NA_PAIRS_PER_STEP * LANES
    n_groups = qw // width
    q_rows = NA_QROWS * GRID_W
    n_rb = DEC_SEQ // q_rows
    q_off = N_PROMPT // q_rows
    kv_off = N_PROMPT // DEC_SEQ
    return pl.pallas_call(
        _na_latent_kernel,
        grid=(DEC_BATCH, n_groups, n_rb),
        in_specs=[
            pl.BlockSpec((q_rows, width), lambda b, j, r: (q_off + b * n_rb + r, j)),
            pl.BlockSpec((DEC_SEQ, width), lambda b, j, r: (kv_off + b, n_groups + j)),
            pl.BlockSpec((DEC_SEQ, width), lambda b, j, r: (kv_off + b, 2 * n_groups + j)),
            pl.BlockSpec((1, PAST_LEN, width), lambda b, j, r: (b, 0, j)),
            pl.BlockSpec((1, PAST_LEN, width), lambda b, j, r: (b, 0, j)),
            pl.BlockSpec((2 * NA_PAIRS_PER_STEP, NA_DR_SLOTS, GRID_W, LANES), lambda b, j, r: (j, 0, 0, 0)),
        ],
        out_specs=pl.BlockSpec((q_rows, width), lambda b, j, r: (b * n_rb + r, j)),
        out_shape=jax.ShapeDtypeStruct((N_SAMPLE, qw), BF16),
        compiler_params=_cparams(("arbitrary", "arbitrary", "arbitrary")),
    )(qkv, qkv, qkv, k_ctx, v_ctx, pairs)


def _na_pair_table(rel_bias):
    cols = np.arange(GRID_W)
    col_start = np.clip(cols - NA_WIN_COLS // 2, 0, GRID_W - NA_WIN_COLS)
    col_ok = (cols[None, :] >= col_start[:, None]) & (cols[None, :] < col_start[:, None] + NA_WIN_COLS)
    dc = np.clip(cols[None, :] - cols[:, None] + NA_WIN_COLS - 1, 0, 2 * NA_WIN_COLS - 2)
    n_dc = 2 * NA_WIN_COLS - 1
    n_dr = 2 * NA_WIN_ROWS - 1
    pick = np.zeros((n_dc, GRID_W * GRID_W), np.float32)
    pick[dc.reshape(-1), np.arange(GRID_W * GRID_W)] = 1.0
    flat = jnp.dot(rel_bias.astype(F32).reshape(NA_HEADS * n_dr, n_dc), jnp.asarray(pick),
                   precision=lax.Precision.HIGHEST)
    bias = jnp.where(col_ok.reshape(1, 1, -1), flat.reshape(NA_HEADS, n_dr, -1) * LOG2E, NEG_INF)
    bias = bias.reshape(NA_HEADS, n_dr, GRID_W, GRID_W)
    masked = jnp.full((NA_HEADS, 1, GRID_W, GRID_W), NEG_INF, F32)
    ext = jnp.concatenate([masked, bias, masked], axis=1)
    return jnp.concatenate([ext[:, :-1], ext[:, 1:]], axis=-1)


GQA_Q_BLOCK = GQA_WINDOW
GQA_GROUP = GQA_HEADS // GQA_KV_HEADS


def _gqa_latent_kernel(sink_ref, q_ref, kp_ref, kc_ref, kn_ref, vp_ref, vc_ref, vn_ref,
                       kx_ref, vx_ref, o_ref):
    qb = pl.program_id(1)
    n_qb = pl.num_programs(1)
    blk = GQA_Q_BLOCK
    rows = GQA_GROUP * blk
    qq = lax.broadcasted_iota(jnp.int32, (rows, blk), 0) % blk
    kk = lax.broadcasted_iota(jnp.int32, (rows, blk), 1)
    neg = jnp.float32(NEG_INF)
    bias_prev = jnp.where((kk >= qq) & (qb > 0), 0.0, neg)
    bias_next = jnp.where((kk <= qq) & (qb < n_qb - 1), 0.0, neg)
    row_grp = lax.broadcasted_iota(jnp.int32, (rows, 1), 0) // blk
    for pair in range(GQA_KV_HEADS // 2):
        kv = slice(pair * LANES, (pair + 1) * LANES)
        q0 = pair * GQA_GROUP * LANES
        q = jnp.concatenate([q_ref[:, q0 + m * LANES:q0 + (m + 1) * LANES] for m in range(GQA_GROUP)], axis=0)
        sinks = []
        for h in range(2):
            base = pair * 2 * GQA_GROUP + h * GQA_GROUP
            col = jnp.zeros((rows, 1), F32)
            for m in range(GQA_GROUP):
                col = jnp.where(row_grp == m, sink_ref[base + m] * LOG2E, col)
            sinks.append(col)
        o = _pair_attention(
            q, _half_masks(),
            [kp_ref[:, kv], kc_ref[:, kv], kn_ref[:, kv], kx_ref[0, :, kv].astype(BF16)],
            [vp_ref[:, kv], vc_ref[:, kv], vn_ref[:, kv], vx_ref[0, :, kv].astype(BF16)],
            [lambda h: bias_prev, None, lambda h: bias_next, None], 1.0, sinks)
        for m in range(GQA_GROUP):
            o_ref[:, q0 + m * LANES:q0 + (m + 1) * LANES] = o[m * blk:(m + 1) * blk]


def _gqa_latent_attn(qkv, sink, k_ctx, v_ctx):
    qw = GQA_HEADS * GQA_HEAD_DIM
    blk = GQA_Q_BLOCK
    n_qb = DEC_SEQ // blk
    q_off = N_PROMPT // blk
    kw = GQA_KV_HEADS * GQA_HEAD_DIM
    k_col = qw // kw
    v_col = k_col + 1

    def kv_spec(col, shift):
        def imap(b, i, s):
            return (q_off + b * n_qb + jnp.clip(i + shift, 0, n_qb - 1), col)
        return pl.BlockSpec((blk, kw), imap)

    ctx_spec = pl.BlockSpec((1, PAST_LEN, kw), lambda b, i, s: (b, 0, 0))
    return pl.pallas_call(
        _gqa_latent_kernel,
        grid_spec=pltpu.PrefetchScalarGridSpec(
            num_scalar_prefetch=1,
            grid=(DEC_BATCH, n_qb),
            in_specs=[
                pl.BlockSpec((blk, qw), lambda b, i, s: (q_off + b * n_qb + i, 0)),
                kv_spec(k_col, -1), kv_spec(k_col, 0), kv_spec(k_col, 1),
                kv_spec(v_col, -1), kv_spec(v_col, 0), kv_spec(v_col, 1),
                ctx_spec, ctx_spec,
            ],
            out_specs=pl.BlockSpec((blk, qw), lambda b, i, s: (b * n_qb + i, 0)),
        ),
        out_shape=jax.ShapeDtypeStruct((N_SAMPLE, qw), BF16),
        compiler_params=_cparams(("arbitrary", "arbitrary")),
    )(sink, qkv, qkv, qkv, qkv, qkv, qkv, qkv, k_ctx, v_ctx)


MLA_PAIRS = MLA_HEADS // 2


def _mla_masks():
    lane = lax.broadcasted_iota(jnp.int32, (1, 2 * LANES), 1)
    half = LANES // 2
    even = (lane < half) | ((lane >= LANES) & (lane < LANES + MLA_ROPE))
    odd = ((lane >= half) & (lane < LANES)) | ((lane >= LANES + MLA_ROPE) & (lane < LANES + 2 * MLA_ROPE))
    return [even, odd]


def _mla_ctx_kernel(q_ref, kv_ref, kpe_ref, o_ref):
    masks = _mla_masks()
    kpe = kpe_ref[...].astype(BF16)
    half = MLA_Q_COLS // 2
    for j in range(MLA_PAIRS):
        cols = slice(j * LANES, (j + 1) * LANES)
        rcols = slice(half + j * LANES, half + (j + 1) * LANES)
        q = jnp.concatenate([q_ref[:, cols], q_ref[:, rcols]], axis=1)
        k = jnp.concatenate([kv_ref[:, cols], kpe], axis=1)
        o_ref[:, cols] = _pair_attention(q, masks, [k], [kv_ref[:, rcols]], [None], 1.0, None)


def _mla_ctx_attn(q, kv, kpe):
    ow = MLA_HEADS * MLA_V
    return pl.pallas_call(
        _mla_ctx_kernel,
        grid=(BATCH,),
        in_specs=[
            pl.BlockSpec((SEQ, MLA_Q_COLS), lambda b: (b, 0)),
            pl.BlockSpec((SEQ, MLA_Q_COLS), lambda b: (b, 0)),
            pl.BlockSpec((SEQ, LANES), lambda b: (b, 0)),
        ],
        out_specs=pl.BlockSpec((SEQ, ow), lambda b: (b, 0)),
        out_shape=jax.ShapeDtypeStruct((N_PROMPT, ow), BF16),
        compiler_params=_cparams(("arbitrary",)),
    )(q, kv, kpe)


MLA_Q_BLOCK = 256


MLA_PAIRS_PER_STEP = 4


def _mla_latent_kernel(qn_ref, qr_ref, ka_ref, kpe_ref, v_ref, o_ref, kcat_ref):
    @pl.when(pl.program_id(2) == 0)
    def _():
        kpe = kpe_ref[...].astype(BF16)
        for p in range(MLA_PAIRS_PER_STEP):
            kcat_ref[p, :, :LANES] = ka_ref[:, p * LANES:(p + 1) * LANES]
            kcat_ref[p, :, LANES:] = kpe

    for p in range(MLA_PAIRS_PER_STEP):
        cols = slice(p * LANES, (p + 1) * LANES)
        q = jnp.concatenate([qn_ref[:, cols], qr_ref[:, cols]], axis=1)
        o_ref[:, cols] = _pair_attention(q, _mla_masks(), [kcat_ref[p]], [v_ref[:, cols]], [None], 1.0, None)


def _mla_latent_attn(q, kv, kpe):
    ow = MLA_HEADS * MLA_V
    n_qb = DEC_SEQ // MLA_Q_BLOCK
    q_off = N_PROMPT // MLA_Q_BLOCK
    width = MLA_PAIRS_PER_STEP * LANES
    n_groups = MLA_PAIRS // MLA_PAIRS_PER_STEP
    return pl.pallas_call(
        _mla_latent_kernel,
        grid=(DEC_BATCH, n_groups, n_qb),
        in_specs=[
            pl.BlockSpec((MLA_Q_BLOCK, width), lambda b, j, i: (q_off + b * n_qb + i, j)),
            pl.BlockSpec((MLA_Q_BLOCK, width), lambda b, j, i: (q_off + b * n_qb + i, n_groups + j)),
            pl.BlockSpec((MLA_KEYS, width), lambda b, j, i: (b, j)),
            pl.BlockSpec((MLA_KEYS, LANES), lambda b, j, i: (b, 0)),
            pl.BlockSpec((MLA_KEYS, width), lambda b, j, i: (b, n_groups + j)),
        ],
        out_specs=pl.BlockSpec((MLA_Q_BLOCK, width), lambda b, j, i: (b * n_qb + i, j)),
        out_shape=jax.ShapeDtypeStruct((N_SAMPLE, ow), BF16),
        scratch_shapes=[pltpu.VMEM((MLA_PAIRS_PER_STEP, MLA_KEYS, 2 * LANES), BF16)],
        compiler_params=_cparams(("arbitrary", "arbitrary", "arbitrary")),
    )(q, q, kv, kpe, kv)


def _mixer_residual(x_ref, op_ref, os_ref, gate_m_ref, wo_ref):
    is_prompt = pl.program_id(0) < N_PROMPT // ROW_TILE
    o = jnp.where(is_prompt, op_ref[...], os_ref[...])
    return x_ref[...] + gate_m_ref[0] * _dot(o, wo_ref[...])


def _attn_out_specs(k):
    n_p = N_PROMPT // ROW_TILE
    return [
        pl.BlockSpec((ROW_TILE, k), lambda i: (jnp.minimum(i, n_p - 1), 0)),
        pl.BlockSpec((ROW_TILE, k), lambda i: (jnp.maximum(i - n_p, 0), 0)),
        _mod_spec(ROW_TILE),
    ]


def _ffn_kernel(x_ref, op_ref, os_ref, gate_m_ref, wo_ref, g_ref, sh_ref, sc_ref, gate_ref,
                wg_ref, wu_ref, wd_ref, y_ref, acc_ref):
    x = _mixer_residual(x_ref, op_ref, os_ref, gate_m_ref, wo_ref)
    hb = (_rms(x, g_ref[...]) * (1.0 + sc_ref[0]) + sh_ref[0]).astype(BF16)
    acc_ref[...] = jnp.zeros_like(acc_ref)

    def chunk(c, carry):
        a = _silu(_dot(hb, wg_ref[c])) * _dot(hb, wu_ref[c])
        acc_ref[...] += _dot(a.astype(BF16), wd_ref[c])
        return carry

    lax.fori_loop(0, N_FFN_CHUNKS, chunk, 0)
    y_ref[...] = x + gate_ref[0] * acc_ref[...]


def _ffn(x, o_prompt, o_sample, gate_m, w_o, gain, shift, scale, gate, wg, wu, wd):
    return pl.pallas_call(
        _ffn_kernel,
        grid=(N_TOK // ROW_TILE,),
        in_specs=[
            pl.BlockSpec((ROW_TILE, D_MODEL), lambda i: (i, 0)),
            *_attn_out_specs(w_o.shape[0]),
            _resident(w_o.shape),
            pl.BlockSpec((1, D_MODEL), lambda i: (0, 0)),
            _mod_spec(ROW_TILE), _mod_spec(ROW_TILE), _mod_spec(ROW_TILE),
            _resident(wg.shape), _resident(wu.shape), _resident(wd.shape),
        ],
        out_specs=pl.BlockSpec((ROW_TILE, D_MODEL), lambda i: (i, 0)),
        out_shape=jax.ShapeDtypeStruct((N_TOK, D_MODEL), F32),
        scratch_shapes=[pltpu.VMEM((ROW_TILE, D_MODEL), F32)],
        compiler_params=_cparams(("arbitrary",)),
    )(x, o_prompt, o_sample, gate_m, w_o, gain, shift, scale, gate, wg, wu, wd)


SLAB = D_MODEL // LANES


def _store_slabs(ref, x):
    n = x.shape[0]
    for s in range(SLAB):
        ref[pl.ds(s, n, stride=SLAB), :] = x[:, s * LANES:(s + 1) * LANES]


def _load_slabs(ref, n, dtype, row0=0):
    return jnp.concatenate([ref[pl.ds(row0 + s, n, stride=SLAB), :].astype(dtype) for s in range(SLAB)], axis=1)


META_E1, META_E2, META_W1, META_W2, META_R1, META_R2 = range(6)


def _router_kernel(x_ref, op_ref, os_ref, gate_m_ref, wo_ref, g_ref, sh_ref, sc_ref, whi_ref, wlo_ref, tri_ref,
                   x1_ref, h_ref, meta_ref, cnt_ref, carry_ref):
    @pl.when(pl.program_id(0) == 0)
    def _():
        carry_ref[...] = jnp.zeros_like(carry_ref)

    x = _mixer_residual(x_ref, op_ref, os_ref, gate_m_ref, wo_ref)
    x1_ref[...] = x
    h = _rms(x, g_ref[...]) * (1.0 + sc_ref[0]) + sh_ref[0]
    _store_slabs(h_ref, h)
    h_hi, h_lo = _split_bf16(h)
    logits = _dot(h_hi, whi_ref[...]) + _dot(h_lo, whi_ref[...]) + _dot(h_hi, wlo_ref[...])
    lane = lax.broadcasted_iota(jnp.int32, logits.shape, 1)
    lane_f = lane.astype(F32)
    ninf = jnp.float32(-jnp.inf)
    lg = jnp.where(lane < N_EXPERTS, logits, ninf)
    m1 = jnp.max(lg, axis=-1, keepdims=True)
    i1 = jnp.min(jnp.where(lg == m1, lane_f, float(LANES)), axis=-1, keepdims=True)
    lg2 = jnp.where(lane_f == i1, ninf, lg)
    m2 = jnp.max(lg2, axis=-1, keepdims=True)
    i2 = jnp.min(jnp.where(lg2 == m2, lane_f, float(LANES)), axis=-1, keepdims=True)
    e = jnp.exp(m2 - m1)
    w1 = 1.0 / (1.0 + e)
    w2 = e / (1.0 + e)
    sel1 = lane_f == i1
    sel2 = lane_f == i2
    onehot = jnp.where(sel1 | sel2, 1.0, 0.0)
    ranks = _dot(tri_ref[...], onehot.astype(BF16)) + carry_ref[...]
    r1 = jnp.sum(jnp.where(sel1, ranks, 0.0), axis=-1, keepdims=True)
    r2 = jnp.sum(jnp.where(sel2, ranks, 0.0), axis=-1, keepdims=True)
    carry_ref[...] += jnp.sum(onehot, axis=0, keepdims=True)
    cnt_ref[...] = carry_ref[...]
    meta = jnp.zeros(logits.shape, F32)
    for idx, val in ((META_E1, i1), (META_E2, i2), (META_W1, w1), (META_W2, w2), (META_R1, r1), (META_R2, r2)):
        meta = jnp.where(lane == idx, val, meta)
    meta_ref[...] = meta


def _router(x, o_prompt, o_sample, gate_m, w_o, gain, shift, scale, w_hi, w_lo, tri):
    return pl.pallas_call(
        _router_kernel,
        grid=(N_TOK // ROW_TILE,),
        in_specs=[
            pl.BlockSpec((ROW_TILE, D_MODEL), lambda i: (i, 0)),
            *_attn_out_specs(w_o.shape[0]),
            _resident(w_o.shape),
            pl.BlockSpec((1, D_MODEL), lambda i: (0, 0)),
            _mod_spec(ROW_TILE), _mod_spec(ROW_TILE),
            _resident(w_hi.shape), _resident(w_lo.shape), _resident(tri.shape),
        ],
        out_specs=[
            pl.BlockSpec((ROW_TILE, D_MODEL), lambda i: (i, 0)),
            pl.BlockSpec((ROW_TILE * SLAB, LANES), lambda i: (i, 0)),
            pl.BlockSpec((ROW_TILE, LANES), lambda i: (i, 0)),
            pl.BlockSpec((1, LANES), lambda i: (0, 0)),
        ],
        out_shape=[
            jax.ShapeDtypeStruct((N_TOK, D_MODEL), F32),
            jax.ShapeDtypeStruct((N_TOK * SLAB, LANES), F32),
            jax.ShapeDtypeStruct((N_TOK, LANES), F32),
            jax.ShapeDtypeStruct((1, LANES), F32),
        ],
        scratch_shapes=[pltpu.VMEM((1, LANES), F32)],
        compiler_params=_cparams(("arbitrary",)),
    )(x, o_prompt, o_sample, gate_m, w_o, gain, shift, scale, w_hi, w_lo, tri)


def _route_plan(meta, counts):
    cnt = counts[0, :N_EXPERTS].astype(jnp.int32)
    tiles = (cnt + EXP_TILE - 1) // EXP_TILE
    tile_end = jnp.cumsum(tiles)
    tile_start = tile_end - tiles
    offs = tile_start * EXP_TILE
    experts = jnp.arange(N_EXPERTS, dtype=jnp.int32)

    def lookup(table, idx):
        return jnp.sum(jnp.where(idx[:, None] == experts[None, :], table[None, :], 0), axis=1)

    e1 = meta[:, META_E1].astype(jnp.int32)
    e2 = meta[:, META_E2].astype(jnp.int32)
    pos1 = lookup(offs, e1) + meta[:, META_R1].astype(jnp.int32)
    pos2 = lookup(offs, e2) + meta[:, META_R2].astype(jnp.int32)
    t = jnp.arange(N_EXP_TILES, dtype=jnp.int32)
    tile_expert = jnp.minimum(jnp.sum(t[:, None] >= tile_end[None, :], axis=1), N_EXPERTS - 1).astype(jnp.int32)
    in_group = (t - lookup(tile_start, tile_expert)) * EXP_TILE
    tile_rows = jnp.where(t < tile_end[-1], jnp.clip(lookup(cnt, tile_expert) - in_group, 0, EXP_TILE),
                          0).astype(jnp.int32)
    return pos1, pos2, tile_expert, tile_rows


DISPATCH_TILE = 512


def _dispatch_kernel(pos1_ref, pos2_ref, h_ref, init_ref, xs_ref, sem):
    del init_ref
    base = pl.program_id(0) * DISPATCH_TILE

    def row_copy(t, dst):
        return pltpu.make_async_copy(
            h_ref.at[pl.ds(pl.multiple_of(t * SLAB, SLAB), SLAB)],
            xs_ref.at[pl.ds(pl.multiple_of(dst * SLAB, SLAB), SLAB)], sem)

    def issue(t, carry):
        row_copy(t, pos1_ref[base + t]).start()
        row_copy(t, pos2_ref[base + t]).start()
        return carry

    lax.fori_loop(0, DISPATCH_TILE, issue, 0)

    def drain(t, carry):
        row_copy(0, 0).wait()
        row_copy(0, 0).wait()
        return carry

    lax.fori_loop(0, DISPATCH_TILE, drain, 0)


def _dispatch(pos1, pos2, h_slabs, xs_init):
    return pl.pallas_call(
        _dispatch_kernel,
        grid_spec=pltpu.PrefetchScalarGridSpec(
            num_scalar_prefetch=2,
            grid=(N_TOK // DISPATCH_TILE,),
            in_specs=[pl.BlockSpec((DISPATCH_TILE * SLAB, LANES), lambda i, p1, p2: (i, 0)),
                      pl.BlockSpec(memory_space=pl.ANY)],
            out_specs=pl.BlockSpec(memory_space=pl.ANY),
            scratch_shapes=[pltpu.SemaphoreType.DMA(())],
        ),
        out_shape=jax.ShapeDtypeStruct(xs_init.shape, F32),
        input_output_aliases={3: 0},
        compiler_params=_cparams(("arbitrary",)),
    )(pos1, pos2, h_slabs, xs_init)


def _expert_kernel(te_ref, tr_ref, xs_ref, wg_ref, wu_ref, wd_ref, ys_ref, xb_ref, acc_ref):
    t = pl.program_id(0)
    c = pl.program_id(1)

    @pl.when(c == 0)
    def _():
        xb_ref[...] = _load_slabs(xs_ref, EXP_TILE, BF16)
        acc_ref[...] = jnp.zeros_like(acc_ref)

    rows = tr_ref[t]

    def swiglu_rows(n):
        xb = xb_ref[:n]
        a = _silu(_dot(xb, wg_ref[...].astype(BF16))) * _dot(xb, wu_ref[...].astype(BF16))
        acc_ref[:n] += _dot(a.astype(BF16), wd_ref[...].astype(BF16))

    for parts in range(1, EXP_TILE // EXP_PART + 1):
        @pl.when((rows > (parts - 1) * EXP_PART) & (rows <= parts * EXP_PART))
        def _(parts=parts):
            swiglu_rows(parts * EXP_PART)

    @pl.when(c == N_EXP_CHUNKS - 1)
    def _():
        _store_slabs(ys_ref, acc_ref[...])


def _experts(layer, tile_expert, tile_rows, xs, wg, wu, wd):
    def chunk_of(t, c, tr):
        return jnp.where(tr[t] > 0, c, N_EXP_CHUNKS - 1)

    return pl.pallas_call(
        _expert_kernel,
        grid_spec=pltpu.PrefetchScalarGridSpec(
            num_scalar_prefetch=2,
            grid=(N_EXP_TILES, N_EXP_CHUNKS),
            in_specs=[
                pl.BlockSpec((EXP_TILE * SLAB, LANES), lambda t, c, te, tr: (t, 0)),
                pl.BlockSpec((None, None, D_MODEL, EXP_CHUNK),
                             lambda t, c, te, tr: (layer, te[t], 0, chunk_of(t, c, tr))),
                pl.BlockSpec((None, None, D_MODEL, EXP_CHUNK),
                             lambda t, c, te, tr: (layer, te[t], 0, chunk_of(t, c, tr))),
                pl.BlockSpec((None, None, EXP_CHUNK, D_MODEL),
                             lambda t, c, te, tr: (layer, te[t], chunk_of(t, c, tr), 0)),
            ],
            out_specs=pl.BlockSpec((EXP_TILE * SLAB, LANES), lambda t, c, te, tr: (t, 0)),
            scratch_shapes=[pltpu.VMEM((EXP_TILE, D_MODEL), BF16), pltpu.VMEM((EXP_TILE, D_MODEL), F32)],
        ),
        out_shape=jax.ShapeDtypeStruct(xs.shape, F32),
        compiler_params=_cparams(("arbitrary", "arbitrary")),
    )(tile_expert, tile_rows, xs, wg, wu, wd)


def _combine_kernel(pos1_ref, pos2_ref, x_ref, gate_ref, meta_ref, ys_ref, y_ref, buf_ref, sem):
    i = pl.program_id(0)
    n = COMBINE_TILE
    slot = i % 2

    def fetch(step, dst):
        base = step * n

        def body(t, carry):
            for pos_ref, row in ((pos1_ref, t), (pos2_ref, n + t)):
                pltpu.make_async_copy(
                    ys_ref.at[pl.ds(pl.multiple_of(pos_ref[base + t] * SLAB, SLAB), SLAB)],
                    buf_ref.at[dst, pl.ds(pl.multiple_of(row * SLAB, SLAB), SLAB)], sem.at[dst]).start()
            return carry

        lax.fori_loop(0, n, body, 0)

    @pl.when(i == 0)
    def _():
        fetch(0, 0)

    @pl.when(i + 1 < pl.num_programs(0))
    def _():
        fetch(i + 1, 1 - slot)

    pltpu.make_async_copy(ys_ref.at[pl.ds(0, 2 * n * SLAB)], buf_ref.at[slot], sem.at[slot]).wait()
    meta = meta_ref[...]
    w1 = meta[:, META_W1:META_W1 + 1]
    w2 = meta[:, META_W2:META_W2 + 1]
    rows = buf_ref.at[slot]
    y1 = _load_slabs(rows, n, F32)
    y2 = _load_slabs(rows, n, F32, row0=n * SLAB)
    y_ref[...] = x_ref[...] + gate_ref[0] * (w1 * y1 + w2 * y2)


def _combine(pos1, pos2, x, gate, meta, ys):
    return pl.pallas_call(
        _combine_kernel,
        grid_spec=pltpu.PrefetchScalarGridSpec(
            num_scalar_prefetch=2,
            grid=(N_TOK // COMBINE_TILE,),
            in_specs=[
                pl.BlockSpec((COMBINE_TILE, D_MODEL), lambda i, p1, p2: (i, 0)),
                pl.BlockSpec((1, 1, D_MODEL), lambda i, p1, p2: (_cond_of_row(i * COMBINE_TILE), 0, 0)),
                pl.BlockSpec((COMBINE_TILE, LANES), lambda i, p1, p2: (i, 0)),
                pl.BlockSpec(memory_space=pl.ANY),
            ],
            out_specs=pl.BlockSpec((COMBINE_TILE, D_MODEL), lambda i, p1, p2: (i, 0)),
            scratch_shapes=[pltpu.VMEM((2, 2 * COMBINE_TILE * SLAB, LANES), F32), pltpu.SemaphoreType.DMA((2,))],
        ),
        out_shape=jax.ShapeDtypeStruct((N_TOK, D_MODEL), F32),
        compiler_params=_cparams(("arbitrary",)),
    )(pos1, pos2, x, gate, meta, ys)


def _moe(layer, x, o_prompt, o_sample, gate_m, w_o, gain, shift, scale, gate, router_w, wg, wu, wd, tri):
    w = jnp.pad(router_w, ((0, 0), (0, LANES - N_EXPERTS)))
    w_hi = w.astype(BF16)
    w_lo = (w - w_hi.astype(F32)).astype(BF16)
    x1, h_slabs, meta, counts = _router(x, o_prompt, o_sample, gate_m, w_o, gain, shift, scale, w_hi, w_lo, tri)
    pos1, pos2, tile_expert, tile_rows = _route_plan(meta, counts)
    xs = _dispatch(pos1, pos2, h_slabs, jnp.zeros((N_EXP_TILES * EXP_TILE * SLAB, LANES), F32))
    ys = _experts(layer, tile_expert, tile_rows, xs, wg, wu, wd)
    return _combine(pos1, pos2, x1, gate, meta, ys)


def _final_norm_kernel(x_ref, g_ref, y_ref):
    y_ref[...] = _rms(x_ref[...], g_ref[...])


def _final_norm(x, gain, row0, rows):
    off = row0 // ROW_TILE
    return pl.pallas_call(
        _final_norm_kernel,
        grid=(rows // ROW_TILE,),
        in_specs=[
            pl.BlockSpec((ROW_TILE, D_MODEL), lambda i: (off + i, 0)),
            pl.BlockSpec((1, D_MODEL), lambda i: (0, 0)),
        ],
        out_specs=pl.BlockSpec((ROW_TILE, D_MODEL), lambda i: (i, 0)),
        out_shape=jax.ShapeDtypeStruct((rows, D_MODEL), F32),
        compiler_params=_cparams(("arbitrary",)),
    )(x, gain)


def _rope_tables(rot_dim, reps):
    n_freq = rot_dim // 4
    inv_freq = ROPE_THETA ** (-np.arange(n_freq, dtype=np.float64) / n_freq)
    t = np.arange(DEC_SEQ)
    ang_r = (t // GRID_W)[:, None] * inv_freq
    ang_c = (t % GRID_W)[:, None] * inv_freq
    cos = np.concatenate([np.cos(ang_r), np.cos(ang_r), np.cos(ang_c), np.cos(ang_c)], axis=1)
    sin = np.concatenate([-np.sin(ang_r), np.sin(ang_r), -np.sin(ang_c), np.sin(ang_c)], axis=1)
    pad = LANES - reps * rot_dim
    cos = np.concatenate([cos] * reps + [np.ones((DEC_SEQ, pad))], axis=1)
    sin = np.concatenate([sin] * reps + [np.zeros((DEC_SEQ, pad))], axis=1)
    cos = np.concatenate([cos, np.ones((ROW_TILE, LANES))], axis=0)
    sin = np.concatenate([sin, np.zeros((ROW_TILE, LANES))], axis=0)
    return jnp.asarray(cos, F32), jnp.asarray(sin, F32)


def _swap_halves(w, rot_dim):
    k, n = w.shape
    q = rot_dim // 4
    return w.reshape(k, n // (2 * q), 2, q)[:, :, ::-1, :].reshape(k, n)


def _gqa_arrange(w_q, axis):
    pairs = GQA_KV_HEADS // 2
    shape = w_q.shape
    split = shape[:axis] + (pairs, 2, GQA_GROUP, GQA_HEAD_DIM) + shape[axis + 1:]
    order = list(range(len(split)))
    order[axis + 1], order[axis + 2] = order[axis + 2], order[axis + 1]
    return w_q.reshape(split).transpose(order).reshape(shape)


def _mla_layouts(wq_b, wkv_a, wkv_b):
    hd = MLA_NOPE + MLA_ROPE
    w3 = wq_b.reshape(MLA_Q_LORA, MLA_HEADS, hd)
    nope = w3[:, :, :MLA_NOPE].reshape(MLA_Q_LORA, MLA_HEADS * MLA_NOPE)
    rope = w3[:, :, MLA_NOPE:].reshape(MLA_Q_LORA, MLA_HEADS * MLA_ROPE)

    def rope_blocks(r):
        r = r.reshape(-1, MLA_PAIRS, 2 * MLA_ROPE)
        return jnp.pad(r, ((0, 0), (0, 0), (0, LANES - 2 * MLA_ROPE))).reshape(-1, MLA_PAIRS * LANES)

    wb = jnp.concatenate([nope, rope_blocks(rope)], axis=1)
    wbs = rope_blocks(_swap_halves(rope, MLA_ROPE))
    kpe_w = wkv_a[:, MLA_KV_LORA:]
    kpe_ws = _swap_halves(kpe_w, MLA_ROPE)
    zeros_a = jnp.zeros((D_MODEL, LANES - 2 * MLA_ROPE), F32)
    wa_tail = jnp.concatenate([wkv_a[:, :MLA_KV_LORA], kpe_w, kpe_w, zeros_a, kpe_ws, kpe_ws, zeros_a], axis=1)
    kv3 = wkv_b.reshape(MLA_KV_LORA, MLA_HEADS, MLA_NOPE + MLA_V)
    wkvb = jnp.concatenate([kv3[:, :, :MLA_NOPE].reshape(MLA_KV_LORA, -1),
                            kv3[:, :, MLA_NOPE:].reshape(MLA_KV_LORA, -1)], axis=1)
    return wb, wbs, wa_tail, wkvb


def _chunk_cols(w, chunk):
    k, n = w.shape
    return w.reshape(k, n // chunk, chunk).transpose(1, 0, 2).astype(BF16)


def kernel(x_prompt, x_sample, cache_l0_k, cache_l0_v, cache_l1_ckv, cache_l1_kpe, cache_l2_k, cache_l2_v, cache_l3_k, cache_l3_v, c, c_ctx, ada_w, ada_b, norm_mix, norm_ffn, norm_final, na_w_qkv, na_w_o, na_rel_bias, mla_wq_a, mla_q_norm, mla_wq_b, mla_wkv_a, mla_kv_norm, mla_wkv_b, mla_w_o, gqa_w_qkv, gqa_w_o, gqa_sink, ffn_w_gate, ffn_w_up, ffn_w_down, moe_router, moe_w_gate, moe_w_up, moe_w_down):
    x = jnp.concatenate([x_prompt.reshape(N_PROMPT, D_MODEL), x_sample.reshape(N_SAMPLE, D_MODEL)], axis=0)
    cond = jnp.concatenate([c_ctx[None, :], c, jnp.zeros((N_COND - 1 - DEC_BATCH, D_MODEL), F32)], axis=0)
    mods = _adaln(cond, ada_w, ada_b).reshape(DEPTH, N_COND, 6, 1, D_MODEL)
    tri = jnp.asarray(np.tril(np.ones((ROW_TILE, ROW_TILE), np.float32), -1), BF16)
    no_sink = jnp.zeros((GQA_HEADS,), F32)
    na_caches = {0: (cache_l0_k, cache_l0_v), 3: (cache_l3_k, cache_l3_v)}
    state = []

    for i in range(DEPTH):
        sh_m, sc_m, g_m, sh_f, sc_f, g_f = [mods[i, :, j] for j in range(6)]
        gain_m = norm_mix[i][None, :]
        gain_f = norm_ffn[i][None, :]
        kind, s = i % 3, i // 3
        if kind == 0:
            qw = NA_HEADS * NA_HEAD_DIM
            qkv, k_rows, v_rows = _norm_proj(x, gain_m, sh_m, sc_m, na_w_qkv[s].astype(BF16), qw)
            o_p = _ctx_attn(qkv, no_sink, qw, False)
            ck, cv = na_caches[i]
            o_s = _na_latent_attn(qkv, ck.reshape(DEC_BATCH, PAST_LEN, qw), cv.reshape(DEC_BATCH, PAST_LEN, qw),
                                  _na_pair_table(na_rel_bias[s]))
            w_o = na_w_o[s].astype(BF16)
            kv_shape = (BATCH, SEQ, NA_HEADS, NA_HEAD_DIM)
            state += [k_rows.reshape(kv_shape), v_rows.reshape(kv_shape)]
        elif kind == 1:
            wb, wbs, wa_tail, wkvb = _mla_layouts(mla_wq_b[s], mla_wkv_a[s], mla_wkv_b[s])
            wa = jnp.concatenate([mla_wq_a[s], wa_tail], axis=1).astype(BF16)
            cos, sin = _rope_tables(MLA_ROPE, 2)
            q, ckv, kpe = _mla_proj(x, gain_m, sh_m, sc_m, wa, mla_q_norm[s][None, :], mla_kv_norm[s][None, :],
                                    wb.astype(BF16), wbs.astype(BF16), cos, sin)
            wkvb = wkvb.astype(BF16)
            kv_p = _rows_matmul(ckv[:N_PROMPT], wkvb, BF16)
            o_p = _mla_ctx_attn(q, kv_p, kpe)
            ckv_s = ckv[N_PROMPT:].reshape(DEC_BATCH, DEC_SEQ, MLA_KV_LORA)
            ckv_all = jnp.concatenate([cache_l1_ckv, ckv_s], axis=1).reshape(DEC_BATCH * MLA_KEYS, MLA_KV_LORA)
            kv_s = _rows_matmul(ckv_all, wkvb, BF16)
            kpe_cache = jnp.concatenate(
                [cache_l1_kpe, cache_l1_kpe, jnp.zeros((DEC_BATCH, PAST_LEN, LANES - 2 * MLA_ROPE), F32)], axis=2)
            kpe_all = jnp.concatenate([kpe_cache, kpe[N_PROMPT:].reshape(DEC_BATCH, DEC_SEQ, LANES)], axis=1)
            o_s = _mla_latent_attn(q, kv_s, kpe_all.reshape(DEC_BATCH * MLA_KEYS, LANES))
            w_o = mla_w_o[s].astype(BF16)
            state += [ckv[:N_PROMPT].reshape(BATCH, SEQ, MLA_KV_LORA),
                      kpe[:N_PROMPT, :MLA_ROPE].reshape(BATCH, SEQ, MLA_ROPE)]
        else:
            qw = GQA_HEADS * GQA_HEAD_DIM
            kw = GQA_KV_HEADS * GQA_HEAD_DIM
            w_qkv = jnp.concatenate([_gqa_arrange(gqa_w_qkv[s][:, :qw], 1), gqa_w_qkv[s][:, qw:]], axis=1)
            w_swap = _swap_halves(w_qkv[:, :qw + kw], GQA_HEAD_DIM)
            cos, sin = _rope_tables(GQA_HEAD_DIM, 2)
            qkv, k_rows, v_rows = _norm_proj(x, gain_m, sh_m, sc_m, w_qkv.astype(BF16), kw,
                                             rope=(w_swap.astype(BF16), cos, sin))
            sink = gqa_sink[s]
            o_p = _ctx_attn(qkv, sink, kw, True)
            o_s = _gqa_latent_attn(qkv, sink, cache_l2_k.reshape(DEC_BATCH, PAST_LEN, kw),
                                   cache_l2_v.reshape(DEC_BATCH, PAST_LEN, kw))
            w_o = _gqa_arrange(gqa_w_o[s], 0).astype(BF16)
            kv_shape = (BATCH, SEQ, GQA_KV_HEADS, GQA_HEAD_DIM)
            state += [k_rows.reshape(kv_shape), v_rows.reshape(kv_shape)]
        s = i // 2
        if i % 2 == 0:
            x = _ffn(x, o_p, o_s, g_m, w_o, gain_f, sh_f, sc_f, g_f, _chunk_cols(ffn_w_gate[s], FFN_CHUNK),
                     _chunk_cols(ffn_w_up[s], FFN_CHUNK),
                     ffn_w_down[s].reshape(N_FFN_CHUNKS, FFN_CHUNK, D_MODEL).astype(BF16))
        else:
            x = _moe(s, x, o_p, o_s, g_m, w_o, gain_f, sh_f, sc_f, g_f, moe_router[s],
                     moe_w_gate, moe_w_up, moe_w_down, tri)

    y_prompt = _final_norm(x, norm_final[None, :], 0, N_PROMPT).reshape(BATCH, SEQ, D_MODEL)
    y_sample = _final_norm(x, norm_final[None, :], N_PROMPT, N_SAMPLE).reshape(DEC_BATCH, DEC_SEQ, D_MODEL)
    return (y_prompt, y_sample, *state)
```

```python
import functools
import math

import numpy as np
import jax
import jax.numpy as jnp
from jax import lax
from jax.experimental import pallas as pl
from jax.experimental.pallas import tpu as pltpu

F32 = jnp.float32
BF16 = jnp.bfloat16

D_MODEL = 1024
BATCH = 16
SEQ = 256
DEPTH = 4
DEC_BATCH = 4
DEC_SEQ = 2048
PAST_LEN = 512
GRID_W = 64
GRID_ROWS = DEC_SEQ // GRID_W
NORM_EPS = 1e-6
NEG_INF = -1e30
ROPE_THETA = 10000.0
NA_HEADS = 16
NA_HEAD_DIM = 64
NA_WIN_ROWS = 8
NA_WIN_COLS = 16
MLA_HEADS = 16
MLA_Q_LORA = 512
MLA_KV_LORA = 256
MLA_NOPE = 64
MLA_ROPE = 32
MLA_V = 64
GQA_HEADS = 16
GQA_KV_HEADS = 4
GQA_HEAD_DIM = 64
GQA_WINDOW = 128
FFN_DIM = 2816
N_EXPERTS = 8
EXPERT_DIM = 3584

N_PROMPT = BATCH * SEQ
N_SAMPLE = DEC_BATCH * DEC_SEQ
N_TOK = N_PROMPT + N_SAMPLE
N_COND = 8
LOG2E = math.log2(math.e)
HEAD_SCALE = NA_HEAD_DIM ** -0.5 * LOG2E
MLA_SCALE = (MLA_NOPE + MLA_ROPE) ** -0.5 * LOG2E

LANES = 128
SUBLANES = 8
VMEM_LIMIT = 56 * 1024 * 1024

ROW_TILE = 512
FFN_CHUNK = 256
N_FFN_CHUNKS = FFN_DIM // FFN_CHUNK
EXP_TILE = 1024
EXP_PART = 256
EXP_CHUNK = 512
N_EXP_CHUNKS = EXPERT_DIM // EXP_CHUNK
N_EXP_TILES = (2 * N_TOK) // EXP_TILE + N_EXPERTS - 1
COMBINE_TILE = 256
NA_QROWS = 4
NA_UNION = NA_QROWS + NA_WIN_ROWS
NA_PAIRS_PER_STEP = 8
MLA_KEYS = PAST_LEN + DEC_SEQ


def _cparams(sem):
    return pltpu.CompilerParams(dimension_semantics=sem, vmem_limit_bytes=VMEM_LIMIT)


def _cond_of_row(row):
    return jnp.where(row < N_PROMPT, 0, 1 + (row - N_PROMPT) // DEC_SEQ)


def _nt_dot(a, b):
    return lax.dot_general(a, b, (((1,), (1,)), ((), ())), preferred_element_type=F32)


def _dot(a, b):
    return jnp.dot(a, b, preferred_element_type=F32)


def _rms(x, gain):
    return x * lax.rsqrt(jnp.mean(x * x, axis=-1, keepdims=True) + NORM_EPS) * gain


def _silu(x):
    return x / (1.0 + jnp.exp(-x))


def _split_bf16(x):
    hi = x.astype(BF16)
    lo = (x - hi.astype(F32)).astype(BF16)
    return hi, lo


def _resident(shape):
    nd = len(shape)
    return pl.BlockSpec(shape, lambda *_: (0,) * nd, pipeline_mode=pl.Buffered(1))


def _mod_spec(tile):
    return pl.BlockSpec((1, 1, D_MODEL), lambda i: (_cond_of_row(i * tile), 0, 0))


ADA_TILE = 1536


def _adaln_kernel(c_ref, w_ref, b_ref, o_ref):
    a_hi, a_lo = _split_bf16(_silu(c_ref[...]))
    w_hi, w_lo = _split_bf16(w_ref[0])
    o_ref[0] = _dot(a_hi, w_hi) + _dot(a_lo, w_hi) + _dot(a_hi, w_lo) + b_ref[0]


def _adaln(cond, ada_w, ada_b):
    n = 6 * D_MODEL
    return pl.pallas_call(
        _adaln_kernel,
        grid=(DEPTH, n // ADA_TILE),
        in_specs=[
            pl.BlockSpec((N_COND, D_MODEL), lambda l, j: (0, 0)),
            pl.BlockSpec((1, D_MODEL, ADA_TILE), lambda l, j: (l, 0, j)),
            pl.BlockSpec((1, 1, ADA_TILE), lambda l, j: (l, 0, j)),
        ],
        out_specs=pl.BlockSpec((1, N_COND, ADA_TILE), lambda l, j: (l, 0, j)),
        out_shape=jax.ShapeDtypeStruct((DEPTH, N_COND, n), F32),
        compiler_params=_cparams(("arbitrary", "arbitrary")),
    )(cond, ada_w, ada_b.reshape(DEPTH, 1, n))


PROJ_COLS = 512


def _norm_proj_kernel(x_ref, g_ref, sh_ref, sc_ref, w_ref, *rest, n_out, n_rope, kv_cols):
    rest = list(rest)
    if n_rope:
        w2_ref, cos_ref, sin_ref = rest[:3]
        rest = rest[3:]
    o_ref, k_ref, v_ref = rest
    h = _rms(x_ref[...], g_ref[...]) * (1.0 + sc_ref[0]) + sh_ref[0]
    hb = h.astype(BF16)
    is_prompt = pl.program_id(0) < N_PROMPT // ROW_TILE
    cuts = sorted({0, n_out, n_rope, n_out - 2 * kv_cols, n_out - kv_cols}
                  | set(range(0, n_out, PROJ_COLS)))
    for c0, c1 in zip(cuts[:-1], cuts[1:]):
        cols = slice(c0, c1)
        y = _dot(hb, w_ref[:, cols])
        if c0 < n_rope:
            ys = _dot(hb, w2_ref[:, cols])
            reps = (c1 - c0) // LANES
            cos = jnp.concatenate([cos_ref[...]] * reps, axis=1)
            sin = jnp.concatenate([sin_ref[...]] * reps, axis=1)
            y = y * cos + ys * sin
        is_query = c0 < n_out - 2 * kv_cols
        o_ref[:, cols] = (y * HEAD_SCALE if is_query else y).astype(BF16)
        for kv_ref, start in ((k_ref, n_out - 2 * kv_cols), (v_ref, n_out - kv_cols)):
            if start <= c0 < start + kv_cols:
                @pl.when(is_prompt)
                def _(kv_ref=kv_ref, start=start, y=y, c0=c0, c1=c1):
                    kv_ref[:, c0 - start:c1 - start] = y


def _rope_block(i):
    row = i * ROW_TILE
    return jnp.where(row < N_PROMPT, DEC_SEQ // ROW_TILE, ((row - N_PROMPT) % DEC_SEQ) // ROW_TILE)


def _norm_proj(x, gain, shift, scale, w, kv_cols, rope=None):
    n_out = w.shape[1]
    last_prompt = N_PROMPT // ROW_TILE - 1
    kv_spec = pl.BlockSpec((ROW_TILE, kv_cols), lambda i: (jnp.minimum(i, last_prompt), 0))
    kv_shape = jax.ShapeDtypeStruct((N_PROMPT, kv_cols), F32)
    in_specs = [
        pl.BlockSpec((ROW_TILE, D_MODEL), lambda i: (i, 0)),
        pl.BlockSpec((1, D_MODEL), lambda i: (0, 0)),
        _mod_spec(ROW_TILE),
        _mod_spec(ROW_TILE),
        _resident(w.shape),
    ]
    args = [x, gain, shift, scale, w]
    n_rope = 0
    if rope is not None:
        w2, cos, sin = rope
        n_rope = w2.shape[1]
        in_specs += [
            _resident(w2.shape),
            pl.BlockSpec((ROW_TILE, LANES), lambda i: (_rope_block(i), 0)),
            pl.BlockSpec((ROW_TILE, LANES), lambda i: (_rope_block(i), 0)),
        ]
        args += [w2, cos, sin]
    return pl.pallas_call(
        functools.partial(_norm_proj_kernel, n_out=n_out, n_rope=n_rope, kv_cols=kv_cols),
        grid=(N_TOK // ROW_TILE,),
        in_specs=in_specs,
        out_specs=[pl.BlockSpec((ROW_TILE, n_out), lambda i: (i, 0)), kv_spec, kv_spec],
        out_shape=[jax.ShapeDtypeStruct((N_TOK, n_out), BF16), kv_shape, kv_shape],
        compiler_params=_cparams(("arbitrary",)),
    )(*args)


MLA_Q_COLS = 2 * MLA_HEADS * MLA_NOPE


def _mla_proj_kernel(x_ref, g_ref, sh_ref, sc_ref, wa_ref, qn_ref, kvn_ref, wb_ref, wbs_ref,
                     cos_ref, sin_ref, q_ref, ckv_ref, kpe_ref):
    h = _rms(x_ref[...], g_ref[...]) * (1.0 + sc_ref[0]) + sh_ref[0]
    a = _dot(h.astype(BF16), wa_ref[...])
    c0 = MLA_Q_LORA
    c1 = c0 + MLA_KV_LORA
    cos = cos_ref[...]
    sin = sin_ref[...]
    ckv_ref[...] = _rms(a[:, c0:c1], kvn_ref[...])
    kpe_ref[...] = a[:, c1:c1 + LANES] * cos + a[:, c1 + LANES:c1 + 2 * LANES] * sin
    qn = _rms(a[:, :c0], qn_ref[...]).astype(BF16)
    half = MLA_Q_COLS // 2
    q_scale = MLA_SCALE
    q_ref[:, :half] = (_dot(qn, wb_ref[:, :half]) * q_scale).astype(BF16)
    q_rope = _dot(qn, wb_ref[:, half:])
    q_swap = _dot(qn, wbs_ref[...])
    reps = half // LANES
    cos_r = jnp.concatenate([cos] * reps, axis=1) * q_scale
    sin_r = jnp.concatenate([sin] * reps, axis=1) * q_scale
    q_ref[:, half:] = (q_rope * cos_r + q_swap * sin_r).astype(BF16)


def _mla_proj(x, gain, shift, scale, wa, q_norm, kv_norm, wb, wbs, cos, sin):
    return pl.pallas_call(
        _mla_proj_kernel,
        grid=(N_TOK // ROW_TILE,),
        in_specs=[
            pl.BlockSpec((ROW_TILE, D_MODEL), lambda i: (i, 0)),
            pl.BlockSpec((1, D_MODEL), lambda i: (0, 0)),
            _mod_spec(ROW_TILE),
            _mod_spec(ROW_TILE),
            _resident(wa.shape),
            pl.BlockSpec((1, MLA_Q_LORA), lambda i: (0, 0)),
            pl.BlockSpec((1, MLA_KV_LORA), lambda i: (0, 0)),
            _resident(wb.shape),
            _resident(wbs.shape),
            pl.BlockSpec((ROW_TILE, LANES), lambda i: (_rope_block(i), 0)),
            pl.BlockSpec((ROW_TILE, LANES), lambda i: (_rope_block(i), 0)),
        ],
        out_specs=[
            pl.BlockSpec((ROW_TILE, MLA_Q_COLS), lambda i: (i, 0)),
            pl.BlockSpec((ROW_TILE, MLA_KV_LORA), lambda i: (i, 0)),
            pl.BlockSpec((ROW_TILE, LANES), lambda i: (i, 0)),
        ],
        out_shape=[
            jax.ShapeDtypeStruct((N_TOK, MLA_Q_COLS), BF16),
            jax.ShapeDtypeStruct((N_TOK, MLA_KV_LORA), F32),
            jax.ShapeDtypeStruct((N_TOK, LANES), F32),
        ],
        compiler_params=_cparams(("arbitrary",)),
    )(x, gain, shift, scale, wa, q_norm, kv_norm, wb, wbs, cos, sin)


def _rows_matmul_kernel(x_ref, w_ref, o_ref):
    o_ref[...] = _dot(x_ref[...].astype(BF16), w_ref[...]).astype(o_ref.dtype)


def _rows_matmul(x, w, out_dtype):
    rows, k = x.shape
    n = w.shape[1]
    return pl.pallas_call(
        _rows_matmul_kernel,
        grid=(rows // ROW_TILE,),
        in_specs=[pl.BlockSpec((ROW_TILE, k), lambda i: (i, 0)), _resident(w.shape)],
        out_specs=pl.BlockSpec((ROW_TILE, n), lambda i: (i, 0)),
        out_shape=jax.ShapeDtypeStruct((rows, n), out_dtype),
        compiler_params=_cparams(("arbitrary",)),
    )(x, w)


def _pair_attention(q, masks, keys, values, biases, q_scale, sinks):
    half = LANES // 2
    lane = lax.broadcasted_iota(jnp.int32, (1, LANES), 1)
    outs = []
    sink_terms = []
    for h in range(2):
        qh = jnp.where(masks[h], q, jnp.zeros_like(q))
        if q_scale != 1.0:
            qh = qh * q_scale
        qh = qh.astype(BF16)
        scores = []
        for k, bias in zip(keys, biases):
            s = _nt_dot(qh, k)
            if bias is not None:
                s = s + bias(h)
            scores.append(s)
        m = functools.reduce(jnp.maximum, [jnp.max(s, axis=-1, keepdims=True) for s in scores])
        if sinks is not None:
            m = jnp.maximum(m, sinks[h])
        own = (lane < half) if h == 0 else (lane >= half)
        acc = None
        for s, v in zip(scores, values):
            part = _dot(jnp.exp2(s - m).astype(BF16), jnp.where(own, v, jnp.ones_like(v)))
            acc = part if acc is None else acc + part
        outs.append(acc)
        if sinks is not None:
            sink_terms.append(jnp.exp2(sinks[h] - m))
    first = lane < half
    numer = jnp.where(first, outs[0], outs[1])
    denom = pltpu.roll(jnp.where(first, outs[1], outs[0]), half, axis=1)
    if sinks is not None:
        denom = denom + jnp.where(first, sink_terms[0], sink_terms[1])
    return (numer / denom).astype(BF16)


def _half_masks(width=LANES):
    lane = lax.broadcasted_iota(jnp.int32, (1, width), 1)
    return [lane < LANES // 2, lane >= LANES // 2]


def _ctx_attn_kernel(sink_ref, q_ref, k_ref, v_ref, o_ref, *, kv_blocks, use_sink):
    masks = _half_masks()
    n_q_blocks = o_ref.shape[1] // LANES
    for qb in range(n_q_blocks):
        kb = qb * kv_blocks // n_q_blocks
        cols = slice(kb * LANES, (kb + 1) * LANES)
        sinks = None
        if use_sink:
            base = (qb // 4) * 8 + qb % 4
            sinks = [sink_ref[base] * LOG2E, sink_ref[base + 4] * LOG2E]
        o_ref[:, qb * LANES:(qb + 1) * LANES] = _pair_attention(
            q_ref[:, qb * LANES:(qb + 1) * LANES], masks,
            [k_ref[:, cols]], [v_ref[:, cols]], [None], 1.0, sinks)


def _ctx_attn(qkv, sink, kv_width, use_sink):
    qw = NA_HEADS * NA_HEAD_DIM
    k_blk = qw // kv_width
    return pl.pallas_call(
        functools.partial(_ctx_attn_kernel, kv_blocks=kv_width // LANES, use_sink=use_sink),
        grid_spec=pltpu.PrefetchScalarGridSpec(
            num_scalar_prefetch=1,
            grid=(BATCH,),
            in_specs=[
                pl.BlockSpec((SEQ, qw), lambda b, s: (b, 0)),
                pl.BlockSpec((SEQ, kv_width), lambda b, s: (b, k_blk)),
                pl.BlockSpec((SEQ, kv_width), lambda b, s: (b, k_blk + 1)),
            ],
            out_specs=pl.BlockSpec((SEQ, qw), lambda b, s: (b, 0)),
        ),
        out_shape=jax.ShapeDtypeStruct((N_PROMPT, qw), BF16),
        compiler_params=_cparams(("arbitrary",)),
    )(sink, qkv, qkv, qkv)


def _na_union_start(rb):
    return jnp.clip(rb * NA_QROWS - NA_WIN_ROWS // 2, 0, GRID_ROWS - NA_UNION)


NA_DR_SLOTS = 2 * NA_WIN_ROWS


def _na_bias(pairs_ref, h, rb):
    u0 = _na_union_start(rb)
    left = lax.broadcasted_iota(jnp.int32, (1, LANES), 1) < GRID_W
    rows = []
    for j in range(NA_QROWS):
        r = rb * NA_QROWS + j
        start = jnp.clip(r - NA_WIN_ROWS // 2, 0, GRID_ROWS - NA_WIN_ROWS)
        blocks = []
        for ip in range(NA_UNION // 2):
            key_row = u0 + 2 * ip
            dr = key_row - r + NA_WIN_ROWS - 1
            ok_l = ((key_row >= start) & (key_row < start + NA_WIN_ROWS)).astype(jnp.int32)
            ok_r = ((key_row + 1 >= start) & (key_row + 1 < start + NA_WIN_ROWS)).astype(jnp.int32)
            blk = pairs_ref[h, jnp.clip(dr + 1, 0, NA_DR_SLOTS - 1)]
            blocks.append(jnp.where(jnp.where(left, ok_l, ok_r) > 0, blk, NEG_INF))
        rows.append(jnp.concatenate(blocks, axis=1))
    return jnp.concatenate(rows, axis=0)


def _na_latent_kernel(q_ref, k_ref, v_ref, kc_ref, vc_ref, pairs_ref, o_ref):
    rb = pl.program_id(2)
    start = pl.multiple_of(_na_union_start(rb) * GRID_W, GRID_W)
    n_loc = NA_UNION * GRID_W
    for p in range(NA_PAIRS_PER_STEP):
        cols = slice(p * LANES, (p + 1) * LANES)
        k_loc = k_ref[pl.ds(start, n_loc), cols]
        v_loc = v_ref[pl.ds(start, n_loc), cols]
        o_ref[:, cols] = _pair_attention(
            q_ref[:, cols], _half_masks(),
            [k_loc, kc_ref[0, :, cols].astype(BF16)], [v_loc, vc_ref[0, :, cols].astype(BF16)],
            [lambda h, p=p: _na_bias(pairs_ref, 2 * p + h, rb), None], 1.0, None)


def _na_latent_attn(qkv, k_ctx, v_ctx, pairs):
    qw = NA_HEADS * NA_HEAD_DIM
    width = NA_PAIRS_PER_STEP * LANES
    n_groups = qw // width
    q_rows = NA_QROWS * GRID_W
    n_rb = DEC_SEQ // q_rows
    q_off = N_PROMPT // q_rows
    kv_off = N_PROMPT // DEC_SEQ
    return pl.pallas_call(
        _na_latent_kernel,
        grid=(DEC_BATCH, n_groups, n_rb),
        in_specs=[
            pl.BlockSpec((q_rows, width), lambda b, j, r: (q_off + b * n_rb + r, j)),
            pl.BlockSpec((DEC_SEQ, width), lambda b, j, r: (kv_off + b, n_groups + j)),
            pl.BlockSpec((DEC_SEQ, width), lambda b, j, r: (kv_off + b, 2 * n_groups + j)),
            pl.BlockSpec((1, PAST_LEN, width), lambda b, j, r: (b, 0, j)),
            pl.BlockSpec((1, PAST_LEN, width), lambda b, j, r: (b, 0, j)),
            pl.BlockSpec((2 * NA_PAIRS_PER_STEP, NA_DR_SLOTS, GRID_W, LANES), lambda b, j, r: (j, 0, 0, 0)),
        ],
        out_specs=pl.BlockSpec((q_rows, width), lambda b, j, r: (b * n_rb + r, j)),
        out_shape=jax.ShapeDtypeStruct((N_SAMPLE, qw), BF16),
        compiler_params=_cparams(("arbitrary", "arbitrary", "arbitrary")),
    )(qkv, qkv, qkv, k_ctx, v_ctx, pairs)


def _na_pair_table(rel_bias):
    cols = np.arange(GRID_W)
    col_start = np.clip(cols - NA_WIN_COLS // 2, 0, GRID_W - NA_WIN_COLS)
    col_ok = (cols[None, :] >= col_start[:, None]) & (cols[None, :] < col_start[:, None] + NA_WIN_COLS)
    dc = np.clip(cols[None, :] - cols[:, None] + NA_WIN_COLS - 1, 0, 2 * NA_WIN_COLS - 2)
    n_dc = 2 * NA_WIN_COLS - 1
    n_dr = 2 * NA_WIN_ROWS - 1
    pick = np.zeros((n_dc, GRID_W * GRID_W), np.float32)
    pick[dc.reshape(-1), np.arange(GRID_W * GRID_W)] = 1.0
    flat = jnp.dot(rel_bias.astype(F32).reshape(NA_HEADS * n_dr, n_dc), jnp.asarray(pick),
                   precision=lax.Precision.HIGHEST)
    bias = jnp.where(col_ok.reshape(1, 1, -1), flat.reshape(NA_HEADS, n_dr, -1) * LOG2E, NEG_INF)
    bias = bias.reshape(NA_HEADS, n_dr, GRID_W, GRID_W)
    masked = jnp.full((NA_HEADS, 1, GRID_W, GRID_W), NEG_INF, F32)
    ext = jnp.concatenate([masked, bias, masked], axis=1)
    return jnp.concatenate([ext[:, :-1], ext[:, 1:]], axis=-1)


GQA_Q_BLOCK = GQA_WINDOW
GQA_GROUP = GQA_HEADS // GQA_KV_HEADS


def _gqa_latent_kernel(sink_ref, q_ref, kp_ref, kc_ref, kn_ref, vp_ref, vc_ref, vn_ref,
                       kx_ref, vx_ref, o_ref):
    qb = pl.program_id(1)
    n_qb = pl.num_programs(1)
    blk = GQA_Q_BLOCK
    rows = GQA_GROUP * blk
    qq = lax.broadcasted_iota(jnp.int32, (rows, blk), 0) % blk
    kk = lax.broadcasted_iota(jnp.int32, (rows, blk), 1)
    neg = jnp.float32(NEG_INF)
    bias_prev = jnp.where((kk >= qq) & (qb > 0), 0.0, neg)
    bias_next = jnp.where((kk <= qq) & (qb < n_qb - 1), 0.0, neg)
    row_grp = lax.broadcasted_iota(jnp.int32, (rows, 1), 0) // blk
    for pair in range(GQA_KV_HEADS // 2):
        kv = slice(pair * LANES, (pair + 1) * LANES)
        q0 = pair * GQA_GROUP * LANES
        q = jnp.concatenate([q_ref[:, q0 + m * LANES:q0 + (m + 1) * LANES] for m in range(GQA_GROUP)], axis=0)
        sinks = []
        for h in range(2):
            base = pair * 2 * GQA_GROUP + h * GQA_GROUP
            col = jnp.zeros((rows, 1), F32)
            for m in range(GQA_GROUP):
                col = jnp.where(row_grp == m, sink_ref[base + m] * LOG2E, col)
            sinks.append(col)
        o = _pair_attention(
            q, _half_masks(),
            [kp_ref[:, kv], kc_ref[:, kv], kn_ref[:, kv], kx_ref[0, :, kv].astype(BF16)],
            [vp_ref[:, kv], vc_ref[:, kv], vn_ref[:, kv], vx_ref[0, :, kv].astype(BF16)],
            [lambda h: bias_prev, None, lambda h: bias_next, None], 1.0, sinks)
        for m in range(GQA_GROUP):
            o_ref[:, q0 + m * LANES:q0 + (m + 1) * LANES] = o[m * blk:(m + 1) * blk]


def _gqa_latent_attn(qkv, sink, k_ctx, v_ctx):
    qw = GQA_HEADS * GQA_HEAD_DIM
    blk = GQA_Q_BLOCK
    n_qb = DEC_SEQ // blk
    q_off = N_PROMPT // blk
    kw = GQA_KV_HEADS * GQA_HEAD_DIM
    k_col = qw // kw
    v_col = k_col + 1

    def kv_spec(col, shift):
        def imap(b, i, s):
            return (q_off + b * n_qb + jnp.clip(i + shift, 0, n_qb - 1), col)
        return pl.BlockSpec((blk, kw), imap)

    ctx_spec = pl.BlockSpec((1, PAST_LEN, kw), lambda b, i, s: (b, 0, 0))
    return pl.pallas_call(
        _gqa_latent_kernel,
        grid_spec=pltpu.PrefetchScalarGridSpec(
            num_scalar_prefetch=1,
            grid=(DEC_BATCH, n_qb),
            in_specs=[
                pl.BlockSpec((blk, qw), lambda b, i, s: (q_off + b * n_qb + i, 0)),
                kv_spec(k_col, -1), kv_spec(k_col, 0), kv_spec(k_col, 1),
                kv_spec(v_col, -1), kv_spec(v_col, 0), kv_spec(v_col, 1),
                ctx_spec, ctx_spec,
            ],
            out_specs=pl.BlockSpec((blk, qw), lambda b, i, s: (b * n_qb + i, 0)),
        ),
        out_shape=jax.ShapeDtypeStruct((N_SAMPLE, qw), BF16),
        compiler_params=_cparams(("arbitrary", "arbitrary")),
    )(sink, qkv, qkv, qkv, qkv, qkv, qkv, qkv, k_ctx, v_ctx)


MLA_PAIRS = MLA_HEADS // 2


def _mla_masks():
    lane = lax.broadcasted_iota(jnp.int32, (1, 2 * LANES), 1)
    half = LANES // 2
    even = (lane < half) | ((lane >= LANES) & (lane < LANES + MLA_ROPE))
    odd = ((lane >= half) & (lane < LANES)) | ((lane >= LANES + MLA_ROPE) & (lane < LANES + 2 * MLA_ROPE))
    return [even, odd]


def _mla_ctx_kernel(q_ref, kv_ref, kpe_ref, o_ref):
    masks = _mla_masks()
    kpe = kpe_ref[...].astype(BF16)
    half = MLA_Q_COLS // 2
    for j in range(MLA_PAIRS):
        cols = slice(j * LANES, (j + 1) * LANES)
        rcols = slice(half + j * LANES, half + (j + 1) * LANES)
        q = jnp.concatenate([q_ref[:, cols], q_ref[:, rcols]], axis=1)
        k = jnp.concatenate([kv_ref[:, cols], kpe], axis=1)
        o_ref[:, cols] = _pair_attention(q, masks, [k], [kv_ref[:, rcols]], [None], 1.0, None)


def _mla_ctx_attn(q, kv, kpe):
    ow = MLA_HEADS * MLA_V
    return pl.pallas_call(
        _mla_ctx_kernel,
        grid=(BATCH,),
        in_specs=[
            pl.BlockSpec((SEQ, MLA_Q_COLS), lambda b: (b, 0)),
            pl.BlockSpec((SEQ, MLA_Q_COLS), lambda b: (b, 0)),
            pl.BlockSpec((SEQ, LANES), lambda b: (b, 0)),
        ],
        out_specs=pl.BlockSpec((SEQ, ow), lambda b: (b, 0)),
        out_shape=jax.ShapeDtypeStruct((N_PROMPT, ow), BF16),
        compiler_params=_cparams(("arbitrary",)),
    )(q, kv, kpe)


MLA_Q_BLOCK = 256


MLA_PAIRS_PER_STEP = 8


def _mla_latent_kernel(qn_ref, qr_ref, ka_ref, kpe_ref, v_ref, o_ref, kcat_ref):
    @pl.when(pl.program_id(2) == 0)
    def _():
        kpe = kpe_ref[...].astype(BF16)
        for p in range(MLA_PAIRS_PER_STEP):
            kcat_ref[p, :, :LANES] = ka_ref[:, p * LANES:(p + 1) * LANES]
            kcat_ref[p, :, LANES:] = kpe

    for p in range(MLA_PAIRS_PER_STEP):
        cols = slice(p * LANES, (p + 1) * LANES)
        q = jnp.concatenate([qn_ref[:, cols], qr_ref[:, cols]], axis=1)
        o_ref[:, cols] = _pair_attention(q, _mla_masks(), [kcat_ref[p]], [v_ref[:, cols]], [None], 1.0, None)


def _mla_latent_attn(q, kv, kpe):
    ow = MLA_HEADS * MLA_V
    n_qb = DEC_SEQ // MLA_Q_BLOCK
    q_off = N_PROMPT // MLA_Q_BLOCK
    width = MLA_PAIRS_PER_STEP * LANES
    n_groups = MLA_PAIRS // MLA_PAIRS_PER_STEP
    return pl.pallas_call(
        _mla_latent_kernel,
        grid=(DEC_BATCH, n_groups, n_qb),
        in_specs=[
            pl.BlockSpec((MLA_Q_BLOCK, width), lambda b, j, i: (q_off + b * n_qb + i, j)),
            pl.BlockSpec((MLA_Q_BLOCK, width), lambda b, j, i: (q_off + b * n_qb + i, n_groups + j)),
            pl.BlockSpec((MLA_KEYS, width), lambda b, j, i: (b, j)),
            pl.BlockSpec((MLA_KEYS, LANES), lambda b, j, i: (b, 0)),
            pl.BlockSpec((MLA_KEYS, width), lambda b, j, i: (b, n_groups + j)),
        ],
        out_specs=pl.BlockSpec((MLA_Q_BLOCK, width), lambda b, j, i: (b * n_qb + i, j)),
        out_shape=jax.ShapeDtypeStruct((N_SAMPLE, ow), BF16),
        scratch_shapes=[pltpu.VMEM((MLA_PAIRS_PER_STEP, MLA_KEYS, 2 * LANES), BF16)],
        compiler_params=_cparams(("arbitrary", "arbitrary", "arbitrary")),
    )(q, q, kv, kpe, kv)


def _mixer_residual(x_ref, op_ref, os_ref, gate_m_ref, wo_ref):
    is_prompt = pl.program_id(0) < N_PROMPT // ROW_TILE
    o = jnp.where(is_prompt, op_ref[...], os_ref[...])
    return x_ref[...] + gate_m_ref[0] * _dot(o, wo_ref[...])


def _attn_out_specs(k):
    n_p = N_PROMPT // ROW_TILE
    return [
        pl.BlockSpec((ROW_TILE, k), lambda i: (jnp.minimum(i, n_p - 1), 0)),
        pl.BlockSpec((ROW_TILE, k), lambda i: (jnp.maximum(i - n_p, 0), 0)),
        _mod_spec(ROW_TILE),
    ]


def _ffn_kernel(x_ref, op_ref, os_ref, gate_m_ref, wo_ref, g_ref, sh_ref, sc_ref, gate_ref,
                wg_ref, wu_ref, wd_ref, y_ref, acc_ref):
    x = _mixer_residual(x_ref, op_ref, os_ref, gate_m_ref, wo_ref)
    hb = (_rms(x, g_ref[...]) * (1.0 + sc_ref[0]) + sh_ref[0]).astype(BF16)
    acc_ref[...] = jnp.zeros_like(acc_ref)

    def chunk(c, carry):
        a = _silu(_dot(hb, wg_ref[c])) * _dot(hb, wu_ref[c])
        acc_ref[...] += _dot(a.astype(BF16), wd_ref[c])
        return carry

    lax.fori_loop(0, N_FFN_CHUNKS, chunk, 0)
    y_ref[...] = x + gate_ref[0] * acc_ref[...]


def _ffn(x, o_prompt, o_sample, gate_m, w_o, gain, shift, scale, gate, wg, wu, wd):
    return pl.pallas_call(
        _ffn_kernel,
        grid=(N_TOK // ROW_TILE,),
        in_specs=[
            pl.BlockSpec((ROW_TILE, D_MODEL), lambda i: (i, 0)),
            *_attn_out_specs(w_o.shape[0]),
            _resident(w_o.shape),
            pl.BlockSpec((1, D_MODEL), lambda i: (0, 0)),
            _mod_spec(ROW_TILE), _mod_spec(ROW_TILE), _mod_spec(ROW_TILE),
            _resident(wg.shape), _resident(wu.shape), _resident(wd.shape),
        ],
        out_specs=pl.BlockSpec((ROW_TILE, D_MODEL), lambda i: (i, 0)),
        out_shape=jax.ShapeDtypeStruct((N_TOK, D_MODEL), F32),
        scratch_shapes=[pltpu.VMEM((ROW_TILE, D_MODEL), F32)],
        compiler_params=_cparams(("arbitrary",)),
    )(x, o_prompt, o_sample, gate_m, w_o, gain, shift, scale, gate, wg, wu, wd)


SLAB = D_MODEL // LANES


def _store_slabs(ref, x):
    n = x.shape[0]
    for s in range(SLAB):
        ref[pl.ds(s, n, stride=SLAB), :] = x[:, s * LANES:(s + 1) * LANES]


def _load_slabs(ref, n, dtype, row0=0):
    return jnp.concatenate([ref[pl.ds(row0 + s, n, stride=SLAB), :].astype(dtype) for s in range(SLAB)], axis=1)


META_E1, META_E2, META_W1, META_W2, META_R1, META_R2 = range(6)


def _router_kernel(x_ref, op_ref, os_ref, gate_m_ref, wo_ref, g_ref, sh_ref, sc_ref, whi_ref, wlo_ref, tri_ref,
                   x1_ref, h_ref, meta_ref, cnt_ref, carry_ref):
    @pl.when(pl.program_id(0) == 0)
    def _():
        carry_ref[...] = jnp.zeros_like(carry_ref)

    x = _mixer_residual(x_ref, op_ref, os_ref, gate_m_ref, wo_ref)
    x1_ref[...] = x
    h = _rms(x, g_ref[...]) * (1.0 + sc_ref[0]) + sh_ref[0]
    _store_slabs(h_ref, h)
    h_hi, h_lo = _split_bf16(h)
    logits = _dot(h_hi, whi_ref[...]) + _dot(h_lo, whi_ref[...]) + _dot(h_hi, wlo_ref[...])
    lane = lax.broadcasted_iota(jnp.int32, logits.shape, 1)
    lane_f = lane.astype(F32)
    ninf = jnp.float32(-jnp.inf)
    lg = jnp.where(lane < N_EXPERTS, logits, ninf)
    m1 = jnp.max(lg, axis=-1, keepdims=True)
    i1 = jnp.min(jnp.where(lg == m1, lane_f, float(LANES)), axis=-1, keepdims=True)
    lg2 = jnp.where(lane_f == i1, ninf, lg)
    m2 = jnp.max(lg2, axis=-1, keepdims=True)
    i2 = jnp.min(jnp.where(lg2 == m2, lane_f, float(LANES)), axis=-1, keepdims=True)
    e = jnp.exp(m2 - m1)
    w1 = 1.0 / (1.0 + e)
    w2 = e / (1.0 + e)
    sel1 = lane_f == i1
    sel2 = lane_f == i2
    onehot = jnp.where(sel1 | sel2, 1.0, 0.0)
    ranks = _dot(tri_ref[...], onehot.astype(BF16)) + carry_ref[...]
    r1 = jnp.sum(jnp.where(sel1, ranks, 0.0), axis=-1, keepdims=True)
    r2 = jnp.sum(jnp.where(sel2, ranks, 0.0), axis=-1, keepdims=True)
    carry_ref[...] += jnp.sum(onehot, axis=0, keepdims=True)
    cnt_ref[...] = carry_ref[...]
    meta = jnp.zeros(logits.shape, F32)
    for idx, val in ((META_E1, i1), (META_E2, i2), (META_W1, w1), (META_W2, w2), (META_R1, r1), (META_R2, r2)):
        meta = jnp.where(lane == idx, val, meta)
    meta_ref[...] = meta


def _router(x, o_prompt, o_sample, gate_m, w_o, gain, shift, scale, w_hi, w_lo, tri):
    return pl.pallas_call(
        _router_kernel,
        grid=(N_TOK // ROW_TILE,),
        in_specs=[
            pl.BlockSpec((ROW_TILE, D_MODEL), lambda i: (i, 0)),
            *_attn_out_specs(w_o.shape[0]),
            _resident(w_o.shape),
            pl.BlockSpec((1, D_MODEL), lambda i: (0, 0)),
            _mod_spec(ROW_TILE), _mod_spec(ROW_TILE),
            _resident(w_hi.shape), _resident(w_lo.shape), _resident(tri.shape),
        ],
        out_specs=[
            pl.BlockSpec((ROW_TILE, D_MODEL), lambda i: (i, 0)),
            pl.BlockSpec((ROW_TILE * SLAB, LANES), lambda i: (i, 0)),
            pl.BlockSpec((ROW_TILE, LANES), lambda i: (i, 0)),
            pl.BlockSpec((1, LANES), lambda i: (0, 0)),
        ],
        out_shape=[
            jax.ShapeDtypeStruct((N_TOK, D_MODEL), F32),
            jax.ShapeDtypeStruct((N_TOK * SLAB, LANES), F32),
            jax.ShapeDtypeStruct((N_TOK, LANES), F32),
            jax.ShapeDtypeStruct((1, LANES), F32),
        ],
        scratch_shapes=[pltpu.VMEM((1, LANES), F32)],
        compiler_params=_cparams(("arbitrary",)),
    )(x, o_prompt, o_sample, gate_m, w_o, gain, shift, scale, w_hi, w_lo, tri)


def _route_plan(meta, counts):
    cnt = counts[0, :N_EXPERTS].astype(jnp.int32)
    tiles = (cnt + EXP_TILE - 1) // EXP_TILE
    tile_end = jnp.cumsum(tiles)
    tile_start = tile_end - tiles
    offs = tile_start * EXP_TILE
    experts = jnp.arange(N_EXPERTS, dtype=jnp.int32)

    def lookup(table, idx):
        return jnp.sum(jnp.where(idx[:, None] == experts[None, :], table[None, :], 0), axis=1)

    e1 = meta[:, META_E1].astype(jnp.int32)
    e2 = meta[:, META_E2].astype(jnp.int32)
    pos1 = lookup(offs, e1) + meta[:, META_R1].astype(jnp.int32)
    pos2 = lookup(offs, e2) + meta[:, META_R2].astype(jnp.int32)
    t = jnp.arange(N_EXP_TILES, dtype=jnp.int32)
    tile_expert = jnp.minimum(jnp.sum(t[:, None] >= tile_end[None, :], axis=1), N_EXPERTS - 1).astype(jnp.int32)
    in_group = (t - lookup(tile_start, tile_expert)) * EXP_TILE
    tile_rows = jnp.where(t < tile_end[-1], jnp.clip(lookup(cnt, tile_expert) - in_group, 0, EXP_TILE),
                          0).astype(jnp.int32)
    return pos1, pos2, tile_expert, tile_rows


DISPATCH_TILE = 512


def _dispatch_kernel(pos1_ref, pos2_ref, h_ref, init_ref, xs_ref, sem):
    del init_ref
    base = pl.program_id(0) * DISPATCH_TILE

    def row_copy(t, dst):
        return pltpu.make_async_copy(
            h_ref.at[pl.ds(pl.multiple_of(t * SLAB, SLAB), SLAB)],
            xs_ref.at[pl.ds(pl.multiple_of(dst * SLAB, SLAB), SLAB)], sem)

    def issue(t, carry):
        row_copy(t, pos1_ref[base + t]).start()
        row_copy(t, pos2_ref[base + t]).start()
        return carry

    lax.fori_loop(0, DISPATCH_TILE, issue, 0)

    def drain(t, carry):
        row_copy(0, 0).wait()
        row_copy(0, 0).wait()
        return carry

    lax.fori_loop(0, DISPATCH_TILE, drain, 0)


def _dispatch(pos1, pos2, h_slabs, xs_init):
    return pl.pallas_call(
        _dispatch_kernel,
        grid_spec=pltpu.PrefetchScalarGridSpec(
            num_scalar_prefetch=2,
            grid=(N_TOK // DISPATCH_TILE,),
            in_specs=[pl.BlockSpec((DISPATCH_TILE * SLAB, LANES), lambda i, p1, p2: (i, 0)),
                      pl.BlockSpec(memory_space=pl.ANY)],
            out_specs=pl.BlockSpec(memory_space=pl.ANY),
            scratch_shapes=[pltpu.SemaphoreType.DMA(())],
        ),
        out_shape=jax.ShapeDtypeStruct(xs_init.shape, F32),
        input_output_aliases={3: 0},
        compiler_params=_cparams(("arbitrary",)),
    )(pos1, pos2, h_slabs, xs_init)


def _expert_kernel(te_ref, tr_ref, xs_ref, wg_ref, wu_ref, wd_ref, ys_ref, xb_ref, acc_ref):
    t = pl.program_id(0)
    c = pl.program_id(1)

    @pl.when(c == 0)
    def _():
        xb_ref[...] = _load_slabs(xs_ref, EXP_TILE, BF16)
        acc_ref[...] = jnp.zeros_like(acc_ref)

    rows = tr_ref[t]

    def swiglu_rows(n):
        xb = xb_ref[:n]
        a = _silu(_dot(xb, wg_ref[...].astype(BF16))) * _dot(xb, wu_ref[...].astype(BF16))
        acc_ref[:n] += _dot(a.astype(BF16), wd_ref[...].astype(BF16))

    for parts in range(1, EXP_TILE // EXP_PART + 1):
        @pl.when((rows > (parts - 1) * EXP_PART) & (rows <= parts * EXP_PART))
        def _(parts=parts):
            swiglu_rows(parts * EXP_PART)

    @pl.when(c == N_EXP_CHUNKS - 1)
    def _():
        _store_slabs(ys_ref, acc_ref[...])


def _experts(layer, tile_expert, tile_rows, xs, wg, wu, wd):
    def chunk_of(t, c, tr):
        return jnp.where(tr[t] > 0, c, N_EXP_CHUNKS - 1)

    return pl.pallas_call(
        _expert_kernel,
        grid_spec=pltpu.PrefetchScalarGridSpec(
            num_scalar_prefetch=2,
            grid=(N_EXP_TILES, N_EXP_CHUNKS),
            in_specs=[
                pl.BlockSpec((EXP_TILE * SLAB, LANES), lambda t, c, te, tr: (t, 0)),
                pl.BlockSpec((None, None, D_MODEL, EXP_CHUNK),
                             lambda t, c, te, tr: (layer, te[t], 0, chunk_of(t, c, tr))),
                pl.BlockSpec((None, None, D_MODEL, EXP_CHUNK),
                             lambda t, c, te, tr: (layer, te[t], 0, chunk_of(t, c, tr))),
                pl.BlockSpec((None, None, EXP_CHUNK, D_MODEL),
                             lambda t, c, te, tr: (layer, te[t], chunk_of(t, c, tr), 0)),
            ],
            out_specs=pl.BlockSpec((EXP_TILE * SLAB, LANES), lambda t, c, te, tr: (t, 0)),
            scratch_shapes=[pltpu.VMEM((EXP_TILE, D_MODEL), BF16), pltpu.VMEM((EXP_TILE, D_MODEL), F32)],
        ),
        out_shape=jax.ShapeDtypeStruct(xs.shape, F32),
        compiler_params=_cparams(("arbitrary", "arbitrary")),
    )(tile_expert, tile_rows, xs, wg, wu, wd)


def _combine_kernel(pos1_ref, pos2_ref, x_ref, gate_ref, meta_ref, ys_ref, y_ref, buf_ref, sem):
    i = pl.program_id(0)
    n = COMBINE_TILE
    slot = i % 2

    def fetch(step, dst):
        base = step * n

        def body(t, carry):
            for pos_ref, row in ((pos1_ref, t), (pos2_ref, n + t)):
                pltpu.make_async_copy(
                    ys_ref.at[pl.ds(pl.multiple_of(pos_ref[base + t] * SLAB, SLAB), SLAB)],
                    buf_ref.at[dst, pl.ds(pl.multiple_of(row * SLAB, SLAB), SLAB)], sem.at[dst]).start()
            return carry

        lax.fori_loop(0, n, body, 0)

    @pl.when(i == 0)
    def _():
        fetch(0, 0)

    @pl.when(i + 1 < pl.num_programs(0))
    def _():
        fetch(i + 1, 1 - slot)

    pltpu.make_async_copy(ys_ref.at[pl.ds(0, 2 * n * SLAB)], buf_ref.at[slot], sem.at[slot]).wait()
    meta = meta_ref[...]
    w1 = meta[:, META_W1:META_W1 + 1]
    w2 = meta[:, META_W2:META_W2 + 1]
    rows = buf_ref.at[slot]
    y1 = _load_slabs(rows, n, F32)
    y2 = _load_slabs(rows, n, F32, row0=n * SLAB)
    y_ref[...] = x_ref[...] + gate_ref[0] * (w1 * y1 + w2 * y2)


def _combine(pos1, pos2, x, gate, meta, ys):
    return pl.pallas_call(
        _combine_kernel,
        grid_spec=pltpu.PrefetchScalarGridSpec(
            num_scalar_prefetch=2,
            grid=(N_TOK // COMBINE_TILE,),
            in_specs=[
                pl.BlockSpec((COMBINE_TILE, D_MODEL), lambda i, p1, p2: (i, 0)),
                pl.BlockSpec((1, 1, D_MODEL), lambda i, p1, p2: (_cond_of_row(i * COMBINE_TILE), 0, 0)),
                pl.BlockSpec((COMBINE_TILE, LANES), lambda i, p1, p2: (i, 0)),
                pl.BlockSpec(memory_space=pl.ANY),
            ],
            out_specs=pl.BlockSpec((COMBINE_TILE, D_MODEL), lambda i, p1, p2: (i, 0)),
            scratch_shapes=[pltpu.VMEM((2, 2 * COMBINE_TILE * SLAB, LANES), F32), pltpu.SemaphoreType.DMA((2,))],
        ),
        out_shape=jax.ShapeDtypeStruct((N_TOK, D_MODEL), F32),
        compiler_params=_cparams(("arbitrary",)),
    )(pos1, pos2, x, gate, meta, ys)


def _moe(layer, x, o_prompt, o_sample, gate_m, w_o, gain, shift, scale, gate, router_w, wg, wu, wd, tri):
    w = jnp.pad(router_w, ((0, 0), (0, LANES - N_EXPERTS)))
    w_hi = w.astype(BF16)
    w_lo = (w - w_hi.astype(F32)).astype(BF16)
    x1, h_slabs, meta, counts = _router(x, o_prompt, o_sample, gate_m, w_o, gain, shift, scale, w_hi, w_lo, tri)
    pos1, pos2, tile_expert, tile_rows = _route_plan(meta, counts)
    xs = _dispatch(pos1, pos2, h_slabs, jnp.zeros((N_EXP_TILES * EXP_TILE * SLAB, LANES), F32))
    ys = _experts(layer, tile_expert, tile_rows, xs, wg, wu, wd)
    return _combine(pos1, pos2, x1, gate, meta, ys)


def _final_norm_kernel(x_ref, g_ref, y_ref):
    y_ref[...] = _rms(x_ref[...], g_ref[...])


def _final_norm(x, gain, row0, rows):
    off = row0 // ROW_TILE
    return pl.pallas_call(
        _final_norm_kernel,
        grid=(rows // ROW_TILE,),
        in_specs=[
            pl.BlockSpec((ROW_TILE, D_MODEL), lambda i: (off + i, 0)),
            pl.BlockSpec((1, D_MODEL), lambda i: (0, 0)),
        ],
        out_specs=pl.BlockSpec((ROW_TILE, D_MODEL), lambda i: (i, 0)),
        out_shape=jax.ShapeDtypeStruct((rows, D_MODEL), F32),
        compiler_params=_cparams(("arbitrary",)),
    )(x, gain)


def _rope_tables(rot_dim, reps):
    n_freq = rot_dim // 4
    inv_freq = ROPE_THETA ** (-np.arange(n_freq, dtype=np.float64) / n_freq)
    t = np.arange(DEC_SEQ)
    ang_r = (t // GRID_W)[:, None] * inv_freq
    ang_c = (t % GRID_W)[:, None] * inv_freq
    cos = np.concatenate([np.cos(ang_r), np.cos(ang_r), np.cos(ang_c), np.cos(ang_c)], axis=1)
    sin = np.concatenate([-np.sin(ang_r), np.sin(ang_r), -np.sin(ang_c), np.sin(ang_c)], axis=1)
    pad = LANES - reps * rot_dim
    cos = np.concatenate([cos] * reps + [np.ones((DEC_SEQ, pad))], axis=1)
    sin = np.concatenate([sin] * reps + [np.zeros((DEC_SEQ, pad))], axis=1)
    cos = np.concatenate([cos, np.ones((ROW_TILE, LANES))], axis=0)
    sin = np.concatenate([sin, np.zeros((ROW_TILE, LANES))], axis=0)
    return jnp.asarray(cos, F32), jnp.asarray(sin, F32)


def _swap_halves(w, rot_dim):
    k, n = w.shape
    q = rot_dim // 4
    return w.reshape(k, n // (2 * q), 2, q)[:, :, ::-1, :].reshape(k, n)


def _gqa_arrange(w_q, axis):
    pairs = GQA_KV_HEADS // 2
    shape = w_q.shape
    split = shape[:axis] + (pairs, 2, GQA_GROUP, GQA_HEAD_DIM) + shape[axis + 1:]
    order = list(range(len(split)))
    order[axis + 1], order[axis + 2] = order[axis + 2], order[axis + 1]
    return w_q.reshape(split).transpose(order).reshape(shape)


def _mla_layouts(wq_b, wkv_a, wkv_b):
    hd = MLA_NOPE + MLA_ROPE
    w3 = wq_b.reshape(MLA_Q_LORA, MLA_HEADS, hd)
    nope = w3[:, :, :MLA_NOPE].reshape(MLA_Q_LORA, MLA_HEADS * MLA_NOPE)
    rope = w3[:, :, MLA_NOPE:].reshape(MLA_Q_LORA, MLA_HEADS * MLA_ROPE)

    def rope_blocks(r):
        r = r.reshape(-1, MLA_PAIRS, 2 * MLA_ROPE)
        return jnp.pad(r, ((0, 0), (0, 0), (0, LANES - 2 * MLA_ROPE))).reshape(-1, MLA_PAIRS * LANES)

    wb = jnp.concatenate([nope, rope_blocks(rope)], axis=1)
    wbs = rope_blocks(_swap_halves(rope, MLA_ROPE))
    kpe_w = wkv_a[:, MLA_KV_LORA:]
    kpe_ws = _swap_halves(kpe_w, MLA_ROPE)
    zeros_a = jnp.zeros((D_MODEL, LANES - 2 * MLA_ROPE), F32)
    wa_tail = jnp.concatenate([wkv_a[:, :MLA_KV_LORA], kpe_w, kpe_w, zeros_a, kpe_ws, kpe_ws, zeros_a], axis=1)
    kv3 = wkv_b.reshape(MLA_KV_LORA, MLA_HEADS, MLA_NOPE + MLA_V)
    wkvb = jnp.concatenate([kv3[:, :, :MLA_NOPE].reshape(MLA_KV_LORA, -1),
                            kv3[:, :, MLA_NOPE:].reshape(MLA_KV_LORA, -1)], axis=1)
    return wb, wbs, wa_tail, wkvb


def _chunk_cols(w, chunk):
    k, n = w.shape
    return w.reshape(k, n // chunk, chunk).transpose(1, 0, 2).astype(BF16)


def kernel(x_prompt, x_sample, cache_l0_k, cache_l0_v, cache_l1_ckv, cache_l1_kpe, cache_l2_k, cache_l2_v, cache_l3_k, cache_l3_v, c, c_ctx, ada_w, ada_b, norm_mix, norm_ffn, norm_final, na_w_qkv, na_w_o, na_rel_bias, mla_wq_a, mla_q_norm, mla_wq_b, mla_wkv_a, mla_kv_norm, mla_wkv_b, mla_w_o, gqa_w_qkv, gqa_w_o, gqa_sink, ffn_w_gate, ffn_w_up, ffn_w_down, moe_router, moe_w_gate, moe_w_up, moe_w_down):
    x = jnp.concatenate([x_prompt.reshape(N_PROMPT, D_MODEL), x_sample.reshape(N_SAMPLE, D_MODEL)], axis=0)
    cond = jnp.concatenate([c_ctx[None, :], c, jnp.zeros((N_COND - 1 - DEC_BATCH, D_MODEL), F32)], axis=0)
    mods = _adaln(cond, ada_w, ada_b).reshape(DEPTH, N_COND, 6, 1, D_MODEL)
    tri = jnp.asarray(np.tril(np.ones((ROW_TILE, ROW_TILE), np.float32), -1), BF16)
    no_sink = jnp.zeros((GQA_HEADS,), F32)
    na_caches = {0: (cache_l0_k, cache_l0_v), 3: (cache_l3_k, cache_l3_v)}
    state = []

    for i in range(DEPTH):
        sh_m, sc_m, g_m, sh_f, sc_f, g_f = [mods[i, :, j] for j in range(6)]
        gain_m = norm_mix[i][None, :]
        gain_f = norm_ffn[i][None, :]
        kind, s = i % 3, i // 3
        if kind == 0:
            qw = NA_HEADS * NA_HEAD_DIM
            qkv, k_rows, v_rows = _norm_proj(x, gain_m, sh_m, sc_m, na_w_qkv[s].astype(BF16), qw)
            o_p = _ctx_attn(qkv, no_sink, qw, False)
            ck, cv = na_caches[i]
            o_s = _na_latent_attn(qkv, ck.reshape(DEC_BATCH, PAST_LEN, qw), cv.reshape(DEC_BATCH, PAST_LEN, qw),
                                  _na_pair_table(na_rel_bias[s]))
            w_o = na_w_o[s].astype(BF16)
            kv_shape = (BATCH, SEQ, NA_HEADS, NA_HEAD_DIM)
            state += [k_rows.reshape(kv_shape), v_rows.reshape(kv_shape)]
        elif kind == 1:
            wb, wbs, wa_tail, wkvb = _mla_layouts(mla_wq_b[s], mla_wkv_a[s], mla_wkv_b[s])
            wa = jnp.concatenate([mla_wq_a[s], wa_tail], axis=1).astype(BF16)
            cos, sin = _rope_tables(MLA_ROPE, 2)
            q, ckv, kpe = _mla_proj(x, gain_m, sh_m, sc_m, wa, mla_q_norm[s][None, :], mla_kv_norm[s][None, :],
                                    wb.astype(BF16), wbs.astype(BF16), cos, sin)
            wkvb = wkvb.astype(BF16)
            kv_p = _rows_matmul(ckv[:N_PROMPT], wkvb, BF16)
            o_p = _mla_ctx_attn(q, kv_p, kpe)
            ckv_s = ckv[N_PROMPT:].reshape(DEC_BATCH, DEC_SEQ, MLA_KV_LORA)
            ckv_all = jnp.concatenate([cache_l1_ckv, ckv_s], axis=1).reshape(DEC_BATCH * MLA_KEYS, MLA_KV_LORA)
            kv_s = _rows_matmul(ckv_all, wkvb, BF16)
            kpe_cache = jnp.concatenate(
                [cache_l1_kpe, cache_l1_kpe, jnp.zeros((DEC_BATCH, PAST_LEN, LANES - 2 * MLA_ROPE), F32)], axis=2)
            kpe_all = jnp.concatenate([kpe_cache, kpe[N_PROMPT:].reshape(DEC_BATCH, DEC_SEQ, LANES)], axis=1)
            o_s = _mla_latent_attn(q, kv_s, kpe_all.reshape(DEC_BATCH * MLA_KEYS, LANES))
            w_o = mla_w_o[s].astype(BF16)
            state += [ckv[:N_PROMPT].reshape(BATCH, SEQ, MLA_KV_LORA),
                      kpe[:N_PROMPT, :MLA_ROPE].reshape(BATCH, SEQ, MLA_ROPE)]
        else:
            qw = GQA_HEADS * GQA_HEAD_DIM
            kw = GQA_KV_HEADS * GQA_HEAD_DIM
            w_qkv = jnp.concatenate([_gqa_arrange(gqa_w_qkv[s][:, :qw], 1), gqa_w_qkv[s][:, qw:]], axis=1)
            w_swap = _swap_halves(w_qkv[:, :qw + kw], GQA_HEAD_DIM)
            cos, sin = _rope_tables(GQA_HEAD_DIM, 2)
            qkv, k_rows, v_rows = _norm_proj(x, gain_m, sh_m, sc_m, w_qkv.astype(BF16), kw,
                                             rope=(w_swap.astype(BF16), cos, sin))
            sink = gqa_sink[s]
            o_p = _ctx_attn(qkv, sink, kw, True)
            o_s = _gqa_latent_attn(qkv, sink, cache_l2_k.reshape(DEC_BATCH, PAST_LEN, kw),
                                   cache_l2_v.reshape(DEC_BATCH, PAST_LEN, kw))
            w_o = _gqa_arrange(gqa_w_o[s], 0).astype(BF16)
            kv_shape = (BATCH, SEQ, GQA_KV_HEADS, GQA_HEAD_DIM)
            state += [k_rows.reshape(kv_shape), v_rows.reshape(kv_shape)]
        s = i // 2
        if i % 2 == 0:
            x = _ffn(x, o_p, o_s, g_m, w_o, gain_f, sh_f, sc_f, g_f, _chunk_cols(ffn_w_gate[s], FFN_CHUNK),
                     _chunk_cols(ffn_w_up[s], FFN_CHUNK),
                     ffn_w_down[s].reshape(N_FFN_CHUNKS, FFN_CHUNK, D_MODEL).astype(BF16))
        else:
            x = _moe(s, x, o_p, o_s, g_m, w_o, gain_f, sh_f, sc_f, g_f, moe_router[s],
                     moe_w_gate, moe_w_up, moe_w_down, tri)

    y_prompt = _final_norm(x, norm_final[None, :], 0, N_PROMPT).reshape(BATCH, SEQ, D_MODEL)
    y_sample = _final_norm(x, norm_final[None, :], N_PROMPT, N_SAMPLE).reshape(DEC_BATCH, DEC_SEQ, D_MODEL)
    return (y_prompt, y_sample, *state)
```
